```python
import math
import jax, jax.numpy as jnp
from jax import lax
import numpy as np

D_MODEL = 1024
BATCH = 4
SEQ = 8192
DEPTH = 2

GRID_W = 64
CTX_LEN = 256
EPS = 1e-6
CHUNK = 64
N_MOD = 9
MASK_VALUE = -1e30
TINY = 1e-20

D_MIX = D_MODEL
D_HG = D_MIX // 4
HG_HEAD_DIM = 64
HG_HEADS = D_HG // HG_HEAD_DIM
D_NA = D_MIX // 4
NA_HEAD_DIM = 64
NA_HEADS = D_NA // NA_HEAD_DIM
WIN_R = 8
WIN_C = 16
D_SSM = D_MIX // 2
SSM_HEAD_DIM = 64
SSM_HEADS = D_SSM // SSM_HEAD_DIM
SSM_STATE = 128
SSM_GROUPS = 2
CONV_W = 5
CONV_DIM = D_SSM + 2 * SSM_GROUPS * SSM_STATE
D_IN_PROJ = 5 * D_HG + 3 * D_NA + D_SSM + CONV_DIM + 2 * SSM_HEADS
D_FF = ((8 * D_MODEL // 3 + 255) // 256) * 256

kernel_name = 'hybrid_hgrn2_natten_mamba2_macaron_dit'


def rms_norm(x, w):
    xf = x.astype(jnp.float32)
    y = xf * lax.rsqrt(jnp.mean(xf * xf, axis=-1, keepdims=True) + EPS)
    return (y * w.astype(jnp.float32)).astype(x.dtype)


def modulate(x, w, shift, scale):
    return rms_norm(x, w) * (1 + scale[..., None, :]) + shift[..., None, :]


def swiglu(h, w13, w2):
    up, gate = jnp.split(h @ w13, 2, axis=-1)
    return (jax.nn.silu(gate) * up) @ w2


def flip(t):
    return jnp.flip(t, axis=1)


def split_columns(p):
    sizes = [D_HG] * 5 + [D_NA] * 3 + [D_SSM, CONV_DIM, SSM_HEADS, SSM_HEADS]
    return jnp.split(p, np.cumsum(sizes)[:-1].tolist(), axis=-1)


def hgrn2_forget(f_raw, lb):
    r = f_raw.astype(jnp.float32)
    f = lb + (1.0 - lb) * jax.nn.sigmoid(r)
    log_f = jnp.log(jnp.maximum(f, TINY))
    return log_f, (1.0 - lb) * jax.nn.sigmoid(-r)


def gla_chunk_scan(q, k, v, log_f, s0):
    out_dtype = v.dtype
    q, k, v, log_f = (t.astype(jnp.float32) for t in (q, k, v, log_f))
    b, L, H, _ = q.shape
    nc = L // CHUNK

    def to_chunks(t):
        return jnp.moveaxis(t.reshape(b, nc, CHUNK, H, t.shape[-1]), 1, 0)

    causal = jnp.tril(jnp.ones((CHUNK, CHUNK), dtype=bool))

    def step(s, inp):
        qc, kc, vc, gc = inp
        bcum = jnp.cumsum(gc, axis=1)
        diff = bcum[:, :, None] - bcum[:, None, :]
        decay = jnp.where(causal[None, :, :, None, None], jnp.exp(jnp.minimum(diff, 0.0)), 0.0)
        scores = jnp.einsum('bthd,bshd,btshd->bhts', qc, kc, decay)
        o = (jnp.einsum('bhts,bshv->bthv', scores, vc)
             + jnp.einsum('bthd,bhdv->bthv', qc * jnp.exp(bcum), s))
        last = bcum[:, -1]
        s_new = (jnp.exp(last)[..., None] * s
                 + jnp.einsum('bshd,bshv->bhdv', kc * jnp.exp(jnp.minimum(last[:, None] - bcum, 0.0)), vc))
        return s_new, o

    s_fin, o = lax.scan(step, s0.astype(jnp.float32),
                        (to_chunks(q), to_chunks(k), to_chunks(v), to_chunks(log_f)))
    o = jnp.moveaxis(o, 0, 1).reshape(b, L, H, v.shape[-1])
    return o.astype(out_dtype), s_fin


def hgrn2_prepare(q_raw, f_fwd, f_bwd, i_raw, lb):
    shp = q_raw.shape[:2] + (HG_HEADS, HG_HEAD_DIM)
    q = jax.nn.silu(q_raw).reshape(shp)
    v = i_raw.reshape(shp)
    lf_f, k_f = hgrn2_forget(f_fwd, lb[0])
    lf_b, k_b = hgrn2_forget(f_bwd, lb[1])
    return q, v, lf_f.reshape(shp), k_f.reshape(shp), lf_b.reshape(shp), k_b.reshape(shp)


def gla_bidir(q, v, lf_f, k_f, lf_b, k_b, s0_f, s0_b):
    o_f, s_f = gla_chunk_scan(q, k_f, v, lf_f, s0_f)
    o_b, s_b = gla_chunk_scan(flip(q), flip(k_b), flip(v), flip(lf_b), s0_b)
    return o_f + flip(o_b), s_f, s_b


def hgrn2_output(o, g, w):
    b, L = o.shape[:2]
    y = rms_norm(o, w.reshape(HG_HEADS, HG_HEAD_DIM))
    return y.reshape(b, L, D_HG) * jax.nn.silu(g)


def neighbourhood_attention(q, k, v, k_ctx, v_ctx, rpb):
    b, L, H, Dh = q.shape
    rows = L // GRID_W
    wr = min(WIN_R, rows)
    scale = Dh ** -0.5
    qg = q.reshape(b, rows, GRID_W, H, Dh)
    kg = k.reshape(b, rows, GRID_W, H, Dh)
    vg = v.reshape(b, rows, GRID_W, H, Dh)
    r = jnp.arange(rows)
    r0 = jnp.clip(r - wr // 2, 0, rows - wr)
    row_idx = r0[:, None] + jnp.arange(wr)[None, :]
    k_rows = kg[:, row_idx]
    v_rows = vg[:, row_idx]
    col = jnp.arange(GRID_W)
    c0 = jnp.clip(col - WIN_C // 2, 0, GRID_W - WIN_C)
    col_in = (col[None, :] >= c0[:, None]) & (col[None, :] < c0[:, None] + WIN_C)
    dr_idx = row_idx - r[:, None] + (WIN_R - 1)
    dc_idx = jnp.clip(col[None, :] - col[:, None], -(WIN_C - 1), WIN_C - 1) + (WIN_C - 1)
    bias = rpb[:, dr_idx[:, None, :, None], dc_idx[None, :, None, :]]
    s_loc = jnp.einsum('brqhd,brjkhd->bhrqjk', qg, k_rows).astype(jnp.float32) * scale
    s_loc = jnp.where(col_in[None, None, None, :, None, :],
                      s_loc + bias[None].astype(jnp.float32), MASK_VALUE)
    n_loc = wr * GRID_W
    s_loc = s_loc.reshape(b, H, rows, GRID_W, n_loc)
    s_ctx = jnp.einsum('brqhd,bnhd->bhrqn', qg, k_ctx).astype(jnp.float32) * scale
    p = jax.nn.softmax(jnp.concatenate([s_loc, s_ctx], axis=-1), axis=-1).astype(v.dtype)
    p_loc = p[..., :n_loc].reshape(b, H, rows, GRID_W, wr, GRID_W)
    p_ctx = p[..., n_loc:]
    out = (jnp.einsum('bhrqjk,brjkhd->brqhd', p_loc, v_rows)
           + jnp.einsum('bhrqn,bnhd->brqhd', p_ctx, v_ctx))
    return out.reshape(b, L, H * Dh)


def context_attention(q, k, v):
    b, Lq, H, Dh = q.shape
    s = jnp.einsum('bqhd,bkhd->bhqk', q, k).astype(jnp.float32) * (Dh ** -0.5)
    p = jax.nn.softmax(s, axis=-1).astype(v.dtype)
    return jnp.einsum('bhqk,bkhd->bqhd', p, v).reshape(b, Lq, H * Dh)


def depthwise_conv(u, w, bias):
    out = lax.conv_general_dilated(u, w[:, None, :].astype(u.dtype), window_strides=(1,),
                                   padding=[(CONV_W // 2, CONV_W // 2)],
                                   dimension_numbers=('NWC', 'WIO', 'NWC'),
                                   feature_group_count=u.shape[-1])
    return out + bias


def ssd_chunk_scan(xs, dt, A, Bm, Cm, h0):
    out_dtype = xs.dtype
    xs, dt, A, Bm, Cm = (t.astype(jnp.float32) for t in (xs, dt, A, Bm, Cm))
    b, L, H, P = xs.shape
    G, N = Bm.shape[2], Bm.shape[3]
    R = H // G
    nc = L // CHUNK
    xr = xs.reshape(b, nc, CHUNK, G, R, P)
    dtr = dt.reshape(b, nc, CHUNK, G, R)
    Br = Bm.reshape(b, nc, CHUNK, G, N)
    Cr = Cm.reshape(b, nc, CHUNK, G, N)
    cum = jnp.cumsum(dtr * A.reshape(G, R), axis=2)
    causal = jnp.tril(jnp.ones((CHUNK, CHUNK), dtype=bool))
    seg = cum[:, :, :, None] - cum[:, :, None, :]
    Lm = jnp.where(causal[None, None, :, :, None, None], jnp.exp(jnp.minimum(seg, 0.0)), 0.0)
    cb = jnp.einsum('bctgn,bcsgn->bctsg', Cr, Br)
    y_diag = jnp.einsum('bctsg,bctsgr,bcsgr,bcsgrp->bctgrp', cb, Lm, dtr, xr)
    decay_end = jnp.exp(jnp.minimum(cum[:, :, -1:] - cum, 0.0))
    states = jnp.einsum('bcsgn,bcsgr,bcsgrp->bcgrpn', Br, decay_end * dtr, xr)
    chunk_decay = jnp.exp(cum[:, :, -1])

    def step(h, inp):
        st, dec = inp
        return dec[..., None, None] * h + st, h

    h_fin, h_in = lax.scan(step, h0.astype(jnp.float32),
                           (jnp.moveaxis(states, 1, 0), jnp.moveaxis(chunk_decay, 1, 0)))
    h_in = jnp.moveaxis(h_in, 0, 1)
    y_off = jnp.einsum('bctgn,bcgrpn,bctgr->bctgrp', Cr, h_in, jnp.exp(cum))
    y = (y_diag + y_off).reshape(b, L, H, P)
    return y.astype(out_dtype), h_fin


def mamba2_prepare(z, xbc, dt_f, dt_b, conv_w, conv_b, dt_bias):
    b, L = z.shape[:2]
    xbc = jax.nn.silu(depthwise_conv(xbc, conv_w, conv_b))
    xs, bm, cm = jnp.split(xbc, [D_SSM, D_SSM + SSM_GROUPS * SSM_STATE], axis=-1)
    xs = xs.reshape(b, L, SSM_HEADS, SSM_HEAD_DIM)
    bm = bm.reshape(b, L, SSM_GROUPS, SSM_STATE)
    cm = cm.reshape(b, L, SSM_GROUPS, SSM_STATE)
    dtf = jax.nn.softplus(dt_f.astype(jnp.float32) + dt_bias[0])
    dtb = jax.nn.softplus(dt_b.astype(jnp.float32) + dt_bias[1])
    return z, xs, bm, cm, dtf, dtb


def ssd_bidir(xs, bm, cm, dtf, dtb, A, h0_f, h0_b):
    y_f, h_f = ssd_chunk_scan(xs, dtf, A[0], bm, cm, h0_f)
    y_b, h_b = ssd_chunk_scan(flip(xs), flip(dtb), A[1], flip(bm), flip(cm), h0_b)
    return y_f + flip(y_b), h_f, h_b


def mamba2_output(y, xs, z, d_skip, w):
    b, L = y.shape[:2]
    y = (y + d_skip[:, None] * xs).reshape(b, L, D_SSM)
    return rms_norm(y * jax.nn.silu(z), w)


def token_mixer(h, hc, w_in, w_out, lb, hg_norm_w, rpb, na_norm_w,
                conv_w, conv_b, a_log, dt_bias, d_skip, ssm_norm_w, ctx_out):
    b = h.shape[0]
    lat = split_columns(h @ w_in)
    cpt = split_columns(hc @ w_in)

    q, v, lf_f, k_f, lf_b, k_b = hgrn2_prepare(lat[0], lat[1], lat[2], lat[3], lb)
    qc, vc, lfc_f, kc_f, lfc_b, kc_b = hgrn2_prepare(cpt[0], cpt[1], cpt[2], cpt[3], lb)
    s_zero = jnp.zeros((b, HG_HEADS, HG_HEAD_DIM, HG_HEAD_DIM), jnp.float32)
    oc_hg, s_f, s_b = gla_bidir(qc, vc, lfc_f, kc_f, lfc_b, kc_b, s_zero, s_zero)
    o_hg, _, _ = gla_bidir(q, v, lf_f, k_f, lf_b, k_b, s_f, s_b)
    hg_lat = hgrn2_output(o_hg, lat[4], hg_norm_w)

    def na_heads(t):
        return t.reshape(t.shape[:2] + (NA_HEADS, NA_HEAD_DIM))
    qa, ka, va = na_heads(lat[5]), na_heads(lat[6]), na_heads(lat[7])
    qca, kca, vca = na_heads(cpt[5]), na_heads(cpt[6]), na_heads(cpt[7])
    na_lat = rms_norm(neighbourhood_attention(qa, ka, va, kca, vca, rpb), na_norm_w)

    A = -jnp.exp(a_log.astype(jnp.float32))
    z, xs, bm, cm, dtf, dtb = mamba2_prepare(lat[8], lat[9], lat[10], lat[11], conv_w, conv_b, dt_bias)
    zc, xsc, bmc, cmc, dtfc, dtbc = mamba2_prepare(cpt[8], cpt[9], cpt[10], cpt[11], conv_w, conv_b, dt_bias)
    h_zero = jnp.zeros((b, SSM_GROUPS, SSM_HEADS // SSM_GROUPS, SSM_HEAD_DIM, SSM_STATE), jnp.float32)
    yc_ssm, h_f, h_b = ssd_bidir(xsc, bmc, cmc, dtfc, dtbc, A, h_zero, h_zero)
    y_ssm, _, _ = ssd_bidir(xs, bm, cm, dtf, dtb, A, h_f, h_b)
    ssm_lat = mamba2_output(y_ssm, xs, z, d_skip, ssm_norm_w)

    out_lat = jnp.concatenate([hg_lat, na_lat, ssm_lat], axis=-1) @ w_out
    if not ctx_out:
        return out_lat, None
    hg_c = hgrn2_output(oc_hg, cpt[4], hg_norm_w)
    na_c = rms_norm(context_attention(qca, kca, vca), na_norm_w)
    ssm_c = mamba2_output(yc_ssm, xsc, zc, d_skip, ssm_norm_w)
    out_ctx = jnp.concatenate([hg_c, na_c, ssm_c], axis=-1) @ w_out
    return out_lat, out_ctx


def setup_inputs(seed: int = 0) -> dict:
    key = jax.random.key(seed)
    ks = jax.random.split(key, 32)
    D = D_MODEL

    def nrm(k, shape, scale):
        return jax.random.normal(k, shape, jnp.float32) * scale

    def gain(k, shape):
        return 1.0 + 0.05 * jax.random.normal(k, shape, jnp.float32)

    dt0 = jnp.exp(jax.random.uniform(ks[20], (DEPTH, 2, SSM_HEADS), jnp.float32,
                                     minval=math.log(1e-3), maxval=math.log(1e-1)))
    dt_bias = dt0 + jnp.log(-jnp.expm1(-dt0))
    a_log = jnp.log(jax.random.uniform(ks[21], (DEPTH, 2, SSM_HEADS), jnp.float32, minval=1.0, maxval=16.0))
    return {
        'x': nrm(ks[0], (BATCH, SEQ, D), 1.0),
        'c': nrm(ks[1], (BATCH, D), 1.0),
        'ctx': nrm(ks[2], (BATCH, CTX_LEN, D), 1.0),
        'c_ctx': nrm(ks[3], (D,), 1.0),
        'w_mod': nrm(ks[4], (DEPTH, D, N_MOD * D), D ** -0.5),
        'b_mod': nrm(ks[5], (DEPTH, N_MOD * D), 0.02),
        'norm_ffn1': gain(ks[6], (DEPTH, D)),
        'ffn1_w13': nrm(ks[7], (DEPTH, D, 2 * D_FF), D ** -0.5),
        'ffn1_w2': nrm(ks[8], (DEPTH, D_FF, D), D_FF ** -0.5),
        'norm_mix': gain(ks[9], (DEPTH, D)),
        'w_in': nrm(ks[10], (DEPTH, D, D_IN_PROJ), D ** -0.5),
        'hg_lower_bounds': nrm(ks[11], (2, DEPTH, D_HG), 0.5),
        'hg_norm': gain(ks[12], (DEPTH, D_HG)),
        'na_rpb': nrm(ks[13], (DEPTH, NA_HEADS, 2 * WIN_R - 1, 2 * WIN_C - 1), 0.3),
        'na_norm': gain(ks[14], (DEPTH, D_NA)),
        'ssm_conv_w': nrm(ks[15], (DEPTH, CONV_W, CONV_DIM), CONV_W ** -0.5),
        'ssm_conv_b': nrm(ks[16], (DEPTH, CONV_DIM), 0.02),
        'ssm_a_log': a_log,
        'ssm_dt_bias': dt_bias,
        'ssm_d': gain(ks[17], (DEPTH, SSM_HEADS)),
        'ssm_norm': gain(ks[18], (DEPTH, D_SSM)),
        'w_out': nrm(ks[19], (DEPTH, D_MIX, D), D_MIX ** -0.5),
        'norm_ffn2': gain(ks[22], (DEPTH, D)),
        'ffn2_w13': nrm(ks[23], (DEPTH, D, 2 * D_FF), D ** -0.5),
        'ffn2_w2': nrm(ks[24], (DEPTH, D_FF, D), D_FF ** -0.5),
        'final_norm': gain(ks[25], (D,)),
    }


def reference(x, c, ctx, c_ctx, w_mod, b_mod, norm_ffn1, ffn1_w13, ffn1_w2, norm_mix, w_in,
              hg_lower_bounds, hg_norm, na_rpb, na_norm, ssm_conv_w, ssm_conv_b, ssm_a_log,
              ssm_dt_bias, ssm_d, ssm_norm, w_out, norm_ffn2, ffn2_w13, ffn2_w2, final_norm):
    lb_soft = jax.nn.softmax(hg_lower_bounds.astype(jnp.float32), axis=1)
    lower_bounds = jnp.cumsum(lb_soft, axis=1) - lb_soft[:, :1]
    sc = jax.nn.silu(c)
    scc = jax.nn.silu(c_ctx)
    xc = ctx
    for layer in range(DEPTH):
        last = layer == DEPTH - 1
        mod = jnp.split(sc @ w_mod[layer] + b_mod[layer], N_MOD, axis=-1)
        modc = jnp.split(scc @ w_mod[layer] + b_mod[layer], N_MOD, axis=-1)

        x = x + 0.5 * mod[2][:, None, :] * swiglu(
            modulate(x, norm_ffn1[layer], mod[0], mod[1]), ffn1_w13[layer], ffn1_w2[layer])
        xc = xc + 0.5 * modc[2] * swiglu(
            modulate(xc, norm_ffn1[layer], modc[0], modc[1]), ffn1_w13[layer], ffn1_w2[layer])

        h = modulate(x, norm_mix[layer], mod[3], mod[4])
        hc = modulate(xc, norm_mix[layer], modc[3], modc[4])
        mix, mix_c = token_mixer(h, hc, w_in[layer], w_out[layer], lower_bounds[:, layer],
                                 hg_norm[layer], na_rpb[layer], na_norm[layer],
                                 ssm_conv_w[layer], ssm_conv_b[layer], ssm_a_log[layer],
                                 ssm_dt_bias[layer], ssm_d[layer], ssm_norm[layer],
                                 ctx_out=not last)
        x = x + mod[5][:, None, :] * mix
        if not last:
            xc = xc + modc[5] * mix_c
            xc = xc + 0.5 * modc[8] * swiglu(
                modulate(xc, norm_ffn2[layer], modc[6], modc[7]), ffn2_w13[layer], ffn2_w2[layer])

        x = x + 0.5 * mod[8][:, None, :] * swiglu(
            modulate(x, norm_ffn2[layer], mod[6], mod[7]), ffn2_w13[layer], ffn2_w2[layer])
    return rms_norm(x, final_norm)
```

```python
import functools
import math

import jax
import jax.numpy as jnp
from jax import lax
from jax.experimental import pallas as pl
from jax.experimental.pallas import tpu as pltpu

F32 = jnp.float32
BF16 = jnp.bfloat16

D_MODEL = 1024
GRID_W = 64
CTX_LEN = 256
EPS = 1e-6
N_MOD = 9
MASK_VALUE = -1e30
TINY = 1e-20

D_HG = 256
HG_HEADS = 4
HG_DK = 64
D_NA = 256
NA_HEADS = 4
NA_DH = 64
WIN_R = 8
WIN_C = 16
D_SSM = 512
SSM_HEADS = 8
SSM_P = 64
SSM_N = 128
SSM_G = 2
CONV_W = 5
CONV_DIM = D_SSM + 2 * SSM_G * SSM_N
D_FF = 2816
D_IN_PAD = 3712
LANES = 128
MOD_ROWS = 8

GLA_TILE = 256
GLA_SUB = 16
SSD_Q = 128
CONV_TILE = 256
NA_ROWS = 8

VMEM_LIMIT = 56 * 1024 * 1024


def _cparams(sem):
    return pltpu.CompilerParams(dimension_semantics=sem, vmem_limit_bytes=VMEM_LIMIT)


def _silu(x):
    return x * jax.nn.sigmoid(x)


def _split3(x):
    hi = x.astype(BF16)
    r1 = x - hi.astype(F32)
    mid = r1.astype(BF16)
    lo = (r1 - mid.astype(F32)).astype(BF16)
    return hi, mid, lo


def _ldot3(a_bf16, x):
    hi, mid, lo = _split3(x)
    return (jnp.dot(a_bf16, hi, preferred_element_type=F32)
            + jnp.dot(a_bf16, mid, preferred_element_type=F32)
            + jnp.dot(a_bf16, lo, preferred_element_type=F32))


def _rdot3(x, e_bf16):
    hi, mid, lo = _split3(x)
    return (jnp.dot(hi, e_bf16, preferred_element_type=F32)
            + jnp.dot(mid, e_bf16, preferred_element_type=F32)
            + jnp.dot(lo, e_bf16, preferred_element_type=F32))


def _dot_nt(a, b):
    return lax.dot_general(a, b, (((1,), (1,)), ((), ())), preferred_element_type=F32)


def _dot_tn(a, b):
    return lax.dot_general(a, b, (((0,), (0,)), ((), ())), preferred_element_type=F32)


def _modulated(x, nw, shift, scale):
    ms = jnp.mean(x * x, axis=-1, keepdims=True)
    y = x * lax.rsqrt(ms + EPS)
    return (y * nw) * (1.0 + scale) + shift


def _mods_kernel(c_ref, w_ref, b_ref, o_ref):
    sc = _silu(c_ref[...]).astype(BF16)
    o_ref[...] = jnp.dot(sc, w_ref[...].astype(BF16), preferred_element_type=F32) + b_ref[...]


def _mods_call(c_rows, w_mod, b_mod):
    depth = w_mod.shape[0]
    tn = 1152
    n = N_MOD * D_MODEL
    return pl.pallas_call(
        _mods_kernel,
        grid=(depth, n // tn),
        in_specs=[
            pl.BlockSpec((MOD_ROWS, D_MODEL), lambda l, j: (0, 0)),
            pl.BlockSpec((None, D_MODEL, tn), lambda l, j: (l, 0, j)),
            pl.BlockSpec((None, 1, tn), lambda l, j: (l, 0, j)),
        ],
        out_specs=pl.BlockSpec((None, MOD_ROWS, tn), lambda l, j: (l, 0, j)),
        out_shape=jax.ShapeDtypeStruct((depth, MOD_ROWS, n), F32),
        compiler_params=_cparams(("parallel", "parallel")),
        name="mods",
    )(c_rows, w_mod, b_mod.reshape(depth, 1, n))


def _mod_spec(tm, seq, nb):
    tpb = seq // tm
    return pl.BlockSpec((None, N_MOD, D_MODEL), lambda i: (jnp.minimum(i // tpb, nb), 0, 0))


def _ffn_kernel(x_ref, mod_ref, nw_ref, wu_ref, wg_ref, w2_ref, fw_ref, o_ref, act_ref, *,
                k_shift, k_scale, k_gate, final):
    x = x_ref[...]
    h = _modulated(x, nw_ref[...], mod_ref[k_shift:k_shift + 1, :],
                   mod_ref[k_scale:k_scale + 1, :]).astype(BF16)
    cw = 256
    for c in range(D_FF // cw):
        u = jnp.dot(h, wu_ref[:, c * cw:(c + 1) * cw], preferred_element_type=F32)
        g = jnp.dot(h, wg_ref[:, c * cw:(c + 1) * cw], preferred_element_type=F32)
        act_ref[:, c * cw:(c + 1) * cw] = (_silu(g) * u).astype(BF16)
    y = jnp.dot(act_ref[...], w2_ref[...], preferred_element_type=F32)
    out = x + (0.5 * mod_ref[k_gate:k_gate + 1, :]) * y
    if final:
        ms = jnp.mean(out * out, axis=-1, keepdims=True)
        out = (out * lax.rsqrt(ms + EPS)) * fw_ref[...]
    o_ref[...] = out


def _ffn_call(x_all, mods_l, nw, w13b, w2b, fw, *, n_rows, seq, nb, ks, final, tm=512):
    kern = functools.partial(_ffn_kernel, k_shift=ks[0], k_scale=ks[1], k_gate=ks[2], final=final)
    return pl.pallas_call(
        kern,
        grid=(n_rows // tm,),
        in_specs=[
            pl.BlockSpec((tm, D_MODEL), lambda i: (i, 0)),
            _mod_spec(tm, seq, nb),
            pl.BlockSpec((1, D_MODEL), lambda i: (0, 0)),
            pl.BlockSpec((D_MODEL, D_FF), lambda i: (0, 0), pipeline_mode=pl.Buffered(1)),
            pl.BlockSpec((D_MODEL, D_FF), lambda i: (0, 1), pipeline_mode=pl.Buffered(1)),
            pl.BlockSpec((D_FF, D_MODEL), lambda i: (0, 0), pipeline_mode=pl.Buffered(1)),
            pl.BlockSpec((1, D_MODEL), lambda i: (0, 0)),
        ],
        out_specs=pl.BlockSpec((tm, D_MODEL), lambda i: (i, 0)),
        out_shape=jax.ShapeDtypeStruct((n_rows, D_MODEL), F32),
        scratch_shapes=[pltpu.VMEM((tm, D_FF), BF16)],
        compiler_params=_cparams(("parallel",)),
        name="ffn",
    )(x_all, mods_l, nw, w13b, w13b, w2b, fw)


_IN_SPLITS = ((0, 1280), (1280, 2048), (2048, 2560), (2560, 3584), (3584, 3712))


def _inproj_kernel(x_ref, mod_ref, nw_ref, w_ref, hg_ref, na_ref, z_ref, xbc_ref, dt_ref):
    h = _modulated(x_ref[...], nw_ref[...], mod_ref[3:4, :], mod_ref[4:5, :]).astype(BF16)
    outs = (hg_ref, na_ref, z_ref, xbc_ref, dt_ref)
    for (a, b), o_ref in zip(_IN_SPLITS, outs):
        o_ref[...] = jnp.dot(h, w_ref[:, a:b], preferred_element_type=F32).astype(o_ref.dtype)


def _inproj_call(x_all, mods_l, nw, w_in_b, *, n_rows, seq, nb, tm=512):
    widths = [b - a for a, b in _IN_SPLITS]
    dtypes = [F32, BF16, F32, F32, F32]
    return pl.pallas_call(
        _inproj_kernel,
        grid=(n_rows // tm,),
        in_specs=[
            pl.BlockSpec((tm, D_MODEL), lambda i: (i, 0)),
            _mod_spec(tm, seq, nb),
            pl.BlockSpec((1, D_MODEL), lambda i: (0, 0)),
            pl.BlockSpec((D_MODEL, D_IN_PAD), lambda i: (0, 0), pipeline_mode=pl.Buffered(1)),
        ],
        out_specs=[pl.BlockSpec((tm, w), lambda i: (i, 0)) for w in widths],
        out_shape=[jax.ShapeDtypeStruct((n_rows, w), dt) for w, dt in zip(widths, dtypes)],
        compiler_params=_cparams(("parallel",)),
        name="inproj",
    )(x_all, mods_l, nw, w_in_b)


def _conv_kernel(prev_ref, cur_ref, next_ref, w_ref, b_ref, o_ref, ext_ref, *, n_lat_tiles, tpb):
    i = pl.program_id(0)
    is_ctx = i >= n_lat_tiles
    first = jnp.logical_or(is_ctx, (i % tpb) == 0)
    last = jnp.logical_or(is_ctx, (i % tpb) == tpb - 1)
    tq = cur_ref.shape[0]
    ext_ref[0:8, :] = jnp.where(first, 0.0, prev_ref[...])
    ext_ref[8:8 + tq, :] = cur_ref[...]
    ext_ref[8 + tq:16 + tq, :] = jnp.where(last, 0.0, next_ref[...])
    acc = jnp.zeros((tq, CONV_DIM), F32) + b_ref[...]
    for j in range(CONV_W):
        off = 8 - CONV_W // 2 + j
        acc = acc + ext_ref[off:off + tq, :] * w_ref[j:j + 1, :]
    o_ref[...] = _silu(acc)


def _conv_call(xbc, conv_w, conv_b, *, n_rows, seq, nb):
    tq = CONV_TILE
    n_lat_tiles = nb * seq // tq
    kern = functools.partial(_conv_kernel, n_lat_tiles=n_lat_tiles, tpb=seq // tq)
    r8 = tq // 8
    last8 = n_rows // 8 - 1
    return pl.pallas_call(
        kern,
        grid=(n_rows // tq,),
        in_specs=[
            pl.BlockSpec((8, CONV_DIM), lambda i: (jnp.maximum(i * r8 - 1, 0), 0)),
            pl.BlockSpec((tq, CONV_DIM), lambda i: (i, 0)),
            pl.BlockSpec((8, CONV_DIM), lambda i: (jnp.minimum((i + 1) * r8, last8), 0)),
            pl.BlockSpec((CONV_W, CONV_DIM), lambda i: (0, 0)),
            pl.BlockSpec((1, CONV_DIM), lambda i: (0, 0)),
        ],
        out_specs=pl.BlockSpec((tq, CONV_DIM), lambda i: (i, 0)),
        out_shape=jax.ShapeDtypeStruct((n_rows, CONV_DIM), F32),
        scratch_shapes=[pltpu.VMEM((tq + 16, CONV_DIM), F32)],
        compiler_params=_cparams(("parallel",)),
        name="conv",
    )(xbc, xbc, xbc, conv_w, conv_b)


def _scan_block_map(*, seq, nb, q, rev):
    nc_ctx = CTX_LEN // q
    nc_lat = seq // q
    ctx_base = nb * seq // q

    def row_block(b, j):
        cj = j
        lj = j - nc_ctx
        if rev:
            cj = nc_ctx - 1 - cj
            lj = nc_lat - 1 - lj
        return jnp.where(j < nc_ctx, ctx_base + b * nc_ctx + cj, b * nc_lat + lj)

    return row_block, nc_ctx + nc_lat


def _ssd_kernel(xbc_ref, dt_ref, arow_ref, brow_ref, e_ref, tri_ref, y_ref, st_ref, *, rev, lane0):
    q = SSD_Q
    j = pl.program_id(1)

    @pl.when(j == 0)
    def _():
        st_ref[...] = jnp.zeros_like(st_ref)

    xs = xbc_ref[:, 0:D_SSM]
    dt = jax.nn.softplus(dt_ref[...] + brow_ref[...])
    a = dt * arow_ref[...]
    cum = _ldot3(tri_ref[...], a)
    cum_t = cum.T
    e = e_ref[...]
    cum_e = _rdot3(cum, e)
    dt_e = _rdot3(dt, e)
    edge = 0 if rev else q - 1
    tot_e = cum_e[edge:edge + 1, :]
    xdt = xs * dt_e

    t_idx = lax.broadcasted_iota(jnp.int32, (q, q), 0)
    s_idx = lax.broadcasted_iota(jnp.int32, (q, q), 1)
    causal = (s_idx >= t_idx) if rev else (s_idx <= t_idx)
    lane = lax.broadcasted_iota(jnp.int32, (q, LANES), 1)
    low_half = lane < SSM_P

    y_parts = []
    for g in range(SSM_G):
        bg = xbc_ref[:, D_SSM + g * SSM_N:D_SSM + (g + 1) * SSM_N].astype(BF16)
        cg = xbc_ref[:, D_SSM + SSM_G * SSM_N + g * SSM_N:D_SSM + SSM_G * SSM_N + (g + 1) * SSM_N].astype(BF16)
        cb = _dot_nt(cg, bg)
        gl = g * 256
        y_off = jnp.dot(cg, st_ref[:, gl:gl + 256].astype(BF16), preferred_element_type=F32)
        for hp in range(2):
            pl0 = gl + hp * LANES
            xpair = xdt[:, pl0:pl0 + LANES]
            acc = jnp.zeros((q, LANES), F32)
            for hh in range(2):
                idx = lane0 + g * 4 + hp * 2 + hh
                diff = cum[:, idx:idx + 1] - cum_t[idx:idx + 1, :]
                m = cb * jnp.exp(jnp.where(causal, diff, MASK_VALUE))
                xm = jnp.where(low_half if hh == 0 else jnp.logical_not(low_half), xpair, 0.0)
                acc = acc + jnp.dot(m.astype(BF16), xm.astype(BF16), preferred_element_type=F32)
            y_parts.append(acc + y_off[:, hp * LANES:(hp + 1) * LANES] * jnp.exp(cum_e[:, pl0:pl0 + LANES]))
        xw = (xdt[:, gl:gl + 256] * jnp.exp(tot_e[:, gl:gl + 256] - cum_e[:, gl:gl + 256])).astype(BF16)
        st_ref[:, gl:gl + 256] = (st_ref[:, gl:gl + 256] * jnp.exp(tot_e[:, gl:gl + 256])
                                  + _dot_tn(bg, xw))
    y_ref[...] = jnp.concatenate(y_parts, axis=1)


def _ssd_call(xbc_c, dt_raw, a_row, b_row, e_mat, tri, *, n_rows, seq, nb, rev, lane0):
    q = SSD_Q
    row_block, steps = _scan_block_map(seq=seq, nb=nb, q=q, rev=rev)
    kern = functools.partial(_ssd_kernel, rev=rev, lane0=lane0)
    return pl.pallas_call(
        kern,
        grid=(nb, steps),
        in_specs=[
            pl.BlockSpec((q, CONV_DIM), lambda b, j: (row_block(b, j), 0)),
            pl.BlockSpec((q, LANES), lambda b, j: (row_block(b, j), 0)),
            pl.BlockSpec((1, LANES), lambda b, j: (0, 0)),
            pl.BlockSpec((1, LANES), lambda b, j: (0, 0)),
            pl.BlockSpec((LANES, D_SSM), lambda b, j: (0, 0)),
            pl.BlockSpec((q, q), lambda b, j: (0, 0)),
        ],
        out_specs=pl.BlockSpec((q, D_SSM), lambda b, j: (row_block(b, j), 0)),
        out_shape=jax.ShapeDtypeStruct((n_rows, D_SSM), F32),
        scratch_shapes=[pltpu.VMEM((SSM_N, D_SSM), F32)],
        compiler_params=_cparams(("arbitrary", "arbitrary")),
        name="ssd_bwd" if rev else "ssd_fwd",
    )(xbc_c, dt_raw, a_row, b_row, e_mat, tri)


def _gla_kernel(q_ref, f_ref, v_ref, lb_ref, tri_ref, ones_ref, ind_ref, o_ref,
                q_s, k_s, b_s, qt_s, kt_s, dec_s, p_s, st_ref, *, rev):
    t = GLA_TILE
    c = GLA_SUB
    j = pl.program_id(1)

    @pl.when(j == 0)
    def _():
        st_ref[...] = jnp.zeros_like(st_ref)

    lb = lb_ref[...]
    fr = f_ref[...]
    f = lb + (1.0 - lb) * jax.nn.sigmoid(fr)
    logf = jnp.log(jnp.maximum(f, TINY))
    k = (1.0 - lb) * jax.nn.sigmoid(-fr)
    qv = _silu(q_ref[...])
    brel = _ldot3(tri_ref[...], logf)
    tot = _ldot3(ones_ref[...], logf)
    q_s[...] = qv
    k_s[...] = k
    b_s[...] = brel
    qt_s[...] = (qv * jnp.exp(brel)).astype(BF16)
    kt_s[...] = k * jnp.exp(tot - brel)
    dec_s[...] = jnp.exp(tot)

    t_idx = lax.broadcasted_iota(jnp.int32, (c, D_HG), 0)
    lane = lax.broadcasted_iota(jnp.int32, (c, D_HG), 1)
    head_masks = [(lane >= h * HG_DK) & (lane < (h + 1) * HG_DK) for h in range(HG_HEADS)]
    ind = ind_ref[...]
    nsub = t // c

    def body(i, carry):
        blk = (nsub - 1 - i) if rev else i
        r0 = pl.multiple_of(blk * c, c)
        qb = q_s[pl.ds(r0, c), :]
        kb = k_s[pl.ds(r0, c), :]
        bb = b_s[pl.ds(r0, c), :]
        vb = v_ref[pl.ds(r0, c), :]
        st = st_ref[...]
        o = _dot_nt(qt_s[pl.ds(r0, c), :], st.astype(BF16))
        for s in range(c):
            keep = (t_idx <= s) if rev else (t_idx >= s)
            dec = jnp.exp(jnp.where(keep, bb - bb[s:s + 1, :], MASK_VALUE))
            p_s[s * c:(s + 1) * c, :] = (qb * kb[s:s + 1, :] * dec).astype(BF16)
        r = jnp.dot(p_s[...], ind, preferred_element_type=F32)
        for s in range(c):
            o = o + r[s * c:(s + 1) * c, :] * vb[s:s + 1, :]
        o_ref[pl.ds(r0, c), :] = o
        ktb = kt_s[pl.ds(r0, c), :]
        kx = jnp.concatenate([jnp.where(m, ktb, 0.0) for m in head_masks], axis=0).astype(BF16)
        vx = jnp.concatenate([jnp.where(m, vb, 0.0) for m in head_masks], axis=0).astype(BF16)
        st_ref[...] = st * dec_s[pl.ds(r0, 1), :] + _dot_tn(vx, kx)
        return carry

    lax.fori_loop(0, nsub, body, 0)


def _gla_call(hg, lb_row, tri, ones, ind, *, n_rows, seq, nb, rev):
    t = GLA_TILE
    row_block, steps = _scan_block_map(seq=seq, nb=nb, q=t, rev=rev)
    fcol = 2 if rev else 1
    kern = functools.partial(_gla_kernel, rev=rev)
    const = lambda b, j: (0, 0)
    return pl.pallas_call(
        kern,
        grid=(nb, steps),
        in_specs=[
            pl.BlockSpec((t, D_HG), lambda b, j: (row_block(b, j), 0)),
            pl.BlockSpec((t, D_HG), lambda b, j: (row_block(b, j), fcol)),
            pl.BlockSpec((t, D_HG), lambda b, j: (row_block(b, j), 3)),
            pl.BlockSpec((1, D_HG), const),
            pl.BlockSpec((t, t), const),
            pl.BlockSpec((t, t), const),
            pl.BlockSpec((D_HG, D_HG), const),
        ],
        out_specs=pl.BlockSpec((t, D_HG), lambda b, j: (row_block(b, j), 0)),
        out_shape=jax.ShapeDtypeStruct((n_rows, D_HG), F32),
        scratch_shapes=[
            pltpu.VMEM((t, D_HG), F32), pltpu.VMEM((t, D_HG), F32), pltpu.VMEM((t, D_HG), F32),
            pltpu.VMEM((t, D_HG), BF16), pltpu.VMEM((t, D_HG), F32), pltpu.VMEM((t, D_HG), F32),
            pltpu.VMEM((GLA_SUB * GLA_SUB, D_HG), BF16),
            pltpu.VMEM((D_HG, D_HG), F32),
        ],
        compiler_params=_cparams(("arbitrary", "arbitrary")),
        name="gla_bwd" if rev else "gla_fwd",
    )(hg, hg, hg, lb_row, tri, ones, ind)


def _expand_heads(x, masks):
    return jnp.concatenate([jnp.where(m, x, jnp.zeros_like(x)) for m in masks], axis=0)


def _collapse_heads(o, masks, t):
    acc = jnp.where(masks[0], o[0:t, :], 0.0)
    for h in range(1, NA_HEADS):
        acc = acc + jnp.where(masks[h], o[h * t:(h + 1) * t, :], 0.0)
    return acc


def _na_finish(o, nw):
    ms = jnp.mean(o * o, axis=-1, keepdims=True)
    return (o * lax.rsqrt(ms + EPS)) * nw


def _na_kernel(q_ref, k_ref, v_ref, kc_ref, vc_ref, bias_ref, nw_ref, o_ref, *, n_grid_rows):
    w = GRID_W
    i = pl.program_id(1)
    lane = lax.broadcasted_iota(jnp.int32, (w, D_NA), 1)
    masks = [(lane >= h * NA_DH) & (lane < (h + 1) * NA_DH) for h in range(NA_HEADS)]
    kc = kc_ref[...]
    vc = vc_ref[...]
    nw = nw_ref[...]
    scale = NA_DH ** -0.5

    def body(jr, carry):
        r = i * NA_ROWS + jr
        r0 = jnp.clip(r - WIN_R // 2, 0, n_grid_rows - WIN_R)
        var = r - r0
        k0 = pl.multiple_of(r0 * w, w)
        q0 = pl.multiple_of(jr * w, w)
        qx = _expand_heads(q_ref[pl.ds(q0, w), :] * scale, masks)
        kw = k_ref[pl.ds(k0, WIN_R * w), :]
        vw = v_ref[pl.ds(k0, WIN_R * w), :]
        s_loc = _dot_nt(qx, kw) + bias_ref[var]
        s_ctx = _dot_nt(qx, kc)
        m = jnp.maximum(jnp.max(s_loc, axis=-1, keepdims=True), jnp.max(s_ctx, axis=-1, keepdims=True))
        p_loc = jnp.exp(s_loc - m)
        p_ctx = jnp.exp(s_ctx - m)
        den = jnp.sum(p_loc, axis=-1, keepdims=True) + jnp.sum(p_ctx, axis=-1, keepdims=True)
        o = (jnp.dot(p_loc.astype(BF16), vw, preferred_element_type=F32)
             + jnp.dot(p_ctx.astype(BF16), vc, preferred_element_type=F32)) / den
        o_ref[pl.ds(q0, w), :] = _na_finish(_collapse_heads(o, masks, w), nw)
        return carry

    lax.fori_loop(0, NA_ROWS, body, 0)


def _na_call(na, bias, nw, *, seq, nb):
    rows = seq // GRID_W
    tq = NA_ROWS * GRID_W
    qpb = seq // tq
    ctx_blk = nb * seq // CTX_LEN
    kern = functools.partial(_na_kernel, n_grid_rows=rows)
    return pl.pallas_call(
        kern,
        grid=(nb, qpb),
        in_specs=[
            pl.BlockSpec((tq, D_NA), lambda b, i: (b * qpb + i, 0)),
            pl.BlockSpec((seq, D_NA), lambda b, i: (b, 1)),
            pl.BlockSpec((seq, D_NA), lambda b, i: (b, 2)),
            pl.BlockSpec((CTX_LEN, D_NA), lambda b, i: (ctx_blk + b, 1)),
            pl.BlockSpec((CTX_LEN, D_NA), lambda b, i: (ctx_blk + b, 2)),
            pl.BlockSpec((WIN_R, NA_HEADS * GRID_W, WIN_R * GRID_W), lambda b, i: (0, 0, 0)),
            pl.BlockSpec((1, D_NA), lambda b, i: (0, 0)),
        ],
        out_specs=pl.BlockSpec((tq, D_NA), lambda b, i: (b * qpb + i, 0)),
        out_shape=jax.ShapeDtypeStruct((nb * seq, D_NA), F32),
        compiler_params=_cparams(("parallel", "arbitrary")),
        name="natten",
    )(na, na, na, na, na, bias, nw)


def _ctxattn_kernel(q_ref, k_ref, v_ref, nw_ref, o_ref):
    t = CTX_LEN
    lane = lax.broadcasted_iota(jnp.int32, (t, D_NA), 1)
    masks = [(lane >= h * NA_DH) & (lane < (h + 1) * NA_DH) for h in range(NA_HEADS)]
    qx = _expand_heads(q_ref[...] * (NA_DH ** -0.5), masks)
    s = _dot_nt(qx, k_ref[...])
    m = jnp.max(s, axis=-1, keepdims=True)
    p = jnp.exp(s - m)
    den = jnp.sum(p, axis=-1, keepdims=True)
    o = jnp.dot(p.astype(BF16), v_ref[...], preferred_element_type=F32) / den
    o_ref[...] = _na_finish(_collapse_heads(o, masks, t), nw_ref[...])


def _ctxattn_call(na, nw, *, seq, nb):
    ctx_blk = nb * seq // CTX_LEN
    return pl.pallas_call(
        _ctxattn_kernel,
        grid=(nb,),
        in_specs=[
            pl.BlockSpec((CTX_LEN, D_NA), lambda b: (ctx_blk + b, 0)),
            pl.BlockSpec((CTX_LEN, D_NA), lambda b: (ctx_blk + b, 1)),
            pl.BlockSpec((CTX_LEN, D_NA), lambda b: (ctx_blk + b, 2)),
            pl.BlockSpec((1, D_NA), lambda b: (0, 0)),
        ],
        out_specs=pl.BlockSpec((CTX_LEN, D_NA), lambda b: (b, 0)),
        out_shape=jax.ShapeDtypeStruct((nb * CTX_LEN, D_NA), F32),
        compiler_params=_cparams(("parallel",)),
        name="ctxattn",
    )(na, na, na, nw)


def _outproj_kernel(x_ref, mod_ref, of_ref, ob_ref, g_ref, na_ref, yf_ref, yb_ref, xs_ref, z_ref,
                    hgw_ref, hm_ref, dsk_ref, sw_ref, wo_ref, o_ref):
    o = of_ref[...] + ob_ref[...]
    sq = o * o
    hi = sq.astype(BF16)
    lo = (sq - hi.astype(F32)).astype(BF16)
    hm = hm_ref[...]
    ms = (jnp.dot(hi, hm, preferred_element_type=F32)
          + jnp.dot(lo, hm, preferred_element_type=F32)) * (1.0 / HG_DK)
    hg = (o * lax.rsqrt(ms + EPS)) * hgw_ref[...] * _silu(g_ref[...])
    ys = (yf_ref[...] + yb_ref[...] + dsk_ref[...] * xs_ref[...]) * _silu(z_ref[...])
    ms2 = jnp.mean(ys * ys, axis=-1, keepdims=True)
    ssm = (ys * lax.rsqrt(ms2 + EPS)) * sw_ref[...]
    mix = (jnp.dot(hg.astype(BF16), wo_ref[0:D_HG, :], preferred_element_type=F32)
           + jnp.dot(na_ref[...].astype(BF16), wo_ref[D_HG:D_HG + D_NA, :], preferred_element_type=F32)
           + jnp.dot(ssm.astype(BF16), wo_ref[D_HG + D_NA:, :], preferred_element_type=F32))
    o_ref[...] = x_ref[...] + mod_ref[5:6, :] * mix


def _outproj_call(x_all, mods_l, o_f, o_b, hg, y_na, y_f, y_b, xbc_c, z, hgw, hm, dsk, sw, wo_b,
                  *, n_rows, seq, nb, tm=512):
    row = lambda i: (i, 0)
    const = lambda i: (0, 0)
    return pl.pallas_call(
        _outproj_kernel,
        grid=(n_rows // tm,),
        in_specs=[
            pl.BlockSpec((tm, D_MODEL), row),
            _mod_spec(tm, seq, nb),
            pl.BlockSpec((tm, D_HG), row),
            pl.BlockSpec((tm, D_HG), row),
            pl.BlockSpec((tm, D_HG), lambda i: (i, 4)),
            pl.BlockSpec((tm, D_NA), row),
            pl.BlockSpec((tm, D_SSM), row),
            pl.BlockSpec((tm, D_SSM), row),
            pl.BlockSpec((tm, D_SSM), row),
            pl.BlockSpec((tm, D_SSM), row),
            pl.BlockSpec((1, D_HG), const),
            pl.BlockSpec((D_HG, D_HG), const),
            pl.BlockSpec((1, D_SSM), const),
            pl.BlockSpec((1, D_SSM), const),
            pl.BlockSpec((D_MODEL, D_MODEL), const),
        ],
        out_specs=pl.BlockSpec((tm, D_MODEL), row),
        out_shape=jax.ShapeDtypeStruct((n_rows, D_MODEL), F32),
        compiler_params=_cparams(("parallel",)),
        name="outproj",
    )(x_all, mods_l, o_f, o_b, hg, y_na, y_f, y_b, xbc_c, z, hgw, hm, dsk, sw, wo_b)


def _na_bias_table(rpb):
    w = GRID_W
    var = jnp.arange(WIN_R)[:, None]
    jrow = jnp.arange(WIN_R)[None, :]
    dr = jrow - var + (WIN_R - 1)
    col = jnp.arange(w)
    c0 = jnp.clip(col - WIN_C // 2, 0, w - WIN_C)
    col_in = (col[None, :] >= c0[:, None]) & (col[None, :] < c0[:, None] + WIN_C)
    dc = jnp.clip(col[None, :] - col[:, None], -(WIN_C - 1), WIN_C - 1) + (WIN_C - 1)
    b = rpb[:, dr[:, None, :, None], dc[None, :, None, :]]
    b = jnp.where(col_in[None, None, :, None, :], b.astype(F32), MASK_VALUE)
    b = jnp.transpose(b, (1, 0, 2, 3, 4))
    return b.reshape(WIN_R, NA_HEADS * w, WIN_R * w)


def _block_tri(n, c, rev):
    r = jnp.arange(n)
    same = (r[:, None] // c) == (r[None, :] // c)
    tri = (r[None, :] >= r[:, None]) if rev else (r[None, :] <= r[:, None])
    return (same & tri).astype(BF16), same.astype(BF16)


def _head_block_ones(n, hd):
    r = jnp.arange(n)
    return ((r[:, None] // hd) == (r[None, :] // hd)).astype(BF16)


def _ssd_expand(lane0):
    r = jnp.arange(LANES)[:, None]
    cidx = jnp.arange(D_SSM)[None, :]
    return (r == lane0 + cidx // SSM_P).astype(BF16)


def _lane_row(vals, lane0):
    return jnp.zeros((1, LANES), F32).at[0, lane0:lane0 + vals.shape[0]].set(vals.astype(F32))


def kernel(x, c, ctx, c_ctx, w_mod, b_mod, norm_ffn1, ffn1_w13, ffn1_w2, norm_mix, w_in,
           hg_lower_bounds, hg_norm, na_rpb, na_norm, ssm_conv_w, ssm_conv_b, ssm_a_log,
           ssm_dt_bias, ssm_d, ssm_norm, w_out, norm_ffn2, ffn2_w13, ffn2_w2, final_norm):
    nb, seq, d = x.shape
    depth = w_mod.shape[0]
    assert d == D_MODEL and ctx.shape[1] == CTX_LEN and nb + 1 <= MOD_ROWS
    assert seq % 512 == 0 and seq // GRID_W >= WIN_R
    n_lat = nb * seq
    n_all = n_lat + nb * CTX_LEN

    lb_soft = jax.nn.softmax(hg_lower_bounds.astype(F32), axis=1)
    lower_bounds = jnp.cumsum(lb_soft, axis=1) - lb_soft[:, :1]

    c_rows = jnp.zeros((MOD_ROWS, d), F32).at[:nb].set(c).at[nb].set(c_ctx)
    mods = _mods_call(c_rows, w_mod, b_mod).reshape(depth, MOD_ROWS, N_MOD, d)

    x_all = jnp.concatenate([x.reshape(n_lat, d), ctx.reshape(nb * CTX_LEN, d)], axis=0)

    tri_f, ones_g = _block_tri(GLA_TILE, GLA_SUB, False)
    tri_b, _ = _block_tri(GLA_TILE, GLA_SUB, True)
    ind = _head_block_ones(D_HG, HG_DK)
    ssd_tri_f, _ = _block_tri(SSD_Q, SSD_Q, False)
    ssd_tri_b, _ = _block_tri(SSD_Q, SSD_Q, True)
    e_f = _ssd_expand(0)
    e_b = _ssd_expand(SSM_HEADS)
    one_row = lambda v: v.reshape(1, -1).astype(F32)
    common = dict(seq=seq, nb=nb)

    for layer in range(depth):
        last = layer == depth - 1
        mods_l = mods[layer]
        w13_1 = ffn1_w13[layer].astype(BF16)
        w2_1 = ffn1_w2[layer].astype(BF16)
        w13_2 = ffn2_w13[layer].astype(BF16)
        w2_2 = ffn2_w2[layer].astype(BF16)
        w_in_b = jnp.pad(w_in[layer], ((0, 0), (0, D_IN_PAD - w_in.shape[2]))).astype(BF16)
        wo_b = w_out[layer].astype(BF16)
        fw = one_row(final_norm)

        x_all = _ffn_call(x_all, mods_l, one_row(norm_ffn1[layer]), w13_1, w2_1, fw,
                          n_rows=n_all, ks=(0, 1, 2), final=False, **common)
        hg, na, z, xbc, dt_raw = _inproj_call(x_all, mods_l, one_row(norm_mix[layer]), w_in_b,
                                              n_rows=n_all, **common)

        o_f = _gla_call(hg, one_row(lower_bounds[0, layer]), tri_f, ones_g, ind,
                        n_rows=n_all, rev=False, **common)
        o_b = _gla_call(hg, one_row(lower_bounds[1, layer]), tri_b, ones_g, ind,
                        n_rows=n_all, rev=True, **common)

        bias = _na_bias_table(na_rpb[layer])
        nw = one_row(na_norm[layer])
        y_na = _na_call(na, bias, nw, **common)
        if not last:
            y_na = jnp.concatenate([y_na, _ctxattn_call(na, nw, **common)], axis=0)

        xbc_c = _conv_call(xbc, ssm_conv_w[layer].astype(F32), one_row(ssm_conv_b[layer]),
                           n_rows=n_all, **common)
        a_neg = -jnp.exp(ssm_a_log[layer].astype(F32))
        y_f = _ssd_call(xbc_c, dt_raw, _lane_row(a_neg[0], 0), _lane_row(ssm_dt_bias[layer, 0], 0),
                        e_f, ssd_tri_f, n_rows=n_all, rev=False, lane0=0, **common)
        y_b = _ssd_call(xbc_c, dt_raw, _lane_row(a_neg[1], SSM_HEADS),
                        _lane_row(ssm_dt_bias[layer, 1], SSM_HEADS),
                        e_b, ssd_tri_b, n_rows=n_all, rev=True, lane0=SSM_HEADS, **common)

        n_out = n_lat if last else n_all
        dsk = jnp.repeat(ssm_d[layer].astype(F32), SSM_P).reshape(1, D_SSM)
        x_all = _outproj_call(x_all, mods_l, o_f, o_b, hg, y_na, y_f, y_b, xbc_c, z,
                              one_row(hg_norm[layer]), ind, dsk, one_row(ssm_norm[layer]), wo_b,
                              n_rows=n_out, **common)
        x_all = _ffn_call(x_all, mods_l, one_row(norm_ffn2[layer]), w13_2, w2_2, fw,
                          n_rows=n_out, ks=(6, 7, 8), final=last, **common)

    return x_all.reshape(nb, seq, d)
```

```python
import functools
import math

import jax
import jax.numpy as jnp
import numpy as np
from jax import lax
from jax.experimental import pallas as pl
from jax.experimental.pallas import tpu as pltpu

F32 = jnp.float32
BF16 = jnp.bfloat16

D_MODEL = 1024
GRID_W = 64
CTX_LEN = 256
EPS = 1e-6
N_MOD = 9
MASK_VALUE = -1e30
TINY = 1e-20

D_HG = 256
HG_HEADS = 4
HG_DK = 64
D_NA = 256
NA_HEADS = 4
NA_DH = 64
WIN_R = 8
WIN_C = 16
D_SSM = 512
SSM_HEADS = 8
SSM_P = 64
SSM_N = 128
SSM_G = 2
CONV_W = 5
CONV_DIM = D_SSM + 2 * SSM_G * SSM_N
D_FF = 2816
D_IN_PAD = 3712
LANES = 128
MOD_ROWS = 8

GLA_TILE = 256
GLA_SUB = 16
GLA_P_ROWS = (GLA_TILE // GLA_SUB) * (GLA_SUB * GLA_SUB * 3 // 4)
SSD_Q = 128
SSD_STEP = 256
CONV_TILE = 256
NA_ROWS = 8

VMEM_LIMIT = 56 * 1024 * 1024


def _cparams(sem):
    return pltpu.CompilerParams(dimension_semantics=sem, vmem_limit_bytes=VMEM_LIMIT)


def _silu(x):
    return x * jax.nn.sigmoid(x)


def _split3(x):
    hi = x.astype(BF16)
    r1 = x - hi.astype(F32)
    mid = r1.astype(BF16)
    lo = (r1 - mid.astype(F32)).astype(BF16)
    return hi, mid, lo


def _ldot3(a_bf16, x):
    hi, mid, lo = _split3(x)
    return (jnp.dot(a_bf16, hi, preferred_element_type=F32)
            + jnp.dot(a_bf16, mid, preferred_element_type=F32)
            + jnp.dot(a_bf16, lo, preferred_element_type=F32))


def _rdot3(x, e_bf16):
    hi, mid, lo = _split3(x)
    return (jnp.dot(hi, e_bf16, preferred_element_type=F32)
            + jnp.dot(mid, e_bf16, preferred_element_type=F32)
            + jnp.dot(lo, e_bf16, preferred_element_type=F32))


def _dot_nt(a, b):
    return lax.dot_general(a, b, (((1,), (1,)), ((), ())), preferred_element_type=F32)


def _dot_tn(a, b):
    return lax.dot_general(a, b, (((0,), (0,)), ((), ())), preferred_element_type=F32)


def _modulated(x, nw, shift, scale):
    ms = jnp.mean(x * x, axis=-1, keepdims=True)
    y = x * lax.rsqrt(ms + EPS)
    return (y * nw) * (1.0 + scale) + shift


def _mods_kernel(c_ref, w_ref, b_ref, o_ref):
    sc = _silu(c_ref[...]).astype(BF16)
    o_ref[...] = jnp.dot(sc, w_ref[...].astype(BF16), preferred_element_type=F32) + b_ref[...]


def _mods_call(c_rows, w_mod, b_mod):
    depth = w_mod.shape[0]
    tn = 1152
    n = N_MOD * D_MODEL
    return pl.pallas_call(
        _mods_kernel,
        grid=(depth, n // tn),
        in_specs=[
            pl.BlockSpec((MOD_ROWS, D_MODEL), lambda l, j: (0, 0)),
            pl.BlockSpec((None, D_MODEL, tn), lambda l, j: (l, 0, j)),
            pl.BlockSpec((None, 1, tn), lambda l, j: (l, 0, j)),
        ],
        out_specs=pl.BlockSpec((None, MOD_ROWS, tn), lambda l, j: (l, 0, j)),
        out_shape=jax.ShapeDtypeStruct((depth, MOD_ROWS, n), F32),
        compiler_params=_cparams(("parallel", "parallel")),
        name="mods",
    )(c_rows, w_mod, b_mod.reshape(depth, 1, n))


def _mod_spec(tm, seq, nb):
    tpb = seq // tm
    return pl.BlockSpec((None, N_MOD, D_MODEL), lambda i: (jnp.minimum(i // tpb, nb), 0, 0))


def _ffn_kernel(x_ref, mod_ref, nw_ref, wu_ref, wg_ref, w2_ref, fw_ref, o_ref, act_ref, *,
                k_shift, k_scale, k_gate, final):
    x = x_ref[...]
    h = _modulated(x, nw_ref[...], mod_ref[k_shift:k_shift + 1, :],
                   mod_ref[k_scale:k_scale + 1, :]).astype(BF16)
    cw = 256
    for c in range(D_FF // cw):
        u = jnp.dot(h, wu_ref[:, c * cw:(c + 1) * cw], preferred_element_type=F32)
        g = jnp.dot(h, wg_ref[:, c * cw:(c + 1) * cw], preferred_element_type=F32)
        act_ref[:, c * cw:(c + 1) * cw] = (_silu(g) * u).astype(BF16)
    y = jnp.dot(act_ref[...], w2_ref[...], preferred_element_type=F32)
    out = x + (0.5 * mod_ref[k_gate:k_gate + 1, :]) * y
    if final:
        ms = jnp.mean(out * out, axis=-1, keepdims=True)
        out = (out * lax.rsqrt(ms + EPS)) * fw_ref[...]
    o_ref[...] = out


def _ffn_call(x_all, mods_l, nw, w13b, w2b, fw, *, n_rows, seq, nb, ks, final, tm=512):
    kern = functools.partial(_ffn_kernel, k_shift=ks[0], k_scale=ks[1], k_gate=ks[2], final=final)
    return pl.pallas_call(
        kern,
        grid=(n_rows // tm,),
        in_specs=[
            pl.BlockSpec((tm, D_MODEL), lambda i: (i, 0)),
            _mod_spec(tm, seq, nb),
            pl.BlockSpec((1, D_MODEL), lambda i: (0, 0)),
            pl.BlockSpec((D_MODEL, D_FF), lambda i: (0, 0), pipeline_mode=pl.Buffered(1)),
            pl.BlockSpec((D_MODEL, D_FF), lambda i: (0, 1), pipeline_mode=pl.Buffered(1)),
            pl.BlockSpec((D_FF, D_MODEL), lambda i: (0, 0), pipeline_mode=pl.Buffered(1)),
            pl.BlockSpec((1, D_MODEL), lambda i: (0, 0)),
        ],
        out_specs=pl.BlockSpec((tm, D_MODEL), lambda i: (i, 0)),
        out_shape=jax.ShapeDtypeStruct((n_rows, D_MODEL), F32),
        scratch_shapes=[pltpu.VMEM((tm, D_FF), BF16)],
        compiler_params=_cparams(("parallel",)),
        name="ffn",
    )(x_all, mods_l, nw, w13b, w13b, w2b, fw)


_IN_SPLITS = ((0, 1280), (1280, 2048), (2048, 2560), (2560, 3584), (3584, 3712))


def _inproj_kernel(x_ref, mod_ref, nw_ref, w_ref, hg_ref, na_ref, z_ref, xbc_ref, dt_ref):
    h = _modulated(x_ref[...], nw_ref[...], mod_ref[3:4, :], mod_ref[4:5, :]).astype(BF16)
    outs = (hg_ref, na_ref, z_ref, xbc_ref, dt_ref)
    for (a, b), o_ref in zip(_IN_SPLITS, outs):
        o_ref[...] = jnp.dot(h, w_ref[:, a:b], preferred_element_type=F32).astype(o_ref.dtype)


def _inproj_call(x_all, mods_l, nw, w_in_b, *, n_rows, seq, nb, tm=512):
    widths = [b - a for a, b in _IN_SPLITS]
    dtypes = [F32, BF16, F32, F32, F32]
    return pl.pallas_call(
        _inproj_kernel,
        grid=(n_rows // tm,),
        in_specs=[
            pl.BlockSpec((tm, D_MODEL), lambda i: (i, 0)),
            _mod_spec(tm, seq, nb),
            pl.BlockSpec((1, D_MODEL), lambda i: (0, 0)),
            pl.BlockSpec((D_MODEL, D_IN_PAD), lambda i: (0, 0), pipeline_mode=pl.Buffered(1)),
        ],
        out_specs=[pl.BlockSpec((tm, w), lambda i: (i, 0)) for w in widths],
        out_shape=[jax.ShapeDtypeStruct((n_rows, w), dt) for w, dt in zip(widths, dtypes)],
        compiler_params=_cparams(("parallel",)),
        name="inproj",
    )(x_all, mods_l, nw, w_in_b)


def _conv_kernel(prev_ref, cur_ref, next_ref, w_ref, b_ref, o_ref, ext_ref, *, n_lat_tiles, tpb):
    i = pl.program_id(0)
    is_ctx = i >= n_lat_tiles
    first = jnp.logical_or(is_ctx, (i % tpb) == 0)
    last = jnp.logical_or(is_ctx, (i % tpb) == tpb - 1)
    tq = cur_ref.shape[0]
    ext_ref[0:8, :] = jnp.where(first, 0.0, prev_ref[...])
    ext_ref[8:8 + tq, :] = cur_ref[...]
    ext_ref[8 + tq:16 + tq, :] = jnp.where(last, 0.0, next_ref[...])
    acc = jnp.zeros((tq, CONV_DIM), F32) + b_ref[...]
    for j in range(CONV_W):
        off = 8 - CONV_W // 2 + j
        acc = acc + ext_ref[off:off + tq, :] * w_ref[j:j + 1, :]
    o_ref[...] = _silu(acc)


def _conv_call(xbc, conv_w, conv_b, *, n_rows, seq, nb):
    tq = CONV_TILE
    n_lat_tiles = nb * seq // tq
    kern = functools.partial(_conv_kernel, n_lat_tiles=n_lat_tiles, tpb=seq // tq)
    r8 = tq // 8
    last8 = n_rows // 8 - 1
    return pl.pallas_call(
        kern,
        grid=(n_rows // tq,),
        in_specs=[
            pl.BlockSpec((8, CONV_DIM), lambda i: (jnp.maximum(i * r8 - 1, 0), 0)),
            pl.BlockSpec((tq, CONV_DIM), lambda i: (i, 0)),
            pl.BlockSpec((8, CONV_DIM), lambda i: (jnp.minimum((i + 1) * r8, last8), 0)),
            pl.BlockSpec((CONV_W, CONV_DIM), lambda i: (0, 0)),
            pl.BlockSpec((1, CONV_DIM), lambda i: (0, 0)),
        ],
        out_specs=pl.BlockSpec((tq, CONV_DIM), lambda i: (i, 0)),
        out_shape=jax.ShapeDtypeStruct((n_rows, CONV_DIM), F32),
        scratch_shapes=[pltpu.VMEM((tq + 16, CONV_DIM), F32)],
        compiler_params=_cparams(("parallel",)),
        name="conv",
    )(xbc, xbc, xbc, conv_w, conv_b)


def _scan_block_map(*, seq, nb, q, rev):
    nc_ctx = CTX_LEN // q
    nc_lat = seq // q
    ctx_base = nb * seq // q

    def row_block(b, j):
        cj = j
        lj = j - nc_ctx
        if rev:
            cj = nc_ctx - 1 - cj
            lj = nc_lat - 1 - lj
        return jnp.where(j < nc_ctx, ctx_base + b * nc_ctx + cj, b * nc_lat + lj)

    return row_block, nc_ctx + nc_lat


def _ssd_kernel(xbc_ref, dt_ref, arow_ref, brow_ref, e_ref, tri_ref, y_ref, st_ref, *, rev, lane0):
    q = SSD_Q
    nch = SSD_STEP // q
    gw = D_SSM // SSM_G
    j = pl.program_id(1)

    @pl.when(j == 0)
    def _():
        st_ref[...] = jnp.zeros_like(st_ref)

    t_idx = lax.broadcasted_iota(jnp.int32, (q, q), 0)
    s_idx = lax.broadcasted_iota(jnp.int32, (q, q), 1)
    causal = (s_idx >= t_idx) if rev else (s_idx <= t_idx)
    lane = lax.broadcasted_iota(jnp.int32, (q, LANES), 1)
    low_half = lane < SSM_P
    e = e_ref[...]
    tri = tri_ref[...]
    edge = 0 if rev else q - 1
    st = [st_ref[:, g * gw:(g + 1) * gw] for g in range(SSM_G)]

    for ci in (range(nch - 1, -1, -1) if rev else range(nch)):
        rows = slice(ci * q, (ci + 1) * q)
        xs = xbc_ref[rows, 0:D_SSM]
        dt = jax.nn.softplus(dt_ref[rows, :] + brow_ref[...])
        a = dt * arow_ref[...]
        cum = _ldot3(tri, a)
        cum_t = cum.T
        cum_e = _rdot3(cum, e)
        dt_e = _rdot3(dt, e)
        tot_e = cum_e[edge:edge + 1, :]
        xdt = xs * dt_e
        y_parts = []
        for g in range(SSM_G):
            b0 = D_SSM + g * SSM_N
            c0 = D_SSM + SSM_G * SSM_N + g * SSM_N
            bg = xbc_ref[rows, b0:b0 + SSM_N].astype(BF16)
            cg = xbc_ref[rows, c0:c0 + SSM_N].astype(BF16)
            cb = _dot_nt(cg, bg)
            gl = g * gw
            y_off = jnp.dot(cg, st[g].astype(BF16), preferred_element_type=F32)
            for hp in range(2):
                pl0 = gl + hp * LANES
                xpair = xdt[:, pl0:pl0 + LANES]
                acc = jnp.zeros((q, LANES), F32)
                for hh in range(2):
                    idx = lane0 + g * 4 + hp * 2 + hh
                    diff = cum[:, idx:idx + 1] - cum_t[idx:idx + 1, :]
                    m = cb * jnp.exp(jnp.where(causal, diff, MASK_VALUE))
                    xm = jnp.where(low_half if hh == 0 else jnp.logical_not(low_half), xpair, 0.0)
                    acc = acc + jnp.dot(m.astype(BF16), xm.astype(BF16), preferred_element_type=F32)
                y_parts.append(acc + y_off[:, hp * LANES:(hp + 1) * LANES] * jnp.exp(cum_e[:, pl0:pl0 + LANES]))
            xw = (xdt[:, gl:gl + gw] * jnp.exp(tot_e[:, gl:gl + gw] - cum_e[:, gl:gl + gw])).astype(BF16)
            st[g] = st[g] * jnp.exp(tot_e[:, gl:gl + gw]) + _dot_tn(bg, xw)
        y_ref[rows, :] = jnp.concatenate(y_parts, axis=1)
    for g in range(SSM_G):
        st_ref[:, g * gw:(g + 1) * gw] = st[g]


def _ssd_call(xbc_c, dt_raw, a_row, b_row, e_mat, tri, *, n_rows, seq, nb, rev, lane0):
    q = SSD_STEP
    row_block, steps = _scan_block_map(seq=seq, nb=nb, q=q, rev=rev)
    kern = functools.partial(_ssd_kernel, rev=rev, lane0=lane0)
    return pl.pallas_call(
        kern,
        grid=(nb, steps),
        in_specs=[
            pl.BlockSpec((q, CONV_DIM), lambda b, j: (row_block(b, j), 0)),
            pl.BlockSpec((q, LANES), lambda b, j: (row_block(b, j), 0)),
            pl.BlockSpec((1, LANES), lambda b, j: (0, 0)),
            pl.BlockSpec((1, LANES), lambda b, j: (0, 0)),
            pl.BlockSpec((LANES, D_SSM), lambda b, j: (0, 0)),
            pl.BlockSpec((SSD_Q, SSD_Q), lambda b, j: (0, 0)),
        ],
        out_specs=pl.BlockSpec((q, D_SSM), lambda b, j: (row_block(b, j), 0)),
        out_shape=jax.ShapeDtypeStruct((n_rows, D_SSM), F32),
        scratch_shapes=[pltpu.VMEM((SSM_N, D_SSM), F32)],
        compiler_params=_cparams(("arbitrary", "arbitrary")),
        name="ssd_bwd" if rev else "ssd_fwd",
    )(xbc_c, dt_raw, a_row, b_row, e_mat, tri)


def _gla_kernel(q_ref, f_ref, v_ref, lb_ref, tri_ref, ones_ref, ind_ref, o_ref, p_s, upd_s, st_ref, *, rev):
    t = GLA_TILE
    c = GLA_SUB
    nsub = t // c
    half = c // 2
    j = pl.program_id(1)

    @pl.when(j == 0)
    def _():
        st_ref[...] = jnp.zeros_like(st_ref)

    lb = lb_ref[...]
    fr = f_ref[...]
    f = lb + (1.0 - lb) * jax.nn.sigmoid(fr)
    logf = jnp.log(jnp.maximum(f, TINY))
    k = (1.0 - lb) * jax.nn.sigmoid(-fr)
    qv = _silu(q_ref[...])
    vv = v_ref[...]
    brel = _ldot3(tri_ref[...], logf)
    tot = _ldot3(ones_ref[...], logf)

    qt = (qv * jnp.exp(brel)).astype(BF16)
    kt = k * jnp.exp(tot - brel)
    dec_blk = jnp.exp(tot)
    lane = lax.broadcasted_iota(jnp.int32, (t, D_HG), 1)
    head_masks = [(lane >= h * HG_DK) & (lane < (h + 1) * HG_DK) for h in range(HG_HEADS)]
    kxs = [jnp.where(m, kt, 0.0).astype(BF16) for m in head_masks]
    vxs = [jnp.where(m, vv, 0.0).astype(BF16) for m in head_masks]
    for blk in range(nsub):
        r0 = blk * c
        kx = jnp.concatenate([a[r0:r0 + c, :] for a in kxs], axis=0)
        vx = jnp.concatenate([a[r0:r0 + c, :] for a in vxs], axis=0)
        upd_s[blk] = _dot_tn(vx, kx)

    t_idx = lax.broadcasted_iota(jnp.int32, (c, D_HG), 0)
    full_keys = [s for s in range(c) if (s >= half if rev else s < half)]
    half_keys = [s for s in range(c) if s not in full_keys]
    lo = 0 if rev else half
    units = [(s,) for s in full_keys] + [tuple(half_keys[i:i + 2]) for i in range(0, half, 2)]
    rows_per_blk = len(units) * c

    t_idx_half = lax.broadcasted_iota(jnp.int32, (half, D_HG), 0) + lo

    def pair_rows(qb, kb, bb, s, r_lo, r_hi):
        tt = t_idx if r_hi - r_lo == c else t_idx_half
        keep = (tt <= s) if rev else (tt >= s)
        dec = jnp.exp(jnp.where(keep, bb[r_lo:r_hi, :] - bb[s:s + 1, :], MASK_VALUE))
        return qb[r_lo:r_hi, :] * kb[s:s + 1, :] * dec

    for blk in range(nsub):
        r0 = blk * c
        qb = qv[r0:r0 + c, :]
        kb = k[r0:r0 + c, :]
        bb = brel[r0:r0 + c, :]
        for u, keys in enumerate(units):
            if len(keys) == 1:
                p = pair_rows(qb, kb, bb, keys[0], 0, c)
            else:
                p = jnp.concatenate([pair_rows(qb, kb, bb, s, lo, lo + half) for s in keys], axis=0)
            po = blk * rows_per_blk + u * c
            p_s[po:po + c, :] = p.astype(BF16)
    r = jnp.dot(p_s[...], ind_ref[...], preferred_element_type=F32)
    o_blocks = []
    for blk in range(nsub):
        r0 = blk * c
        vb = vv[r0:r0 + c, :]
        acc_full = jnp.zeros((c, D_HG), F32)
        acc_half = jnp.zeros((half, D_HG), F32)
        for u, keys in enumerate(units):
            po = blk * rows_per_blk + u * c
            rs = r[po:po + c, :]
            if len(keys) == 1:
                acc_full = acc_full + rs * vb[keys[0]:keys[0] + 1, :]
            else:
                acc_half = (acc_half + rs[:half, :] * vb[keys[0]:keys[0] + 1, :]
                            + rs[half:, :] * vb[keys[1]:keys[1] + 1, :])
        zeros_half = jnp.zeros((half, D_HG), F32)
        pieces = [acc_half, zeros_half] if rev else [zeros_half, acc_half]
        o_blocks.append(acc_full + jnp.concatenate(pieces, axis=0))
    o_intra = jnp.concatenate(o_blocks, axis=0)

    st = st_ref[...]
    o_inter = [None] * nsub
    for blk in (range(nsub - 1, -1, -1) if rev else range(nsub)):
        r0 = blk * c
        o_inter[blk] = _dot_nt(qt[r0:r0 + c, :], st.astype(BF16))
        st = st * dec_blk[r0:r0 + 1, :] + upd_s[blk]
    st_ref[...] = st
    o_ref[...] = o_intra + jnp.concatenate(o_inter, axis=0)


def _gla_call(hg, lb_row, tri, ones, ind, *, n_rows, seq, nb, rev):
    t = GLA_TILE
    row_block, steps = _scan_block_map(seq=seq, nb=nb, q=t, rev=rev)
    fcol = 2 if rev else 1
    kern = functools.partial(_gla_kernel, rev=rev)
    const = lambda b, j: (0, 0)
    return pl.pallas_call(
        kern,
        grid=(nb, steps),
        in_specs=[
            pl.BlockSpec((t, D_HG), lambda b, j: (row_block(b, j), 0)),
            pl.BlockSpec((t, D_HG), lambda b, j: (row_block(b, j), fcol)),
            pl.BlockSpec((t, D_HG), lambda b, j: (row_block(b, j), 3)),
            pl.BlockSpec((1, D_HG), const),
            pl.BlockSpec((t, t), const),
            pl.BlockSpec((t, t), const),
            pl.BlockSpec((D_HG, D_HG), const),
        ],
        out_specs=pl.BlockSpec((t, D_HG), lambda b, j: (row_block(b, j), 0)),
        out_shape=jax.ShapeDtypeStruct((n_rows, D_HG), F32),
        scratch_shapes=[
            pltpu.VMEM((GLA_P_ROWS, D_HG), BF16),
            pltpu.VMEM((GLA_TILE // GLA_SUB, D_HG, D_HG), F32),
            pltpu.VMEM((D_HG, D_HG), F32),
        ],
        compiler_params=_cparams(("arbitrary", "arbitrary")),
        name="gla_bwd" if rev else "gla_fwd",
    )(hg, hg, hg, lb_row, tri, ones, ind)


def _expand_heads(x, masks):
    return jnp.concatenate([jnp.where(m, x, jnp.zeros_like(x)) for m in masks], axis=0)


def _collapse_heads(o, masks, t):
    acc = jnp.where(masks[0], o[0:t, :], 0.0)
    for h in range(1, NA_HEADS):
        acc = acc + jnp.where(masks[h], o[h * t:(h + 1) * t, :], 0.0)
    return acc


def _na_finish(o, nw):
    ms = jnp.mean(o * o, axis=-1, keepdims=True)
    return (o * lax.rsqrt(ms + EPS)) * nw


def _na_kernel(q_ref, k_ref, v_ref, kc_ref, vc_ref, bias_ref, nw_ref, o_ref, *, n_grid_rows):
    w = GRID_W
    i = pl.program_id(1)
    lane = lax.broadcasted_iota(jnp.int32, (w, D_NA), 1)
    masks = [(lane >= h * NA_DH) & (lane < (h + 1) * NA_DH) for h in range(NA_HEADS)]
    kc = kc_ref[...]
    vc = vc_ref[...]
    nw = nw_ref[...]
    scale = NA_DH ** -0.5

    def body(jr, carry):
        r = i * NA_ROWS + jr
        r0 = jnp.clip(r - WIN_R // 2, 0, n_grid_rows - WIN_R)
        var = r - r0
        k0 = pl.multiple_of(r0 * w, w)
        q0 = pl.multiple_of(jr * w, w)
        qx = _expand_heads(q_ref[pl.ds(q0, w), :] * scale, masks)
        kw = k_ref[pl.ds(k0, WIN_R * w), :]
        vw = v_ref[pl.ds(k0, WIN_R * w), :]
        s_loc = _dot_nt(qx, kw) + bias_ref[var]
        s_ctx = _dot_nt(qx, kc)
        m = jnp.maximum(jnp.max(s_loc, axis=-1, keepdims=True), jnp.max(s_ctx, axis=-1, keepdims=True))
        p_loc = jnp.exp(s_loc - m)
        p_ctx = jnp.exp(s_ctx - m)
        den = jnp.sum(p_loc, axis=-1, keepdims=True) + jnp.sum(p_ctx, axis=-1, keepdims=True)
        o = (jnp.dot(p_loc.astype(BF16), vw, preferred_element_type=F32)
             + jnp.dot(p_ctx.astype(BF16), vc, preferred_element_type=F32)) / den
        o_ref[pl.ds(q0, w), :] = _na_finish(_collapse_heads(o, masks, w), nw)
        return carry

    lax.fori_loop(0, NA_ROWS, body, 0, unroll=True)


def _na_call(na, bias, nw, *, seq, nb):
    rows = seq // GRID_W
    tq = NA_ROWS * GRID_W
    qpb = seq // tq
    ctx_blk = nb * seq // CTX_LEN
    kern = functools.partial(_na_kernel, n_grid_rows=rows)
    return pl.pallas_call(
        kern,
        grid=(nb, qpb),
        in_specs=[
            pl.BlockSpec((tq, D_NA), lambda b, i: (b * qpb + i, 0)),
            pl.BlockSpec((seq, D_NA), lambda b, i: (b, 1)),
            pl.BlockSpec((seq, D_NA), lambda b, i: (b, 2)),
            pl.BlockSpec((CTX_LEN, D_NA), lambda b, i: (ctx_blk + b, 1)),
            pl.BlockSpec((CTX_LEN, D_NA), lambda b, i: (ctx_blk + b, 2)),
            pl.BlockSpec((WIN_R, NA_HEADS * GRID_W, WIN_R * GRID_W), lambda b, i: (0, 0, 0)),
            pl.BlockSpec((1, D_NA), lambda b, i: (0, 0)),
        ],
        out_specs=pl.BlockSpec((tq, D_NA), lambda b, i: (b * qpb + i, 0)),
        out_shape=jax.ShapeDtypeStruct((nb * seq, D_NA), F32),
        compiler_params=_cparams(("parallel", "arbitrary")),
        name="natten",
    )(na, na, na, na, na, bias, nw)


def _ctxattn_kernel(q_ref, k_ref, v_ref, nw_ref, o_ref):
    t = CTX_LEN
    lane = lax.broadcasted_iota(jnp.int32, (t, D_NA), 1)
    masks = [(lane >= h * NA_DH) & (lane < (h + 1) * NA_DH) for h in range(NA_HEADS)]
    qx = _expand_heads(q_ref[...] * (NA_DH ** -0.5), masks)
    s = _dot_nt(qx, k_ref[...])
    m = jnp.max(s, axis=-1, keepdims=True)
    p = jnp.exp(s - m)
    den = jnp.sum(p, axis=-1, keepdims=True)
    o = jnp.dot(p.astype(BF16), v_ref[...], preferred_element_type=F32) / den
    o_ref[...] = _na_finish(_collapse_heads(o, masks, t), nw_ref[...])


def _ctxattn_call(na, nw, *, seq, nb):
    ctx_blk = nb * seq // CTX_LEN
    return pl.pallas_call(
        _ctxattn_kernel,
        grid=(nb,),
        in_specs=[
            pl.BlockSpec((CTX_LEN, D_NA), lambda b: (ctx_blk + b, 0)),
            pl.BlockSpec((CTX_LEN, D_NA), lambda b: (ctx_blk + b, 1)),
            pl.BlockSpec((CTX_LEN, D_NA), lambda b: (ctx_blk + b, 2)),
            pl.BlockSpec((1, D_NA), lambda b: (0, 0)),
        ],
        out_specs=pl.BlockSpec((CTX_LEN, D_NA), lambda b: (b, 0)),
        out_shape=jax.ShapeDtypeStruct((nb * CTX_LEN, D_NA), F32),
        compiler_params=_cparams(("parallel",)),
        name="ctxattn",
    )(na, na, na, nw)


def _outproj_kernel(x_ref, mod_ref, of_ref, ob_ref, g_ref, na_ref, yf_ref, yb_ref, xs_ref, z_ref,
                    hgw_ref, hm_ref, dsk_ref, sw_ref, wo_ref, o_ref):
    o = of_ref[...] + ob_ref[...]
    sq = o * o
    hi = sq.astype(BF16)
    lo = (sq - hi.astype(F32)).astype(BF16)
    hm = hm_ref[...]
    ms = (jnp.dot(hi, hm, preferred_element_type=F32)
          + jnp.dot(lo, hm, preferred_element_type=F32)) * (1.0 / HG_DK)
    hg = (o * lax.rsqrt(ms + EPS)) * hgw_ref[...] * _silu(g_ref[...])
    ys = (yf_ref[...] + yb_ref[...] + dsk_ref[...] * xs_ref[...]) * _silu(z_ref[...])
    ms2 = jnp.mean(ys * ys, axis=-1, keepdims=True)
    ssm = (ys * lax.rsqrt(ms2 + EPS)) * sw_ref[...]
    mix = (jnp.dot(hg.astype(BF16), wo_ref[0:D_HG, :], preferred_element_type=F32)
           + jnp.dot(na_ref[...].astype(BF16), wo_ref[D_HG:D_HG + D_NA, :], preferred_element_type=F32)
           + jnp.dot(ssm.astype(BF16), wo_ref[D_HG + D_NA:, :], preferred_element_type=F32))
    o_ref[...] = x_ref[...] + mod_ref[5:6, :] * mix


def _outproj_call(x_all, mods_l, o_f, o_b, hg, y_na, y_f, y_b, xbc_c, z, hgw, hm, dsk, sw, wo_b,
                  *, n_rows, seq, nb, tm=512):
    row = lambda i: (i, 0)
    const = lambda i: (0, 0)
    return pl.pallas_call(
        _outproj_kernel,
        grid=(n_rows // tm,),
        in_specs=[
            pl.BlockSpec((tm, D_MODEL), row),
            _mod_spec(tm, seq, nb),
            pl.BlockSpec((tm, D_HG), row),
            pl.BlockSpec((tm, D_HG), row),
            pl.BlockSpec((tm, D_HG), lambda i: (i, 4)),
            pl.BlockSpec((tm, D_NA), row),
            pl.BlockSpec((tm, D_SSM), row),
            pl.BlockSpec((tm, D_SSM), row),
            pl.BlockSpec((tm, D_SSM), row),
            pl.BlockSpec((tm, D_SSM), row),
            pl.BlockSpec((1, D_HG), const),
            pl.BlockSpec((D_HG, D_HG), const),
            pl.BlockSpec((1, D_SSM), const),
            pl.BlockSpec((1, D_SSM), const),
            pl.BlockSpec((D_MODEL, D_MODEL), const),
        ],
        out_specs=pl.BlockSpec((tm, D_MODEL), row),
        out_shape=jax.ShapeDtypeStruct((n_rows, D_MODEL), F32),
        compiler_params=_cparams(("parallel",)),
        name="outproj",
    )(x_all, mods_l, o_f, o_b, hg, y_na, y_f, y_b, xbc_c, z, hgw, hm, dsk, sw, wo_b)


def _na_bias_table(rpb):
    w = GRID_W
    ndr = 2 * WIN_R - 1
    ndc = 2 * WIN_C - 1
    col = np.arange(w)
    c0 = np.clip(col - WIN_C // 2, 0, w - WIN_C)
    col_in = (col[None, :] >= c0[:, None]) & (col[None, :] < c0[:, None] + WIN_C)
    dc = np.clip(col[None, :] - col[:, None], -(WIN_C - 1), WIN_C - 1) + (WIN_C - 1)
    onehot = (dc.reshape(1, -1) == np.arange(ndc)[:, None]).astype(np.float32)
    t = jnp.dot(rpb.reshape(NA_HEADS * ndr, ndc).astype(F32), onehot,
                precision=lax.Precision.HIGHEST).reshape(NA_HEADS, ndr, w, w)
    t = jnp.where(col_in[None, None], t, MASK_VALUE)
    b = jnp.stack([t[:, WIN_R - 1 - var:2 * WIN_R - 1 - var] for var in range(WIN_R)], axis=0)
    b = jnp.transpose(b, (0, 1, 3, 2, 4))
    return b.reshape(WIN_R, NA_HEADS * w, WIN_R * w)


def _block_tri(n, c, rev):
    r = jnp.arange(n)
    same = (r[:, None] // c) == (r[None, :] // c)
    tri = (r[None, :] >= r[:, None]) if rev else (r[None, :] <= r[:, None])
    return (same & tri).astype(BF16), same.astype(BF16)


def _head_block_ones(n, hd):
    r = jnp.arange(n)
    return ((r[:, None] // hd) == (r[None, :] // hd)).astype(BF16)


def _ssd_expand(lane0):
    r = jnp.arange(LANES)[:, None]
    cidx = jnp.arange(D_SSM)[None, :]
    return (r == lane0 + cidx // SSM_P).astype(BF16)


def _lane_row(vals, lane0):
    return jnp.zeros((1, LANES), F32).at[0, lane0:lane0 + vals.shape[0]].set(vals.astype(F32))


def kernel(x, c, ctx, c_ctx, w_mod, b_mod, norm_ffn1, ffn1_w13, ffn1_w2, norm_mix, w_in,
           hg_lower_bounds, hg_norm, na_rpb, na_norm, ssm_conv_w, ssm_conv_b, ssm_a_log,
           ssm_dt_bias, ssm_d, ssm_norm, w_out, norm_ffn2, ffn2_w13, ffn2_w2, final_norm):
    nb, seq, d = x.shape
    depth = w_mod.shape[0]
    assert d == D_MODEL and ctx.shape[1] == CTX_LEN and nb + 1 <= MOD_ROWS
    assert seq % 512 == 0 and seq // GRID_W >= WIN_R
    n_lat = nb * seq
    n_all = n_lat + nb * CTX_LEN

    lb_soft = jax.nn.softmax(hg_lower_bounds.astype(F32), axis=1)
    lower_bounds = jnp.cumsum(lb_soft, axis=1) - lb_soft[:, :1]

    c_rows = jnp.zeros((MOD_ROWS, d), F32).at[:nb].set(c).at[nb].set(c_ctx)
    mods = _mods_call(c_rows, w_mod, b_mod).reshape(depth, MOD_ROWS, N_MOD, d)

    x_all = jnp.concatenate([x.reshape(n_lat, d), ctx.reshape(nb * CTX_LEN, d)], axis=0)

    tri_f, ones_g = _block_tri(GLA_TILE, GLA_SUB, False)
    tri_b, _ = _block_tri(GLA_TILE, GLA_SUB, True)
    ind = _head_block_ones(D_HG, HG_DK)
    ssd_tri_f, _ = _block_tri(SSD_Q, SSD_Q, False)
    ssd_tri_b, _ = _block_tri(SSD_Q, SSD_Q, True)
    e_f = _ssd_expand(0)
    e_b = _ssd_expand(SSM_HEADS)
    one_row = lambda v: v.reshape(1, -1).astype(F32)
    common = dict(seq=seq, nb=nb)

    for layer in range(depth):
        last = layer == depth - 1
        mods_l = mods[layer]
        w13_1 = ffn1_w13[layer].astype(BF16)
        w2_1 = ffn1_w2[layer].astype(BF16)
        w13_2 = ffn2_w13[layer].astype(BF16)
        w2_2 = ffn2_w2[layer].astype(BF16)
        w_in_b = jnp.pad(w_in[layer], ((0, 0), (0, D_IN_PAD - w_in.shape[2]))).astype(BF16)
        wo_b = w_out[layer].astype(BF16)
        fw = one_row(final_norm)

        x_all = _ffn_call(x_all, mods_l, one_row(norm_ffn1[layer]), w13_1, w2_1, fw,
                          n_rows=n_all, ks=(0, 1, 2), final=False, **common)
        hg, na, z, xbc, dt_raw = _inproj_call(x_all, mods_l, one_row(norm_mix[layer]), w_in_b,
                                              n_rows=n_all, **common)

        o_f = _gla_call(hg, one_row(lower_bounds[0, layer]), tri_f, ones_g, ind,
                        n_rows=n_all, rev=False, **common)
        o_b = _gla_call(hg, one_row(lower_bounds[1, layer]), tri_b, ones_g, ind,
                        n_rows=n_all, rev=True, **common)

        bias = _na_bias_table(na_rpb[layer])
        nw = one_row(na_norm[layer])
        y_na = _na_call(na, bias, nw, **common)
        if not last:
            y_na = jnp.concatenate([y_na, _ctxattn_call(na, nw, **common)], axis=0)

        xbc_c = _conv_call(xbc, ssm_conv_w[layer].astype(F32), one_row(ssm_conv_b[layer]),
                           n_rows=n_all, **common)
        a_neg = -jnp.exp(ssm_a_log[layer].astype(F32))
        y_f = _ssd_call(xbc_c, dt_raw, _lane_row(a_neg[0], 0), _lane_row(ssm_dt_bias[layer, 0], 0),
                        e_f, ssd_tri_f, n_rows=n_all, rev=False, lane0=0, **common)
        y_b = _ssd_call(xbc_c, dt_raw, _lane_row(a_neg[1], SSM_HEADS),
                        _lane_row(ssm_dt_bias[layer, 1], SSM_HEADS),
                        e_b, ssd_tri_b, n_rows=n_all, rev=True, lane0=SSM_HEADS, **common)

        n_out = n_lat if last else n_all
        dsk = jnp.repeat(ssm_d[layer].astype(F32), SSM_P).reshape(1, D_SSM)
        x_all = _outproj_call(x_all, mods_l, o_f, o_b, hg, y_na, y_f, y_b, xbc_c, z,
                              one_row(hg_norm[layer]), ind, dsk, one_row(ssm_norm[layer]), wo_b,
                              n_rows=n_out, **common)
        x_all = _ffn_call(x_all, mods_l, one_row(norm_ffn2[layer]), w13_2, w2_2, fw,
                          n_rows=n_out, ks=(6, 7, 8), final=last, **common)

    return x_all.reshape(nb, seq, d)
```

```python
import functools
import math

import jax
import jax.numpy as jnp
import numpy as np
from jax import lax
from jax.experimental import pallas as pl
from jax.experimental.pallas import tpu as pltpu

F32 = jnp.float32
BF16 = jnp.bfloat16

D_MODEL = 1024
GRID_W = 64
CTX_LEN = 256
EPS = 1e-6
N_MOD = 9
MASK_VALUE = -1e30
TINY = 1e-20
LOG2E = 1.4426950408889634

D_HG = 256
HG_HEADS = 4
HG_DK = 64
D_NA = 256
NA_HEADS = 4
NA_DH = 64
WIN_R = 8
WIN_C = 16
D_SSM = 512
SSM_HEADS = 8
SSM_P = 64
SSM_N = 128
SSM_G = 2
CONV_W = 5
CONV_DIM = D_SSM + 2 * SSM_G * SSM_N
D_FF = 2816
D_IN_PAD = 3712
LANES = 128
MOD_ROWS = 8

GLA_TILE = 256
GLA_SUB = 16
GLA_P_ROWS = (GLA_TILE // GLA_SUB) * (GLA_SUB * GLA_SUB * 3 // 4)
SSD_Q = 128
SSD_STEP = 256
NA_ROWS = 8

VMEM_LIMIT = 56 * 1024 * 1024


def _cparams(sem):
    return pltpu.CompilerParams(dimension_semantics=sem, vmem_limit_bytes=VMEM_LIMIT)


def _silu(x):
    return x * jax.nn.sigmoid(x)


def _split3(x):
    hi = x.astype(BF16)
    r1 = x - hi.astype(F32)
    mid = r1.astype(BF16)
    lo = (r1 - mid.astype(F32)).astype(BF16)
    return hi, mid, lo


def _ldot3(a_bf16, x):
    hi, mid, lo = _split3(x)
    return (jnp.dot(a_bf16, hi, preferred_element_type=F32)
            + jnp.dot(a_bf16, mid, preferred_element_type=F32)
            + jnp.dot(a_bf16, lo, preferred_element_type=F32))


def _rdot3(x, e_bf16):
    hi, mid, lo = _split3(x)
    return (jnp.dot(hi, e_bf16, preferred_element_type=F32)
            + jnp.dot(mid, e_bf16, preferred_element_type=F32)
            + jnp.dot(lo, e_bf16, preferred_element_type=F32))


def _dot_nt(a, b):
    return lax.dot_general(a, b, (((1,), (1,)), ((), ())), preferred_element_type=F32)


def _dot_tn(a, b):
    return lax.dot_general(a, b, (((0,), (0,)), ((), ())), preferred_element_type=F32)


def _modulated(x, nw, shift, scale):
    ms = jnp.mean(x * x, axis=-1, keepdims=True)
    y = x * lax.rsqrt(ms + EPS)
    return (y * nw) * (1.0 + scale) + shift


def _mods_kernel(c_ref, w_ref, b_ref, o_ref):
    sc = _silu(c_ref[...]).astype(BF16)
    o_ref[...] = jnp.dot(sc, w_ref[...].astype(BF16), preferred_element_type=F32) + b_ref[...]


def _mods_call(c_rows, w_mod, b_mod):
    depth = w_mod.shape[0]
    tn = 1152
    n = N_MOD * D_MODEL
    return pl.pallas_call(
        _mods_kernel,
        grid=(depth, n // tn),
        in_specs=[
            pl.BlockSpec((MOD_ROWS, D_MODEL), lambda l, j: (0, 0)),
            pl.BlockSpec((None, D_MODEL, tn), lambda l, j: (l, 0, j)),
            pl.BlockSpec((None, 1, tn), lambda l, j: (l, 0, j)),
        ],
        out_specs=pl.BlockSpec((None, MOD_ROWS, tn), lambda l, j: (l, 0, j)),
        out_shape=jax.ShapeDtypeStruct((depth, MOD_ROWS, n), F32),
        compiler_params=_cparams(("parallel", "parallel")),
        name="mods",
    )(c_rows, w_mod, b_mod.reshape(depth, 1, n))


def _mod_spec(tm, seq, nb):
    tpb = seq // tm
    return pl.BlockSpec((None, N_MOD, D_MODEL), lambda i: (jnp.minimum(i // tpb, nb), 0, 0))


def _ffn_kernel(*refs, k_shift, k_scale, k_gate, final, n_lat_tiles):
    if n_lat_tiles is None:
        x_ref, mod_ref, nw_ref, wu_ref, wg_ref, w2_ref, fw_ref, o_ref, act_ref = refs
        x = x_ref[...]
    else:
        x_ref, xc_ref, mod_ref, nw_ref, wu_ref, wg_ref, w2_ref, fw_ref, o_ref, act_ref = refs
        x = jnp.where(pl.program_id(0) >= n_lat_tiles, xc_ref[...], x_ref[...])
    h = _modulated(x, nw_ref[...], mod_ref[k_shift:k_shift + 1, :],
                   mod_ref[k_scale:k_scale + 1, :]).astype(BF16)
    cw = 256
    for c in range(D_FF // cw):
        u = jnp.dot(h, wu_ref[:, c * cw:(c + 1) * cw], preferred_element_type=F32)
        g = jnp.dot(h, wg_ref[:, c * cw:(c + 1) * cw], preferred_element_type=F32)
        act_ref[:, c * cw:(c + 1) * cw] = (_silu(g) * u).astype(BF16)
    y = jnp.dot(act_ref[...], w2_ref[...], preferred_element_type=F32)
    out = x + (0.5 * mod_ref[k_gate:k_gate + 1, :]) * y
    if final:
        ms = jnp.mean(out * out, axis=-1, keepdims=True)
        out = (out * lax.rsqrt(ms + EPS)) * fw_ref[...]
    o_ref[...] = out


def _ffn_call(x_all, mods_l, nw, w13b, w2b, fw, *, n_rows, seq, nb, ks, final, x_ctx=None, tm=512):
    if x_ctx is None:
        n_lat_tiles = None
        x_args = (x_all,)
        x_specs = [pl.BlockSpec((tm, D_MODEL), lambda i: (i, 0))]
    else:
        n_lat_tiles = nb * seq // tm
        x_args = (x_all, x_ctx)
        x_specs = [pl.BlockSpec((tm, D_MODEL), lambda i: (jnp.minimum(i, n_lat_tiles - 1), 0)),
                   pl.BlockSpec((tm, D_MODEL), lambda i: (jnp.maximum(i - n_lat_tiles, 0), 0))]
    kern = functools.partial(_ffn_kernel, k_shift=ks[0], k_scale=ks[1], k_gate=ks[2], final=final,
                             n_lat_tiles=n_lat_tiles)
    return pl.pallas_call(
        kern,
        grid=(n_rows // tm,),
        in_specs=x_specs + [
            _mod_spec(tm, seq, nb),
            pl.BlockSpec((1, D_MODEL), lambda i: (0, 0)),
            pl.BlockSpec((D_MODEL, D_FF), lambda i: (0, 0), pipeline_mode=pl.Buffered(1)),
            pl.BlockSpec((D_MODEL, D_FF), lambda i: (0, 1), pipeline_mode=pl.Buffered(1)),
            pl.BlockSpec((D_FF, D_MODEL), lambda i: (0, 0), pipeline_mode=pl.Buffered(1)),
            pl.BlockSpec((1, D_MODEL), lambda i: (0, 0)),
        ],
        out_specs=pl.BlockSpec((tm, D_MODEL), lambda i: (i, 0)),
        out_shape=jax.ShapeDtypeStruct((n_rows, D_MODEL), F32),
        scratch_shapes=[pltpu.VMEM((tm, D_FF), BF16)],
        compiler_params=_cparams(("parallel",)),
        name="ffn",
    )(*x_args, mods_l, nw, w13b, w13b, w2b, fw)


_IN_SPLITS = ((0, 1280), (1280, 2048), (2048, 2560), (2560, 3584), (3584, 3712))


_XBC_COLS = _IN_SPLITS[3]
HALO = 8


def _inproj_kernel(x_ref, xp_ref, xn_ref, mod_ref, nw_ref, w_ref, cw_ref, cb_ref,
                   hg_ref, na_ref, z_ref, xbc_ref, dt_ref, ext_ref, *, n_lat_tiles, tpb):
    tm = x_ref.shape[0]
    i = pl.program_id(0)
    is_ctx = i >= n_lat_tiles
    first = jnp.logical_or(is_ctx, (i % tpb) == 0)
    last = jnp.logical_or(is_ctx, (i % tpb) == tpb - 1)
    nw = nw_ref[...]
    shift = mod_ref[3:4, :]
    scale = mod_ref[4:5, :]
    h = _modulated(x_ref[...], nw, shift, scale).astype(BF16)
    x_halo = jnp.concatenate([xp_ref[...], xn_ref[...]], axis=0)
    h_halo = _modulated(x_halo, nw, shift, scale).astype(BF16)

    a, b = _XBC_COLS
    wx = w_ref[:, a:b]
    halo = jnp.dot(h_halo, wx, preferred_element_type=F32)
    ext_ref[0:HALO, :] = jnp.where(first, 0.0, halo[0:HALO, :])
    ext_ref[HALO:HALO + tm, :] = jnp.dot(h, wx, preferred_element_type=F32)
    ext_ref[HALO + tm:2 * HALO + tm, :] = jnp.where(last, 0.0, halo[HALO:2 * HALO, :])
    pos = lax.broadcasted_iota(jnp.int32, (tm, 1), 0) % CTX_LEN
    acc = jnp.zeros((tm, CONV_DIM), F32) + cb_ref[...]
    for j in range(CONV_W):
        d = j - CONV_W // 2
        tap = ext_ref[HALO + d:HALO + d + tm, :] * cw_ref[j:j + 1, :]
        if d != 0:
            crosses = jnp.logical_and(is_ctx, jnp.logical_or(pos + d < 0, pos + d >= CTX_LEN))
            tap = jnp.where(crosses, 0.0, tap)
        acc = acc + tap
    xbc_ref[...] = _silu(acc)

    for (a, b), o_ref in zip(_IN_SPLITS, (hg_ref, na_ref, z_ref, None, dt_ref)):
        if o_ref is not None:
            o_ref[...] = jnp.dot(h, w_ref[:, a:b], preferred_element_type=F32).astype(o_ref.dtype)


def _inproj_call(x_all, mods_l, nw, w_in_b, conv_w, conv_b, *, n_rows, seq, nb, tm=512):
    widths = [b - a for a, b in _IN_SPLITS]
    dtypes = [F32, BF16, F32, F32, F32]
    r8 = tm // HALO
    last8 = n_rows // HALO - 1
    kern = functools.partial(_inproj_kernel, n_lat_tiles=nb * seq // tm, tpb=seq // tm)
    return pl.pallas_call(
        kern,
        grid=(n_rows // tm,),
        in_specs=[
            pl.BlockSpec((tm, D_MODEL), lambda i: (i, 0)),
            pl.BlockSpec((HALO, D_MODEL), lambda i: (jnp.maximum(i * r8 - 1, 0), 0)),
            pl.BlockSpec((HALO, D_MODEL), lambda i: (jnp.minimum((i + 1) * r8, last8), 0)),
            _mod_spec(tm, seq, nb),
            pl.BlockSpec((1, D_MODEL), lambda i: (0, 0)),
            pl.BlockSpec((D_MODEL, D_IN_PAD), lambda i: (0, 0), pipeline_mode=pl.Buffered(1)),
            pl.BlockSpec((CONV_W, CONV_DIM), lambda i: (0, 0)),
            pl.BlockSpec((1, CONV_DIM), lambda i: (0, 0)),
        ],
        out_specs=[pl.BlockSpec((tm, w), lambda i: (i, 0)) for w in widths],
        out_shape=[jax.ShapeDtypeStruct((n_rows, w), dt) for w, dt in zip(widths, dtypes)],
        scratch_shapes=[pltpu.VMEM((tm + 2 * HALO, CONV_DIM), F32)],
        compiler_params=_cparams(("parallel",)),
        name="inproj",
    )(x_all, x_all, x_all, mods_l, nw, w_in_b, conv_w, conv_b)


def _scan_block_map(*, seq, nb, q, rev):
    nc_ctx = CTX_LEN // q
    nc_lat = seq // q
    ctx_base = nb * seq // q

    def row_block(b, j):
        cj = j
        lj = j - nc_ctx
        if rev:
            cj = nc_ctx - 1 - cj
            lj = nc_lat - 1 - lj
        return jnp.where(j < nc_ctx, ctx_base + b * nc_ctx + cj, b * nc_lat + lj)

    return row_block, nc_ctx + nc_lat


def _round_robin(stage_generators):
    live = list(stage_generators)
    while live:
        for gen in list(live):
            try:
                next(gen)
            except StopIteration:
                live.remove(gen)


def _ssd_direction(xbc_ref, dt_ref, arow_ref, brow_ref, e_ref, tri_ref, y_ref, st_ref, *, rev, lane0):
    q = SSD_Q
    nch = SSD_STEP // q
    gw = D_SSM // SSM_G
    t_idx = lax.broadcasted_iota(jnp.int32, (q, q), 0)
    s_idx = lax.broadcasted_iota(jnp.int32, (q, q), 1)
    causal = (s_idx >= t_idx) if rev else (s_idx <= t_idx)
    lane = lax.broadcasted_iota(jnp.int32, (q, LANES), 1)
    low_half = lane < SSM_P
    e = e_ref[...]
    tri = tri_ref[...]
    edge = 0 if rev else q - 1
    chunks = [dict(rows=slice(ci * q, (ci + 1) * q))
              for ci in (range(nch - 1, -1, -1) if rev else range(nch))]

    def stage_cumsum(ch):
        rows = ch["rows"]
        dt = jax.nn.softplus(dt_ref[rows, :] + brow_ref[...])
        ch["dt"] = dt
        ch["cum"] = _ldot3(tri, dt * arow_ref[...])

    def stage_expand(ch):
        rows = ch["rows"]
        cum = ch["cum"]
        ch["cum_t"] = cum.T
        cum_e = _rdot3(cum, e)
        ch["cum_e"] = cum_e
        ch["tot_e"] = cum_e[edge:edge + 1, :]
        ch["xdt"] = xbc_ref[rows, 0:D_SSM] * _rdot3(ch["dt"], e)
        ch["bg"] = []
        ch["cg"] = []
        ch["cb"] = []
        for g in range(SSM_G):
            b0 = D_SSM + g * SSM_N
            c0 = D_SSM + SSM_G * SSM_N + g * SSM_N
            bg = xbc_ref[rows, b0:b0 + SSM_N].astype(BF16)
            cg = xbc_ref[rows, c0:c0 + SSM_N].astype(BF16)
            ch["bg"].append(bg)
            ch["cg"].append(cg)
            ch["cb"].append(_dot_nt(cg, bg))

    def stage_local(ch):
        cum, cum_t, cum_e, tot_e, xdt = ch["cum"], ch["cum_t"], ch["cum_e"], ch["tot_e"], ch["xdt"]
        ch["y_diag"] = []
        ch["upd"] = []
        for g in range(SSM_G):
            gl = g * gw
            for hp in range(2):
                pl0 = gl + hp * LANES
                xpair = xdt[:, pl0:pl0 + LANES]
                acc = jnp.zeros((q, LANES), F32)
                for hh in range(2):
                    idx = lane0 + g * 4 + hp * 2 + hh
                    diff = cum[:, idx:idx + 1] - cum_t[idx:idx + 1, :]
                    m = ch["cb"][g] * jnp.exp(jnp.where(causal, diff, MASK_VALUE))
                    xm = jnp.where(low_half if hh == 0 else jnp.logical_not(low_half), xpair, 0.0)
                    acc = acc + jnp.dot(m.astype(BF16), xm.astype(BF16), preferred_element_type=F32)
                ch["y_diag"].append(acc)
            xw = (xdt[:, gl:gl + gw] * jnp.exp(tot_e[:, gl:gl + gw] - cum_e[:, gl:gl + gw])).astype(BF16)
            ch["upd"].append(_dot_tn(ch["bg"][g], xw))
        ch["off_scale"] = jnp.exp(cum_e)
        ch["dec"] = jnp.exp(tot_e)

    def stage_state():
        st = [st_ref[:, g * gw:(g + 1) * gw] for g in range(SSM_G)]
        for ch in chunks:
            y_parts = []
            for g in range(SSM_G):
                gl = g * gw
                y_off = jnp.dot(ch["cg"][g], st[g].astype(BF16), preferred_element_type=F32)
                y_parts.append(jnp.concatenate(ch["y_diag"][2 * g:2 * g + 2], axis=1)
                               + y_off * ch["off_scale"][:, gl:gl + gw])
                st[g] = st[g] * ch["dec"][:, gl:gl + gw] + ch["upd"][g]
            y_ref[ch["rows"], :] = jnp.concatenate(y_parts, axis=1)
        for g in range(SSM_G):
            st_ref[:, g * gw:(g + 1) * gw] = st[g]

    for ch in chunks:
        yield stage_cumsum(ch)
    for ch in chunks:
        yield stage_expand(ch)
    for ch in chunks:
        yield stage_local(ch)
    yield stage_state()


def _ssd_kernel(xf_ref, xb_ref, dtf_ref, dtb_ref, arow_ref, brow_ref, ef_ref, eb_ref, trif_ref, trib_ref,
                yf_ref, yb_ref, stf_ref, stb_ref):
    @pl.when(pl.program_id(1) == 0)
    def _():
        stf_ref[...] = jnp.zeros_like(stf_ref)
        stb_ref[...] = jnp.zeros_like(stb_ref)

    _round_robin([
        _ssd_direction(xf_ref, dtf_ref, arow_ref, brow_ref, ef_ref, trif_ref, yf_ref, stf_ref,
                       rev=False, lane0=0),
        _ssd_direction(xb_ref, dtb_ref, arow_ref, brow_ref, eb_ref, trib_ref, yb_ref, stb_ref,
                       rev=True, lane0=SSM_HEADS)])


def _ssd_call(xbc_c, dt_raw, a_row, b_row, e_f, e_b, tri_f, tri_b, *, n_rows, seq, nb):
    q = SSD_STEP
    blk_f, steps = _scan_block_map(seq=seq, nb=nb, q=q, rev=False)
    blk_b, _ = _scan_block_map(seq=seq, nb=nb, q=q, rev=True)
    const = lambda b, j: (0, 0)
    return pl.pallas_call(
        _ssd_kernel,
        grid=(nb, steps),
        in_specs=[
            pl.BlockSpec((q, CONV_DIM), lambda b, j: (blk_f(b, j), 0)),
            pl.BlockSpec((q, CONV_DIM), lambda b, j: (blk_b(b, j), 0)),
            pl.BlockSpec((q, LANES), lambda b, j: (blk_f(b, j), 0)),
            pl.BlockSpec((q, LANES), lambda b, j: (blk_b(b, j), 0)),
            pl.BlockSpec((1, LANES), const),
            pl.BlockSpec((1, LANES), const),
            pl.BlockSpec((LANES, D_SSM), const),
            pl.BlockSpec((LANES, D_SSM), const),
            pl.BlockSpec((SSD_Q, SSD_Q), const),
            pl.BlockSpec((SSD_Q, SSD_Q), const),
        ],
        out_specs=[pl.BlockSpec((q, D_SSM), lambda b, j: (blk_f(b, j), 0)),
                   pl.BlockSpec((q, D_SSM), lambda b, j: (blk_b(b, j), 0))],
        out_shape=[jax.ShapeDtypeStruct((n_rows, D_SSM), F32)] * 2,
        scratch_shapes=[pltpu.VMEM((SSM_N, D_SSM), F32)] * 2,
        compiler_params=_cparams(("arbitrary", "arbitrary")),
        name="ssd",
    )(xbc_c, xbc_c, dt_raw, dt_raw, a_row, b_row, e_f, e_b, tri_f, tri_b)


def _gla_direction(q_ref, f_ref, v_ref, lb_ref, tri_ref, ones_ref, ind_ref, o_ref, p_s, upd_s, st_ref, *, rev):
    t = GLA_TILE
    c = GLA_SUB
    nsub = t // c
    half = c // 2
    lb = lb_ref[...]
    fr = f_ref[...]
    f = lb + (1.0 - lb) * jax.nn.sigmoid(fr)
    logf = jnp.log(jnp.maximum(f, TINY))
    k = (1.0 - lb) * jax.nn.sigmoid(-fr)
    qv = _silu(q_ref[...])
    vv = v_ref[...]
    yield
    brel = _ldot3(tri_ref[...], logf)
    tot = _ldot3(ones_ref[...], logf)
    yield

    qt = (qv * jnp.exp(brel)).astype(BF16)
    kt = k * jnp.exp(tot - brel)
    dec_blk = jnp.exp(tot)
    lane = lax.broadcasted_iota(jnp.int32, (t, D_HG), 1)
    head_masks = [(lane >= h * HG_DK) & (lane < (h + 1) * HG_DK) for h in range(HG_HEADS)]
    kxs = [jnp.where(m, kt, 0.0).astype(BF16) for m in head_masks]
    vxs = [jnp.where(m, vv, 0.0).astype(BF16) for m in head_masks]
    for blk in range(nsub):
        r0 = blk * c
        kx = jnp.concatenate([a[r0:r0 + c, :] for a in kxs], axis=0)
        vx = jnp.concatenate([a[r0:r0 + c, :] for a in vxs], axis=0)
        upd_s[blk] = _dot_tn(vx, kx)
        if blk % 4 == 3:
            yield

    t_idx = lax.broadcasted_iota(jnp.int32, (c, D_HG), 0)
    full_keys = [s for s in range(c) if (s >= half if rev else s < half)]
    half_keys = [s for s in range(c) if s not in full_keys]
    lo = 0 if rev else half
    units = [(s,) for s in full_keys] + [tuple(half_keys[i:i + 2]) for i in range(0, half, 2)]
    rows_per_blk = len(units) * c

    t_idx_half = lax.broadcasted_iota(jnp.int32, (half, D_HG), 0) + lo

    def pair_rows(qb, kb, bb, s, r_lo, r_hi):
        tt = t_idx if r_hi - r_lo == c else t_idx_half
        keep = (tt <= s) if rev else (tt >= s)
        dec = jnp.exp2(jnp.where(keep, bb[r_lo:r_hi, :] - bb[s:s + 1, :], MASK_VALUE))
        return qb[r_lo:r_hi, :] * kb[s:s + 1, :] * dec

    brel2 = brel * LOG2E
    for blk in range(nsub):
        r0 = blk * c
        qb = qv[r0:r0 + c, :]
        kb = k[r0:r0 + c, :]
        bb = brel2[r0:r0 + c, :]
        for u, keys in enumerate(units):
            if len(keys) == 1:
                p = pair_rows(qb, kb, bb, keys[0], 0, c)
            else:
                p = jnp.concatenate([pair_rows(qb, kb, bb, s, lo, lo + half) for s in keys], axis=0)
            po = blk * rows_per_blk + u * c
            p_s[po:po + c, :] = p.astype(BF16)
        yield
    r = jnp.dot(p_s[...], ind_ref[...], preferred_element_type=F32)
    yield
    o_blocks = []
    for blk in range(nsub):
        r0 = blk * c
        vb = vv[r0:r0 + c, :]
        acc_full = jnp.zeros((c, D_HG), F32)
        acc_half = jnp.zeros((half, D_HG), F32)
        for u, keys in enumerate(units):
            po = blk * rows_per_blk + u * c
            rs = r[po:po + c, :]
            if len(keys) == 1:
                acc_full = acc_full + rs * vb[keys[0]:keys[0] + 1, :]
            else:
                acc_half = (acc_half + rs[:half, :] * vb[keys[0]:keys[0] + 1, :]
                            + rs[half:, :] * vb[keys[1]:keys[1] + 1, :])
        zeros_half = jnp.zeros((half, D_HG), F32)
        pieces = [acc_half, zeros_half] if rev else [zeros_half, acc_half]
        o_blocks.append(acc_full + jnp.concatenate(pieces, axis=0))
        yield
    o_intra = jnp.concatenate(o_blocks, axis=0)

    st = st_ref[...]
    o_inter = [None] * nsub
    for blk in (range(nsub - 1, -1, -1) if rev else range(nsub)):
        r0 = blk * c
        o_inter[blk] = _dot_nt(qt[r0:r0 + c, :], st.astype(BF16))
        st = st * dec_blk[r0:r0 + 1, :] + upd_s[blk]
        yield
    st_ref[...] = st
    o_ref[...] = o_intra + jnp.concatenate(o_inter, axis=0)


def _gla_kernel(qf_ref, ff_ref, vf_ref, qb_ref, fb_ref, vb_ref, lbf_ref, lbb_ref, trif_ref, trib_ref,
                ones_ref, ind_ref, of_ref, ob_ref, pf_s, pb_s, updf_s, updb_s, stf_ref, stb_ref):
    @pl.when(pl.program_id(1) == 0)
    def _():
        stf_ref[...] = jnp.zeros_like(stf_ref)
        stb_ref[...] = jnp.zeros_like(stb_ref)

    _round_robin([
        _gla_direction(qf_ref, ff_ref, vf_ref, lbf_ref, trif_ref, ones_ref, ind_ref, of_ref,
                       pf_s, updf_s, stf_ref, rev=False),
        _gla_direction(qb_ref, fb_ref, vb_ref, lbb_ref, trib_ref, ones_ref, ind_ref, ob_ref,
                       pb_s, updb_s, stb_ref, rev=True)])


def _gla_call(hg, lb_f, lb_b, tri_f, tri_b, ones, ind, *, n_rows, seq, nb):
    t = GLA_TILE
    blk_f, steps = _scan_block_map(seq=seq, nb=nb, q=t, rev=False)
    blk_b, _ = _scan_block_map(seq=seq, nb=nb, q=t, rev=True)
    const = lambda b, j: (0, 0)
    col = lambda blk, cidx: pl.BlockSpec((t, D_HG), lambda b, j: (blk(b, j), cidx))
    return pl.pallas_call(
        _gla_kernel,
        grid=(nb, steps),
        in_specs=[
            col(blk_f, 0), col(blk_f, 1), col(blk_f, 3),
            col(blk_b, 0), col(blk_b, 2), col(blk_b, 3),
            pl.BlockSpec((1, D_HG), const),
            pl.BlockSpec((1, D_HG), const),
            pl.BlockSpec((t, t), const),
            pl.BlockSpec((t, t), const),
            pl.BlockSpec((t, t), const),
            pl.BlockSpec((D_HG, D_HG), const),
        ],
        out_specs=[col(blk_f, 0), col(blk_b, 0)],
        out_shape=[jax.ShapeDtypeStruct((n_rows, D_HG), F32)] * 2,
        scratch_shapes=(
            [pltpu.VMEM((GLA_P_ROWS, D_HG), BF16)] * 2
            + [pltpu.VMEM((GLA_TILE // GLA_SUB, D_HG, D_HG), F32)] * 2
            + [pltpu.VMEM((D_HG, D_HG), F32)] * 2),
        compiler_params=_cparams(("arbitrary", "arbitrary")),
        name="gla",
    )(hg, hg, hg, hg, hg, hg, lb_f, lb_b, tri_f, tri_b, ones, ind)


def _expand_heads(x, masks):
    return jnp.concatenate([jnp.where(m, x, jnp.zeros_like(x)) for m in masks], axis=0)


def _collapse_heads(o, masks, t):
    acc = jnp.where(masks[0], o[0:t, :], 0.0)
    for h in range(1, NA_HEADS):
        acc = acc + jnp.where(masks[h], o[h * t:(h + 1) * t, :], 0.0)
    return acc


def _na_finish(o, nw):
    ms = jnp.mean(o * o, axis=-1, keepdims=True)
    return (o * lax.rsqrt(ms + EPS)) * nw


def _na_kernel(q_ref, k_ref, v_ref, kc_ref, vc_ref, bias_ref, nw_ref, o_ref, *, n_grid_rows):
    w = GRID_W
    i = pl.program_id(1)
    lane = lax.broadcasted_iota(jnp.int32, (w, D_NA), 1)
    masks = [(lane >= h * NA_DH) & (lane < (h + 1) * NA_DH) for h in range(NA_HEADS)]
    kc = kc_ref[...]
    vc = vc_ref[...]
    nw = nw_ref[...]
    scale = NA_DH ** -0.5

    def body(jr, carry):
        r = i * NA_ROWS + jr
        r0 = jnp.clip(r - WIN_R // 2, 0, n_grid_rows - WIN_R)
        var = r - r0
        k0 = pl.multiple_of(r0 * w, w)
        q0 = pl.multiple_of(jr * w, w)
        qx = _expand_heads(q_ref[pl.ds(q0, w), :] * scale, masks)
        kw = k_ref[pl.ds(k0, WIN_R * w), :]
        vw = v_ref[pl.ds(k0, WIN_R * w), :]
        s_loc = _dot_nt(qx, kw) + bias_ref[var]
        s_ctx = _dot_nt(qx, kc)
        m = jnp.maximum(jnp.max(s_loc, axis=-1, keepdims=True), jnp.max(s_ctx, axis=-1, keepdims=True))
        p_loc = jnp.exp(s_loc - m)
        p_ctx = jnp.exp(s_ctx - m)
        den = jnp.sum(p_loc, axis=-1, keepdims=True) + jnp.sum(p_ctx, axis=-1, keepdims=True)
        o = (jnp.dot(p_loc.astype(BF16), vw, preferred_element_type=F32)
             + jnp.dot(p_ctx.astype(BF16), vc, preferred_element_type=F32)) / den
        o_ref[pl.ds(q0, w), :] = _na_finish(_collapse_heads(o, masks, w), nw)
        return carry

    lax.fori_loop(0, NA_ROWS, body, 0, unroll=True)


def _na_call(na, bias, nw, *, seq, nb):
    rows = seq // GRID_W
    tq = NA_ROWS * GRID_W
    qpb = seq // tq
    ctx_blk = nb * seq // CTX_LEN
    kern = functools.partial(_na_kernel, n_grid_rows=rows)
    return pl.pallas_call(
        kern,
        grid=(nb, qpb),
        in_specs=[
            pl.BlockSpec((tq, D_NA), lambda b, i: (b * qpb + i, 0)),
            pl.BlockSpec((seq, D_NA), lambda b, i: (b, 1)),
            pl.BlockSpec((seq, D_NA), lambda b, i: (b, 2)),
            pl.BlockSpec((CTX_LEN, D_NA), lambda b, i: (ctx_blk + b, 1)),
            pl.BlockSpec((CTX_LEN, D_NA), lambda b, i: (ctx_blk + b, 2)),
            pl.BlockSpec((WIN_R, NA_HEADS * GRID_W, WIN_R * GRID_W), lambda b, i: (0, 0, 0)),
            pl.BlockSpec((1, D_NA), lambda b, i: (0, 0)),
        ],
        out_specs=pl.BlockSpec((tq, D_NA), lambda b, i: (b * qpb + i, 0)),
        out_shape=jax.ShapeDtypeStruct((nb * seq, D_NA), F32),
        compiler_params=_cparams(("parallel", "arbitrary")),
        name="natten",
    )(na, na, na, na, na, bias, nw)


def _ctxattn_kernel(q_ref, k_ref, v_ref, nw_ref, o_ref):
    t = CTX_LEN
    lane = lax.broadcasted_iota(jnp.int32, (t, D_NA), 1)
    masks = [(lane >= h * NA_DH) & (lane < (h + 1) * NA_DH) for h in range(NA_HEADS)]
    qx = _expand_heads(q_ref[...] * (NA_DH ** -0.5), masks)
    s = _dot_nt(qx, k_ref[...])
    m = jnp.max(s, axis=-1, keepdims=True)
    p = jnp.exp(s - m)
    den = jnp.sum(p, axis=-1, keepdims=True)
    o = jnp.dot(p.astype(BF16), v_ref[...], preferred_element_type=F32) / den
    o_ref[...] = _na_finish(_collapse_heads(o, masks, t), nw_ref[...])


def _ctxattn_call(na, nw, *, seq, nb):
    ctx_blk = nb * seq // CTX_LEN
    return pl.pallas_call(
        _ctxattn_kernel,
        grid=(nb,),
        in_specs=[
            pl.BlockSpec((CTX_LEN, D_NA), lambda b: (ctx_blk + b, 0)),
            pl.BlockSpec((CTX_LEN, D_NA), lambda b: (ctx_blk + b, 1)),
            pl.BlockSpec((CTX_LEN, D_NA), lambda b: (ctx_blk + b, 2)),
            pl.BlockSpec((1, D_NA), lambda b: (0, 0)),
        ],
        out_specs=pl.BlockSpec((CTX_LEN, D_NA), lambda b: (b, 0)),
        out_shape=jax.ShapeDtypeStruct((nb * CTX_LEN, D_NA), F32),
        compiler_params=_cparams(("parallel",)),
        name="ctxattn",
    )(na, na, na, nw)


def _outproj_kernel(x_ref, mod_ref, of_ref, ob_ref, g_ref, na_ref, yf_ref, yb_ref, xs_ref, z_ref,
                    hgw_ref, hm_ref, dsk_ref, sw_ref, wo_ref, o_ref):
    o = of_ref[...] + ob_ref[...]
    sq = o * o
    hi = sq.astype(BF16)
    lo = (sq - hi.astype(F32)).astype(BF16)
    hm = hm_ref[...]
    ms = (jnp.dot(hi, hm, preferred_element_type=F32)
          + jnp.dot(lo, hm, preferred_element_type=F32)) * (1.0 / HG_DK)
    hg = (o * lax.rsqrt(ms + EPS)) * hgw_ref[...] * _silu(g_ref[...])
    ys = (yf_ref[...] + yb_ref[...] + dsk_ref[...] * xs_ref[...]) * _silu(z_ref[...])
    ms2 = jnp.mean(ys * ys, axis=-1, keepdims=True)
    ssm = (ys * lax.rsqrt(ms2 + EPS)) * sw_ref[...]
    mix = (jnp.dot(hg.astype(BF16), wo_ref[0:D_HG, :], preferred_element_type=F32)
           + jnp.dot(na_ref[...].astype(BF16), wo_ref[D_HG:D_HG + D_NA, :], preferred_element_type=F32)
           + jnp.dot(ssm.astype(BF16), wo_ref[D_HG + D_NA:, :], preferred_element_type=F32))
    o_ref[...] = x_ref[...] + mod_ref[5:6, :] * mix


def _outproj_call(x_all, mods_l, o_f, o_b, hg, y_na, y_f, y_b, xbc_c, z, hgw, hm, dsk, sw, wo_b,
                  *, n_rows, seq, nb, tm=512):
    row = lambda i: (i, 0)
    const = lambda i: (0, 0)
    return pl.pallas_call(
        _outproj_kernel,
        grid=(n_rows // tm,),
        in_specs=[
            pl.BlockSpec((tm, D_MODEL), row),
            _mod_spec(tm, seq, nb),
            pl.BlockSpec((tm, D_HG), row),
            pl.BlockSpec((tm, D_HG), row),
            pl.BlockSpec((tm, D_HG), lambda i: (i, 4)),
            pl.BlockSpec((tm, D_NA), row),
            pl.BlockSpec((tm, D_SSM), row),
            pl.BlockSpec((tm, D_SSM), row),
            pl.BlockSpec((tm, D_SSM), row),
            pl.BlockSpec((tm, D_SSM), row),
            pl.BlockSpec((1, D_HG), const),
            pl.BlockSpec((D_HG, D_HG), const),
            pl.BlockSpec((1, D_SSM), const),
            pl.BlockSpec((1, D_SSM), const),
            pl.BlockSpec((D_MODEL, D_MODEL), const),
        ],
        out_specs=pl.BlockSpec((tm, D_MODEL), row),
        out_shape=jax.ShapeDtypeStruct((n_rows, D_MODEL), F32),
        compiler_params=_cparams(("parallel",)),
        name="outproj",
    )(x_all, mods_l, o_f, o_b, hg, y_na, y_f, y_b, xbc_c, z, hgw, hm, dsk, sw, wo_b)


def _na_bias_table(rpb):
    w = GRID_W
    ndr = 2 * WIN_R - 1
    ndc = 2 * WIN_C - 1
    col = np.arange(w)
    c0 = np.clip(col - WIN_C // 2, 0, w - WIN_C)
    col_in = (col[None, :] >= c0[:, None]) & (col[None, :] < c0[:, None] + WIN_C)
    dc = np.clip(col[None, :] - col[:, None], -(WIN_C - 1), WIN_C - 1) + (WIN_C - 1)
    onehot = (dc.reshape(1, -1) == np.arange(ndc)[:, None]).astype(np.float32)
    t = jnp.dot(rpb.reshape(NA_HEADS * ndr, ndc).astype(F32), onehot,
                precision=lax.Precision.HIGHEST).reshape(NA_HEADS, ndr, w, w)
    t = jnp.where(col_in[None, None], t, MASK_VALUE)
    b = jnp.stack([t[:, WIN_R - 1 - var:2 * WIN_R - 1 - var] for var in range(WIN_R)], axis=0)
    b = jnp.transpose(b, (0, 1, 3, 2, 4))
    return b.reshape(WIN_R, NA_HEADS * w, WIN_R * w)


def _block_tri(n, c, rev):
    r = jnp.arange(n)
    same = (r[:, None] // c) == (r[None, :] // c)
    tri = (r[None, :] >= r[:, None]) if rev else (r[None, :] <= r[:, None])
    return (same & tri).astype(BF16), same.astype(BF16)


def _head_block_ones(n, hd):
    r = jnp.arange(n)
    return ((r[:, None] // hd) == (r[None, :] // hd)).astype(BF16)


def _ssd_expand(lane0):
    r = jnp.arange(LANES)[:, None]
    cidx = jnp.arange(D_SSM)[None, :]
    return (r == lane0 + cidx // SSM_P).astype(BF16)


def _lane_row(vals):
    return jnp.zeros((1, LANES), F32).at[0, :vals.shape[0]].set(vals.astype(F32))


def kernel(x, c, ctx, c_ctx, w_mod, b_mod, norm_ffn1, ffn1_w13, ffn1_w2, norm_mix, w_in,
           hg_lower_bounds, hg_norm, na_rpb, na_norm, ssm_conv_w, ssm_conv_b, ssm_a_log,
           ssm_dt_bias, ssm_d, ssm_norm, w_out, norm_ffn2, ffn2_w13, ffn2_w2, final_norm):
    nb, seq, d = x.shape
    depth = w_mod.shape[0]
    assert d == D_MODEL and ctx.shape[1] == CTX_LEN and nb + 1 <= MOD_ROWS
    assert seq % 512 == 0 and seq // GRID_W >= WIN_R
    n_lat = nb * seq
    n_all = n_lat + nb * CTX_LEN

    lb_soft = jax.nn.softmax(hg_lower_bounds.astype(F32), axis=1)
    lower_bounds = jnp.cumsum(lb_soft, axis=1) - lb_soft[:, :1]

    c_rows = jnp.zeros((MOD_ROWS, d), F32).at[:nb].set(c).at[nb].set(c_ctx)
    mods = _mods_call(c_rows, w_mod, b_mod).reshape(depth, MOD_ROWS, N_MOD, d)

    x_all = x.reshape(n_lat, d)
    x_ctx = ctx.reshape(nb * CTX_LEN, d)

    tri_f, ones_g = _block_tri(GLA_TILE, GLA_SUB, False)
    tri_b, _ = _block_tri(GLA_TILE, GLA_SUB, True)
    ind = _head_block_ones(D_HG, HG_DK)
    ssd_tri_f, _ = _block_tri(SSD_Q, SSD_Q, False)
    ssd_tri_b, _ = _block_tri(SSD_Q, SSD_Q, True)
    e_f = _ssd_expand(0)
    e_b = _ssd_expand(SSM_HEADS)
    one_row = lambda v: v.reshape(1, -1).astype(F32)
    common = dict(seq=seq, nb=nb)

    for layer in range(depth):
        last = layer == depth - 1
        mods_l = mods[layer]
        w13_1 = ffn1_w13[layer].astype(BF16)
        w2_1 = ffn1_w2[layer].astype(BF16)
        w13_2 = ffn2_w13[layer].astype(BF16)
        w2_2 = ffn2_w2[layer].astype(BF16)
        w_in_b = jnp.pad(w_in[layer], ((0, 0), (0, D_IN_PAD - w_in.shape[2]))).astype(BF16)
        wo_b = w_out[layer].astype(BF16)
        fw = one_row(final_norm)

        x_all = _ffn_call(x_all, mods_l, one_row(norm_ffn1[layer]), w13_1, w2_1, fw,
                          n_rows=n_all, ks=(0, 1, 2), final=False, x_ctx=x_ctx if layer == 0 else None,
                          **common)
        hg, na, z, xbc_c, dt_raw = _inproj_call(x_all, mods_l, one_row(norm_mix[layer]), w_in_b,
                                                ssm_conv_w[layer].astype(F32), one_row(ssm_conv_b[layer]),
                                                n_rows=n_all, **common)

        o_f, o_b = _gla_call(hg, one_row(lower_bounds[0, layer]), one_row(lower_bounds[1, layer]),
                             tri_f, tri_b, ones_g, ind, n_rows=n_all, **common)

        bias = _na_bias_table(na_rpb[layer])
        nw = one_row(na_norm[layer])
        y_na = _na_call(na, bias, nw, **common)
        if not last:
            y_na = jnp.concatenate([y_na, _ctxattn_call(na, nw, **common)], axis=0)

        a_neg = -jnp.exp(ssm_a_log[layer].astype(F32))
        y_f, y_b = _ssd_call(xbc_c, dt_raw, _lane_row(a_neg.reshape(-1)),
                             _lane_row(ssm_dt_bias[layer].reshape(-1)),
                             e_f, e_b, ssd_tri_f, ssd_tri_b, n_rows=n_all, **common)

        n_out = n_lat if last else n_all
        dsk = jnp.repeat(ssm_d[layer].astype(F32), SSM_P).reshape(1, D_SSM)
        x_all = _outproj_call(x_all, mods_l, o_f, o_b, hg, y_na, y_f, y_b, xbc_c, z,
                              one_row(hg_norm[layer]), ind, dsk, one_row(ssm_norm[layer]), wo_b,
                              n_rows=n_out, **common)
        x_all = _ffn_call(x_all, mods_l, one_row(norm_ffn2[layer]), w13_2, w2_2, fw,
                          n_rows=n_out, ks=(6, 7, 8), final=last, **common)

    return x_all.reshape(nb, seq, d)
```

```python
import functools
import math

import jax
import jax.numpy as jnp
import numpy as np
from jax import lax
from jax.experimental import pallas as pl
from jax.experimental.pallas import tpu as pltpu

F32 = jnp.float32
BF16 = jnp.bfloat16

D_MODEL = 1024
GRID_W = 64
CTX_LEN = 256
EPS = 1e-6
N_MOD = 9
MASK_VALUE = -1e30
TINY = 1e-20
LOG2E = 1.4426950408889634

D_HG = 256
HG_HEADS = 4
HG_DK = 64
D_NA = 256
NA_HEADS = 4
NA_DH = 64
WIN_R = 8
WIN_C = 16
D_SSM = 512
SSM_HEADS = 8
SSM_P = 64
SSM_N = 128
SSM_G = 2
CONV_W = 5
CONV_DIM = D_SSM + 2 * SSM_G * SSM_N
D_FF = 2816
D_IN_PAD = 3712
LANES = 128
SUBLANES = 8
MXU_COLS = 256
MOD_ROWS = 8

GLA_TILE = 256
GLA_SUB = 32
GLA_SAFE_EXPONENT = 60.0
_GLA_GROUPS = GLA_SUB // SUBLANES
GLA_P_ROWS = (GLA_TILE // GLA_SUB) * SUBLANES * SUBLANES * _GLA_GROUPS * (_GLA_GROUPS + 1) // 2
SSD_Q = 128
SSD_STEP = 256
NA_ROWS = 8

VMEM_LIMIT = 56 * 1024 * 1024


def _cparams(sem):
    return pltpu.CompilerParams(dimension_semantics=sem, vmem_limit_bytes=VMEM_LIMIT)


def _silu(x):
    return x * jax.nn.sigmoid(x)


def _split3(x):
    hi = x.astype(BF16)
    r1 = x - hi.astype(F32)
    mid = r1.astype(BF16)
    lo = (r1 - mid.astype(F32)).astype(BF16)
    return hi, mid, lo


def _ldot3(a_bf16, x):
    hi, mid, lo = _split3(x)
    n = x.shape[1]
    if n % MXU_COLS == 0:
        return (jnp.dot(a_bf16, hi, preferred_element_type=F32)
                + jnp.dot(a_bf16, mid, preferred_element_type=F32)
                + jnp.dot(a_bf16, lo, preferred_element_type=F32))
    r = jnp.dot(a_bf16, jnp.concatenate([hi, mid, lo], axis=1), preferred_element_type=F32)
    return r[:, 0:n] + r[:, n:2 * n] + r[:, 2 * n:3 * n]


def _rdot3(x, e3_bf16):
    return jnp.dot(jnp.concatenate(_split3(x), axis=1), e3_bf16, preferred_element_type=F32)


def _dot_nt(a, b):
    return lax.dot_general(a, b, (((1,), (1,)), ((), ())), preferred_element_type=F32)


def _dot_tn(a, b):
    return lax.dot_general(a, b, (((0,), (0,)), ((), ())), preferred_element_type=F32)


def _modulated(x, nw, shift, scale):
    ms = jnp.mean(x * x, axis=-1, keepdims=True)
    y = x * lax.rsqrt(ms + EPS)
    return (y * nw) * (1.0 + scale) + shift


def _mods_kernel(c_ref, w_ref, b_ref, o_ref):
    sc = _silu(c_ref[...]).astype(BF16)
    o_ref[...] = jnp.dot(sc, w_ref[...].astype(BF16), preferred_element_type=F32) + b_ref[...]


def _mods_call(c_rows, w_mod, b_mod):
    depth = w_mod.shape[0]
    tn = 1152
    n = N_MOD * D_MODEL
    return pl.pallas_call(
        _mods_kernel,
        grid=(depth, n // tn),
        in_specs=[
            pl.BlockSpec((MOD_ROWS, D_MODEL), lambda l, j: (0, 0)),
            pl.BlockSpec((None, D_MODEL, tn), lambda l, j: (l, 0, j)),
            pl.BlockSpec((None, 1, tn), lambda l, j: (l, 0, j)),
        ],
        out_specs=pl.BlockSpec((None, MOD_ROWS, tn), lambda l, j: (l, 0, j)),
        out_shape=jax.ShapeDtypeStruct((depth, MOD_ROWS, n), F32),
        compiler_params=_cparams(("parallel", "parallel")),
        name="mods",
    )(c_rows, w_mod, b_mod.reshape(depth, 1, n))


def _mod_spec(tm, seq, nb):
    tpb = seq // tm
    return pl.BlockSpec((None, N_MOD, D_MODEL), lambda i: (jnp.minimum(i // tpb, nb), 0, 0))


def _ffn_kernel(*refs, k_shift, k_scale, k_gate, final, n_lat_tiles):
    if n_lat_tiles is None:
        x_ref, mod_ref, nw_ref, wu_ref, wg_ref, w2_ref, fw_ref, o_ref, act_ref = refs
        x = x_ref[...]
    else:
        x_ref, xc_ref, mod_ref, nw_ref, wu_ref, wg_ref, w2_ref, fw_ref, o_ref, act_ref = refs
        x = jnp.where(pl.program_id(0) >= n_lat_tiles, xc_ref[...], x_ref[...])
    h = _modulated(x, nw_ref[...], mod_ref[k_shift:k_shift + 1, :],
                   mod_ref[k_scale:k_scale + 1, :]).astype(BF16)
    cw = 256
    for c in range(D_FF // cw):
        u = jnp.dot(h, wu_ref[:, c * cw:(c + 1) * cw], preferred_element_type=F32)
        g = jnp.dot(h, wg_ref[:, c * cw:(c + 1) * cw], preferred_element_type=F32)
        act_ref[:, c * cw:(c + 1) * cw] = (_silu(g) * u).astype(BF16)
    y = jnp.dot(act_ref[...], w2_ref[...], preferred_element_type=F32)
    out = x + (0.5 * mod_ref[k_gate:k_gate + 1, :]) * y
    if final:
        ms = jnp.mean(out * out, axis=-1, keepdims=True)
        out = (out * lax.rsqrt(ms + EPS)) * fw_ref[...]
    o_ref[...] = out


def _ffn_call(x_all, mods_l, nw, w13b, w2b, fw, *, n_rows, seq, nb, ks, final, x_ctx=None, tm=512):
    if x_ctx is None:
        n_lat_tiles = None
        x_args = (x_all,)
        x_specs = [pl.BlockSpec((tm, D_MODEL), lambda i: (i, 0))]
    else:
        n_lat_tiles = nb * seq // tm
        x_args = (x_all, x_ctx)
        x_specs = [pl.BlockSpec((tm, D_MODEL), lambda i: (jnp.minimum(i, n_lat_tiles - 1), 0)),
                   pl.BlockSpec((tm, D_MODEL), lambda i: (jnp.maximum(i - n_lat_tiles, 0), 0))]
    kern = functools.partial(_ffn_kernel, k_shift=ks[0], k_scale=ks[1], k_gate=ks[2], final=final,
                             n_lat_tiles=n_lat_tiles)
    return pl.pallas_call(
        kern,
        grid=(n_rows // tm,),
        in_specs=x_specs + [
            _mod_spec(tm, seq, nb),
            pl.BlockSpec((1, D_MODEL), lambda i: (0, 0)),
            pl.BlockSpec((D_MODEL, D_FF), lambda i: (0, 0), pipeline_mode=pl.Buffered(1)),
            pl.BlockSpec((D_MODEL, D_FF), lambda i: (0, 1), pipeline_mode=pl.Buffered(1)),
            pl.BlockSpec((D_FF, D_MODEL), lambda i: (0, 0), pipeline_mode=pl.Buffered(1)),
            pl.BlockSpec((1, D_MODEL), lambda i: (0, 0)),
        ],
        out_specs=pl.BlockSpec((tm, D_MODEL), lambda i: (i, 0)),
        out_shape=jax.ShapeDtypeStruct((n_rows, D_MODEL), F32),
        scratch_shapes=[pltpu.VMEM((tm, D_FF), BF16)],
        compiler_params=_cparams(("parallel",)),
        name="ffn",
    )(*x_args, mods_l, nw, w13b, w13b, w2b, fw)


_IN_SPLITS = ((0, 1280), (1280, 2048), (2048, 2560), (2560, 3584), (3584, 3712))


_XBC_COLS = _IN_SPLITS[3]
HALO = 8


def _inproj_kernel(x_ref, xp_ref, xn_ref, mod_ref, nw_ref, w_ref, cw_ref, cb_ref,
                   hg_ref, na_ref, z_ref, xbc_ref, dt_ref, ext_ref, *, n_lat_tiles, tpb):
    tm = x_ref.shape[0]
    i = pl.program_id(0)
    is_ctx = i >= n_lat_tiles
    first = jnp.logical_or(is_ctx, (i % tpb) == 0)
    last = jnp.logical_or(is_ctx, (i % tpb) == tpb - 1)
    nw = nw_ref[...]
    shift = mod_ref[3:4, :]
    scale = mod_ref[4:5, :]
    h = _modulated(x_ref[...], nw, shift, scale).astype(BF16)
    x_halo = jnp.concatenate([xp_ref[...], xn_ref[...]], axis=0)
    h_halo = _modulated(x_halo, nw, shift, scale).astype(BF16)

    a, b = _XBC_COLS
    wx = w_ref[:, a:b]
    halo = jnp.dot(h_halo, wx, preferred_element_type=F32)
    ext_ref[0:HALO, :] = jnp.where(first, 0.0, halo[0:HALO, :])
    ext_ref[HALO:HALO + tm, :] = jnp.dot(h, wx, preferred_element_type=F32)
    ext_ref[HALO + tm:2 * HALO + tm, :] = jnp.where(last, 0.0, halo[HALO:2 * HALO, :])
    pos = lax.broadcasted_iota(jnp.int32, (tm, 1), 0) % CTX_LEN
    acc = jnp.zeros((tm, CONV_DIM), F32) + cb_ref[...]
    for j in range(CONV_W):
        d = j - CONV_W // 2
        tap = ext_ref[HALO + d:HALO + d + tm, :] * cw_ref[j:j + 1, :]
        if d != 0:
            crosses = jnp.logical_and(is_ctx, jnp.logical_or(pos + d < 0, pos + d >= CTX_LEN))
            tap = jnp.where(crosses, 0.0, tap)
        acc = acc + tap
    xbc_ref[...] = _silu(acc)

    for (a, b), o_ref in zip(_IN_SPLITS, (hg_ref, na_ref, z_ref, None, dt_ref)):
        if o_ref is not None:
            o_ref[...] = jnp.dot(h, w_ref[:, a:b], preferred_element_type=F32).astype(o_ref.dtype)


def _inproj_call(x_all, mods_l, nw, w_in_b, conv_w, conv_b, *, n_rows, seq, nb, tm=512):
    widths = [b - a for a, b in _IN_SPLITS]
    dtypes = [F32, BF16, F32, F32, F32]
    r8 = tm // HALO
    last8 = n_rows // HALO - 1
    kern = functools.partial(_inproj_kernel, n_lat_tiles=nb * seq // tm, tpb=seq // tm)
    return pl.pallas_call(
        kern,
        grid=(n_rows // tm,),
        in_specs=[
            pl.BlockSpec((tm, D_MODEL), lambda i: (i, 0)),
            pl.BlockSpec((HALO, D_MODEL), lambda i: (jnp.maximum(i * r8 - 1, 0), 0)),
            pl.BlockSpec((HALO, D_MODEL), lambda i: (jnp.minimum((i + 1) * r8, last8), 0)),
            _mod_spec(tm, seq, nb),
            pl.BlockSpec((1, D_MODEL), lambda i: (0, 0)),
            pl.BlockSpec((D_MODEL, D_IN_PAD), lambda i: (0, 0), pipeline_mode=pl.Buffered(1)),
            pl.BlockSpec((CONV_W, CONV_DIM), lambda i: (0, 0)),
            pl.BlockSpec((1, CONV_DIM), lambda i: (0, 0)),
        ],
        out_specs=[pl.BlockSpec((tm, w), lambda i: (i, 0)) for w in widths],
        out_shape=[jax.ShapeDtypeStruct((n_rows, w), dt) for w, dt in zip(widths, dtypes)],
        scratch_shapes=[pltpu.VMEM((tm + 2 * HALO, CONV_DIM), F32)],
        compiler_params=_cparams(("parallel",)),
        name="inproj",
    )(x_all, x_all, x_all, mods_l, nw, w_in_b, conv_w, conv_b)


def _scan_block_map(*, seq, nb, q, rev):
    nc_ctx = CTX_LEN // q
    nc_lat = seq // q
    ctx_base = nb * seq // q

    def row_block(b, j):
        cj = j
        lj = j - nc_ctx
        if rev:
            cj = nc_ctx - 1 - cj
            lj = nc_lat - 1 - lj
        return jnp.where(j < nc_ctx, ctx_base + b * nc_ctx + cj, b * nc_lat + lj)

    return row_block, nc_ctx + nc_lat


def _round_robin(stage_generators):
    live = list(stage_generators)
    while live:
        for gen in list(live):
            try:
                next(gen)
            except StopIteration:
                live.remove(gen)


def _ssd_direction(xbc_ref, dt_ref, arow_ref, brow_ref, e_ref, tri_ref, y_ref, st_ref, *, rev, lane0):
    q = SSD_Q
    nch = SSD_STEP // q
    gw = D_SSM // SSM_G
    t_idx = lax.broadcasted_iota(jnp.int32, (q, q), 0)
    s_idx = lax.broadcasted_iota(jnp.int32, (q, q), 1)
    causal = (s_idx >= t_idx) if rev else (s_idx <= t_idx)
    lane = lax.broadcasted_iota(jnp.int32, (q, LANES), 1)
    low_half = lane < SSM_P
    e = e_ref[...]
    tri = tri_ref[...]
    edge = 0 if rev else q - 1
    chunks = [dict(rows=slice(ci * q, (ci + 1) * q))
              for ci in (range(nch - 1, -1, -1) if rev else range(nch))]

    def stage_cumsum(ch):
        rows = ch["rows"]
        dt = jax.nn.softplus(dt_ref[rows, :] + brow_ref[...])
        ch["dt"] = dt
        ch["cum"] = _ldot3(tri, dt * arow_ref[...])

    def stage_expand(ch):
        rows = ch["rows"]
        cum = ch["cum"]
        ch["cum_t"] = cum.T
        cum_e = _rdot3(cum, e)
        ch["cum_e"] = cum_e
        ch["tot_e"] = cum_e[edge:edge + 1, :]
        ch["xdt"] = xbc_ref[rows, 0:D_SSM] * _rdot3(ch["dt"], e)
        ch["bg"] = []
        ch["cg"] = []
        ch["cb"] = []
        for g in range(SSM_G):
            b0 = D_SSM + g * SSM_N
            c0 = D_SSM + SSM_G * SSM_N + g * SSM_N
            bg = xbc_ref[rows, b0:b0 + SSM_N].astype(BF16)
            cg = xbc_ref[rows, c0:c0 + SSM_N].astype(BF16)
            ch["bg"].append(bg)
            ch["cg"].append(cg)
            ch["cb"].append(_dot_nt(cg, bg))

    def stage_local(ch):
        cum, cum_t, cum_e, tot_e, xdt = ch["cum"], ch["cum_t"], ch["cum_e"], ch["tot_e"], ch["xdt"]
        ch["y_diag"] = []
        ch["upd"] = []
        for g in range(SSM_G):
            gl = g * gw
            for hp in range(2):
                pl0 = gl + hp * LANES
                xpair = xdt[:, pl0:pl0 + LANES]
                acc = jnp.zeros((q, LANES), F32)
                for hh in range(2):
                    idx = lane0 + g * 4 + hp * 2 + hh
                    diff = cum[:, idx:idx + 1] - cum_t[idx:idx + 1, :]
                    m = ch["cb"][g] * jnp.exp(jnp.where(causal, diff, MASK_VALUE))
                    xm = jnp.where(low_half if hh == 0 else jnp.logical_not(low_half), xpair, 0.0)
                    acc = acc + jnp.dot(m.astype(BF16), xm.astype(BF16), preferred_element_type=F32)
                ch["y_diag"].append(acc)
            xw = (xdt[:, gl:gl + gw] * jnp.exp(tot_e[:, gl:gl + gw] - cum_e[:, gl:gl + gw])).astype(BF16)
            ch["upd"].append(_dot_tn(ch["bg"][g], xw))
        ch["off_scale"] = jnp.exp(cum_e)
        ch["dec"] = jnp.exp(tot_e)

    def stage_state():
        st = [st_ref[:, g * gw:(g + 1) * gw] for g in range(SSM_G)]
        for ch in chunks:
            y_parts = []
            for g in range(SSM_G):
                gl = g * gw
                y_off = jnp.dot(ch["cg"][g], st[g].astype(BF16), preferred_element_type=F32)
                y_parts.append(jnp.concatenate(ch["y_diag"][2 * g:2 * g + 2], axis=1)
                               + y_off * ch["off_scale"][:, gl:gl + gw])
                st[g] = st[g] * ch["dec"][:, gl:gl + gw] + ch["upd"][g]
            y_ref[ch["rows"], :] = jnp.concatenate(y_parts, axis=1)
        for g in range(SSM_G):
            st_ref[:, g * gw:(g + 1) * gw] = st[g]

    for ch in chunks:
        yield stage_cumsum(ch)
    for ch in chunks:
        yield stage_expand(ch)
    for ch in chunks:
        yield stage_local(ch)
    yield stage_state()


def _ssd_kernel(xf_ref, xb_ref, dtf_ref, dtb_ref, arow_ref, brow_ref, ef_ref, eb_ref, trif_ref, trib_ref,
                yf_ref, yb_ref, stf_ref, stb_ref):
    @pl.when(pl.program_id(1) == 0)
    def _():
        stf_ref[...] = jnp.zeros_like(stf_ref)
        stb_ref[...] = jnp.zeros_like(stb_ref)

    _round_robin([
        _ssd_direction(xf_ref, dtf_ref, arow_ref, brow_ref, ef_ref, trif_ref, yf_ref, stf_ref,
                       rev=False, lane0=0),
        _ssd_direction(xb_ref, dtb_ref, arow_ref, brow_ref, eb_ref, trib_ref, yb_ref, stb_ref,
                       rev=True, lane0=SSM_HEADS)])


def _ssd_call(xbc_c, dt_raw, a_row, b_row, e_f, e_b, tri_f, tri_b, *, n_rows, seq, nb):
    q = SSD_STEP
    blk_f, steps = _scan_block_map(seq=seq, nb=nb, q=q, rev=False)
    blk_b, _ = _scan_block_map(seq=seq, nb=nb, q=q, rev=True)
    const = lambda b, j: (0, 0)
    return pl.pallas_call(
        _ssd_kernel,
        grid=(nb, steps),
        in_specs=[
            pl.BlockSpec((q, CONV_DIM), lambda b, j: (blk_f(b, j), 0)),
            pl.BlockSpec((q, CONV_DIM), lambda b, j: (blk_b(b, j), 0)),
            pl.BlockSpec((q, LANES), lambda b, j: (blk_f(b, j), 0)),
            pl.BlockSpec((q, LANES), lambda b, j: (blk_b(b, j), 0)),
            pl.BlockSpec((1, LANES), const),
            pl.BlockSpec((1, LANES), const),
            pl.BlockSpec((3 * LANES, D_SSM), const),
            pl.BlockSpec((3 * LANES, D_SSM), const),
            pl.BlockSpec((SSD_Q, SSD_Q), const),
            pl.BlockSpec((SSD_Q, SSD_Q), const),
        ],
        out_specs=[pl.BlockSpec((q, D_SSM), lambda b, j: (blk_f(b, j), 0)),
                   pl.BlockSpec((q, D_SSM), lambda b, j: (blk_b(b, j), 0))],
        out_shape=[jax.ShapeDtypeStruct((n_rows, D_SSM), F32)] * 2,
        scratch_shapes=[pltpu.VMEM((SSM_N, D_SSM), F32)] * 2,
        compiler_params=_cparams(("arbitrary", "arbitrary")),
        name="ssd",
    )(xbc_c, xbc_c, dt_raw, dt_raw, a_row, b_row, e_f, e_b, tri_f, tri_b)


def _gla_direction(q_ref, f_ref, v_ref, lb_ref, tri_ref, ones_ref, ind_ref, o_ref, p_s, upd_s, oi_s, qkb_s,
                   st_ref, *, rev):
    t = GLA_TILE
    c = GLA_SUB
    nsub = t // c
    lb = lb_ref[...]
    fr = f_ref[...]
    f = lb + (1.0 - lb) * jax.nn.sigmoid(fr)
    logf = jnp.log(jnp.maximum(f, TINY))
    k = (1.0 - lb) * jax.nn.sigmoid(-fr)
    qv = _silu(q_ref[...])
    vv = v_ref[...]
    yield
    brel = _ldot3(tri_ref[...], logf)
    tot = _ldot3(ones_ref[...], logf)
    qkb_s[0] = qv
    qkb_s[1] = k
    qkb_s[2] = brel
    yield

    qt = (qv * jnp.exp(brel)).astype(BF16)
    kt = k * jnp.exp(tot - brel)
    dec_blk = jnp.exp(tot)
    lane = lax.broadcasted_iota(jnp.int32, (t, D_HG), 1)
    head_masks = [(lane >= h * HG_DK) & (lane < (h + 1) * HG_DK) for h in range(HG_HEADS)]
    kxs = [jnp.where(m, kt, 0.0).astype(BF16) for m in head_masks]
    vxs = [jnp.where(m, vv, 0.0).astype(BF16) for m in head_masks]
    for blk in range(nsub):
        r0 = blk * c
        kx = jnp.concatenate([a[r0:r0 + c, :] for a in kxs], axis=0)
        vx = jnp.concatenate([a[r0:r0 + c, :] for a in vxs], axis=0)
        upd_s[blk] = _dot_tn(vx, kx)
        if blk % 4 == 3:
            yield

    mid = c // 2
    bref = jnp.concatenate([jnp.broadcast_to(brel[b * c + mid:b * c + mid + 1, :], (c, D_HG))
                            for b in range(nsub)], axis=0)
    dev = brel - bref
    worst = jnp.max(jnp.max(jnp.abs(dev), axis=1, keepdims=True), axis=0, keepdims=True)
    safe_v = worst <= GLA_SAFE_EXPONENT
    safe = worst[0, 0] <= GLA_SAFE_EXPONENT
    qh = qv * jnp.exp(dev)
    kh = (k * jnp.exp(-dev)).astype(BF16)
    qx = jnp.concatenate([jnp.where(m, qh, 0.0).astype(BF16) for m in head_masks], axis=0)
    yield
    sc = _dot_nt(qx, kh)
    visible = tri_ref[...].astype(F32) > 0.5
    pm = jnp.concatenate([jnp.where(visible, sc[h * t:(h + 1) * t, :], 0.0).astype(BF16)
                          for h in range(HG_HEADS)], axis=0)
    yield
    oh = jnp.dot(pm, vv.astype(BF16), preferred_element_type=F32)
    o_fast = jnp.where(head_masks[0], oh[0:t, :], 0.0)
    for h in range(1, HG_HEADS):
        o_fast = o_fast + jnp.where(head_masks[h], oh[h * t:(h + 1) * t, :], 0.0)
    oi_s[...] = jnp.where(safe_v, o_fast, 0.0)
    yield

    @pl.when(jnp.logical_not(safe))
    def _():
        _gla_intra_pairwise(qkb_s, v_ref, ind_ref, p_s, oi_s, rev=rev)

    yield

    st = st_ref[...]
    o_inter = [None] * nsub
    for blk in (range(nsub - 1, -1, -1) if rev else range(nsub)):
        r0 = blk * c
        o_inter[blk] = _dot_nt(qt[r0:r0 + c, :], st.astype(BF16))
        st = st * dec_blk[r0:r0 + 1, :] + upd_s[blk]
        yield
    st_ref[...] = st
    o_ref[...] = oi_s[...] + jnp.concatenate(o_inter, axis=0)


def _gla_intra_pairwise(qkb_s, v_ref, ind_ref, p_s, oi_s, *, rev):
    t = GLA_TILE
    c = GLA_SUB
    nsub = t // c
    sub = SUBLANES
    qv, k, brel = qkb_s[0], qkb_s[1], qkb_s[2]
    vv = v_ref[...]
    t_idx = lax.broadcasted_iota(jnp.int32, (sub, D_HG), 0)
    pieces = [(s, g) for s in range(c) for g in range(c // sub)
              if (g <= s // sub if rev else g >= s // sub)]
    rows_per_blk = len(pieces) * sub

    brel2 = brel * LOG2E
    for blk in range(nsub):
        r0 = blk * c
        qb = qv[r0:r0 + c, :]
        kb = k[r0:r0 + c, :]
        bb = brel2[r0:r0 + c, :]
        for u in range(0, len(pieces), 2):
            rows = []
            for s, g in pieces[u:u + 2]:
                tt = t_idx + g * sub
                keep = (tt <= s) if rev else (tt >= s)
                dec = jnp.exp2(jnp.where(keep, bb[g * sub:(g + 1) * sub, :] - bb[s:s + 1, :], MASK_VALUE))
                rows.append(qb[g * sub:(g + 1) * sub, :] * kb[s:s + 1, :] * dec)
            po = blk * rows_per_blk + u * sub
            p_s[po:po + 2 * sub, :] = jnp.concatenate(rows, axis=0).astype(BF16)
    r = jnp.dot(p_s[...], ind_ref[...], preferred_element_type=F32)
    for blk in range(nsub):
        r0 = blk * c
        vb = vv[r0:r0 + c, :]
        accs = [jnp.zeros((sub, D_HG), F32) for _ in range(c // sub)]
        for n, (s, g) in enumerate(pieces):
            po = blk * rows_per_blk + n * sub
            accs[g] = accs[g] + r[po:po + sub, :] * vb[s:s + 1, :]
        oi_s[r0:r0 + c, :] = jnp.concatenate(accs, axis=0)


def _gla_kernel(qf_ref, ff_ref, vf_ref, qb_ref, fb_ref, vb_ref, lbf_ref, lbb_ref, trif_ref, trib_ref,
                ones_ref, ind_ref, of_ref, ob_ref, pf_s, pb_s, updf_s, updb_s, oif_s, oib_s, qkbf_s, qkbb_s,
                stf_ref, stb_ref):
    @pl.when(pl.program_id(1) == 0)
    def _():
        stf_ref[...] = jnp.zeros_like(stf_ref)
        stb_ref[...] = jnp.zeros_like(stb_ref)

    _round_robin([
        _gla_direction(qf_ref, ff_ref, vf_ref, lbf_ref, trif_ref, ones_ref, ind_ref, of_ref,
                       pf_s, updf_s, oif_s, qkbf_s, stf_ref, rev=False),
        _gla_direction(qb_ref, fb_ref, vb_ref, lbb_ref, trib_ref, ones_ref, ind_ref, ob_ref,
                       pb_s, updb_s, oib_s, qkbb_s, stb_ref, rev=True)])


def _gla_call(hg, lb_f, lb_b, tri_f, tri_b, ones, ind, *, n_rows, seq, nb):
    t = GLA_TILE
    blk_f, steps = _scan_block_map(seq=seq, nb=nb, q=t, rev=False)
    blk_b, _ = _scan_block_map(seq=seq, nb=nb, q=t, rev=True)
    const = lambda b, j: (0, 0)
    col = lambda blk, cidx: pl.BlockSpec((t, D_HG), lambda b, j: (blk(b, j), cidx))
    return pl.pallas_call(
        _gla_kernel,
        grid=(nb, steps),
        in_specs=[
            col(blk_f, 0), col(blk_f, 1), col(blk_f, 3),
            col(blk_b, 0), col(blk_b, 2), col(blk_b, 3),
            pl.BlockSpec((1, D_HG), const),
            pl.BlockSpec((1, D_HG), const),
            pl.BlockSpec((t, t), const),
            pl.BlockSpec((t, t), const),
            pl.BlockSpec((t, t), const),
            pl.BlockSpec((D_HG, D_HG), const),
        ],
        out_specs=[col(blk_f, 0), col(blk_b, 0)],
        out_shape=[jax.ShapeDtypeStruct((n_rows, D_HG), F32)] * 2,
        scratch_shapes=(
            [pltpu.VMEM((GLA_P_ROWS, D_HG), BF16)] * 2
            + [pltpu.VMEM((GLA_TILE // GLA_SUB, D_HG, D_HG), F32)] * 2
            + [pltpu.VMEM((GLA_TILE, D_HG), F32)] * 2
            + [pltpu.VMEM((3, GLA_TILE, D_HG), F32)] * 2
            + [pltpu.VMEM((D_HG, D_HG), F32)] * 2),
        compiler_params=_cparams(("arbitrary", "arbitrary")),
        name="gla",
    )(hg, hg, hg, hg, hg, hg, lb_f, lb_b, tri_f, tri_b, ones, ind)


def _expand_heads(x, masks):
    return jnp.concatenate([jnp.where(m, x, jnp.zeros_like(x)) for m in masks], axis=0)


def _collapse_heads(o, masks, t):
    acc = jnp.where(masks[0], o[0:t, :], 0.0)
    for h in range(1, NA_HEADS):
        acc = acc + jnp.where(masks[h], o[h * t:(h + 1) * t, :], 0.0)
    return acc


def _na_finish(o, nw):
    ms = jnp.mean(o * o, axis=-1, keepdims=True)
    return (o * lax.rsqrt(ms + EPS)) * nw


def _na_kernel(q_ref, k_ref, v_ref, kc_ref, vc_ref, bias_ref, nw_ref, o_ref, *, n_grid_rows):
    w = GRID_W
    i = pl.program_id(1)
    lane = lax.broadcasted_iota(jnp.int32, (w, D_NA), 1)
    masks = [(lane >= h * NA_DH) & (lane < (h + 1) * NA_DH) for h in range(NA_HEADS)]
    kc = kc_ref[...]
    vc = vc_ref[...]
    nw = nw_ref[...]
    scale = NA_DH ** -0.5

    def body(jr, carry):
        r = i * NA_ROWS + jr
        r0 = jnp.clip(r - WIN_R // 2, 0, n_grid_rows - WIN_R)
        var = r - r0
        k0 = pl.multiple_of(r0 * w, w)
        q0 = pl.multiple_of(jr * w, w)
        qx = _expand_heads(q_ref[pl.ds(q0, w), :] * scale, masks)
        kw = k_ref[pl.ds(k0, WIN_R * w), :]
        vw = v_ref[pl.ds(k0, WIN_R * w), :]
        s_loc = _dot_nt(qx, kw) + bias_ref[var]
        s_ctx = _dot_nt(qx, kc)
        m = jnp.maximum(jnp.max(s_loc, axis=-1, keepdims=True), jnp.max(s_ctx, axis=-1, keepdims=True))
        p_loc = jnp.exp(s_loc - m)
        p_ctx = jnp.exp(s_ctx - m)
        den = jnp.sum(p_loc, axis=-1, keepdims=True) + jnp.sum(p_ctx, axis=-1, keepdims=True)
        o = (jnp.dot(p_loc.astype(BF16), vw, preferred_element_type=F32)
             + jnp.dot(p_ctx.astype(BF16), vc, preferred_element_type=F32)) / den
        o_ref[pl.ds(q0, w), :] = _na_finish(_collapse_heads(o, masks, w), nw)
        return carry

    lax.fori_loop(0, NA_ROWS, body, 0, unroll=True)


def _na_call(na, bias, nw, *, seq, nb):
    rows = seq // GRID_W
    tq = NA_ROWS * GRID_W
    qpb = seq // tq
    ctx_blk = nb * seq // CTX_LEN
    kern = functools.partial(_na_kernel, n_grid_rows=rows)
    return pl.pallas_call(
        kern,
        grid=(nb, qpb),
        in_specs=[
            pl.BlockSpec((tq, D_NA), lambda b, i: (b * qpb + i, 0)),
            pl.BlockSpec((seq, D_NA), lambda b, i: (b, 1)),
            pl.BlockSpec((seq, D_NA), lambda b, i: (b, 2)),
            pl.BlockSpec((CTX_LEN, D_NA), lambda b, i: (ctx_blk + b, 1)),
            pl.BlockSpec((CTX_LEN, D_NA), lambda b, i: (ctx_blk + b, 2)),
            pl.BlockSpec((WIN_R, NA_HEADS * GRID_W, WIN_R * GRID_W), lambda b, i: (0, 0, 0)),
            pl.BlockSpec((1, D_NA), lambda b, i: (0, 0)),
        ],
        out_specs=pl.BlockSpec((tq, D_NA), lambda b, i: (b * qpb + i, 0)),
        out_shape=jax.ShapeDtypeStruct((nb * seq, D_NA), F32),
        compiler_params=_cparams(("parallel", "arbitrary")),
        name="natten",
    )(na, na, na, na, na, bias, nw)


def _ctxattn_kernel(q_ref, k_ref, v_ref, nw_ref, o_ref):
    t = CTX_LEN
    lane = lax.broadcasted_iota(jnp.int32, (t, D_NA), 1)
    masks = [(lane >= h * NA_DH) & (lane < (h + 1) * NA_DH) for h in range(NA_HEADS)]
    qx = _expand_heads(q_ref[...] * (NA_DH ** -0.5), masks)
    s = _dot_nt(qx, k_ref[...])
    m = jnp.max(s, axis=-1, keepdims=True)
    p = jnp.exp(s - m)
    den = jnp.sum(p, axis=-1, keepdims=True)
    o = jnp.dot(p.astype(BF16), v_ref[...], preferred_element_type=F32) / den
    o_ref[...] = _na_finish(_collapse_heads(o, masks, t), nw_ref[...])


def _ctxattn_call(na, nw, *, seq, nb):
    ctx_blk = nb * seq // CTX_LEN
    return pl.pallas_call(
        _ctxattn_kernel,
        grid=(nb,),
        in_specs=[
            pl.BlockSpec((CTX_LEN, D_NA), lambda b: (ctx_blk + b, 0)),
            pl.BlockSpec((CTX_LEN, D_NA), lambda b: (ctx_blk + b, 1)),
            pl.BlockSpec((CTX_LEN, D_NA), lambda b: (ctx_blk + b, 2)),
            pl.BlockSpec((1, D_NA), lambda b: (0, 0)),
        ],
        out_specs=pl.BlockSpec((CTX_LEN, D_NA), lambda b: (b, 0)),
        out_shape=jax.ShapeDtypeStruct((nb * CTX_LEN, D_NA), F32),
        compiler_params=_cparams(("parallel",)),
        name="ctxattn",
    )(na, na, na, nw)


def _outproj_kernel(x_ref, mod_ref, of_ref, ob_ref, g_ref, na_ref, yf_ref, yb_ref, xs_ref, z_ref,
                    hgw_ref, hm_ref, dsk_ref, sw_ref, wo_ref, o_ref):
    o = of_ref[...] + ob_ref[...]
    sq = o * o
    hi = sq.astype(BF16)
    lo = (sq - hi.astype(F32)).astype(BF16)
    hm = hm_ref[...]
    ms = (jnp.dot(hi, hm, preferred_element_type=F32)
          + jnp.dot(lo, hm, preferred_element_type=F32)) * (1.0 / HG_DK)
    hg = (o * lax.rsqrt(ms + EPS)) * hgw_ref[...] * _silu(g_ref[...])
    ys = (yf_ref[...] + yb_ref[...] + dsk_ref[...] * xs_ref[...]) * _silu(z_ref[...])
    ms2 = jnp.mean(ys * ys, axis=-1, keepdims=True)
    ssm = (ys * lax.rsqrt(ms2 + EPS)) * sw_ref[...]
    mix = (jnp.dot(hg.astype(BF16), wo_ref[0:D_HG, :], preferred_element_type=F32)
           + jnp.dot(na_ref[...].astype(BF16), wo_ref[D_HG:D_HG + D_NA, :], preferred_element_type=F32)
           + jnp.dot(ssm.astype(BF16), wo_ref[D_HG + D_NA:, :], preferred_element_type=F32))
    o_ref[...] = x_ref[...] + mod_ref[5:6, :] * mix


def _outproj_call(x_all, mods_l, o_f, o_b, hg, y_na, y_f, y_b, xbc_c, z, hgw, hm, dsk, sw, wo_b,
                  *, n_rows, seq, nb, tm=512):
    row = lambda i: (i, 0)
    const = lambda i: (0, 0)
    return pl.pallas_call(
        _outproj_kernel,
        grid=(n_rows // tm,),
        in_specs=[
            pl.BlockSpec((tm, D_MODEL), row),
            _mod_spec(tm, seq, nb),
            pl.BlockSpec((tm, D_HG), row),
            pl.BlockSpec((tm, D_HG), row),
            pl.BlockSpec((tm, D_HG), lambda i: (i, 4)),
            pl.BlockSpec((tm, D_NA), row),
            pl.BlockSpec((tm, D_SSM), row),
            pl.BlockSpec((tm, D_SSM), row),
            pl.BlockSpec((tm, D_SSM), row),
            pl.BlockSpec((tm, D_SSM), row),
            pl.BlockSpec((1, D_HG), const),
            pl.BlockSpec((D_HG, D_HG), const),
            pl.BlockSpec((1, D_SSM), const),
            pl.BlockSpec((1, D_SSM), const),
            pl.BlockSpec((D_MODEL, D_MODEL), const),
        ],
        out_specs=pl.BlockSpec((tm, D_MODEL), row),
        out_shape=jax.ShapeDtypeStruct((n_rows, D_MODEL), F32),
        compiler_params=_cparams(("parallel",)),
        name="outproj",
    )(x_all, mods_l, o_f, o_b, hg, y_na, y_f, y_b, xbc_c, z, hgw, hm, dsk, sw, wo_b)


def _na_bias_table(rpb):
    w = GRID_W
    ndr = 2 * WIN_R - 1
    ndc = 2 * WIN_C - 1
    col = np.arange(w)
    c0 = np.clip(col - WIN_C // 2, 0, w - WIN_C)
    col_in = (col[None, :] >= c0[:, None]) & (col[None, :] < c0[:, None] + WIN_C)
    dc = np.clip(col[None, :] - col[:, None], -(WIN_C - 1), WIN_C - 1) + (WIN_C - 1)
    onehot = (dc.reshape(1, -1) == np.arange(ndc)[:, None]).astype(np.float32)
    t = jnp.dot(rpb.reshape(NA_HEADS * ndr, ndc).astype(F32), onehot,
                precision=lax.Precision.HIGHEST).reshape(NA_HEADS, ndr, w, w)
    t = jnp.where(col_in[None, None], t, MASK_VALUE)
    b = jnp.stack([t[:, WIN_R - 1 - var:2 * WIN_R - 1 - var] for var in range(WIN_R)], axis=0)
    b = jnp.transpose(b, (0, 1, 3, 2, 4))
    return b.reshape(WIN_R, NA_HEADS * w, WIN_R * w)


def _block_tri(n, c, rev):
    r = jnp.arange(n)
    same = (r[:, None] // c) == (r[None, :] // c)
    tri = (r[None, :] >= r[:, None]) if rev else (r[None, :] <= r[:, None])
    return (same & tri).astype(BF16), same.astype(BF16)


def _head_block_ones(n, hd):
    r = jnp.arange(n)
    return ((r[:, None] // hd) == (r[None, :] // hd)).astype(BF16)


def _ssd_expand(lane0):
    r = jnp.arange(LANES)[:, None]
    cidx = jnp.arange(D_SSM)[None, :]
    e = (r == lane0 + cidx // SSM_P).astype(BF16)
    return jnp.concatenate([e, e, e], axis=0)


def _lane_row(vals):
    return jnp.zeros((1, LANES), F32).at[0, :vals.shape[0]].set(vals.astype(F32))


def kernel(x, c, ctx, c_ctx, w_mod, b_mod, norm_ffn1, ffn1_w13, ffn1_w2, norm_mix, w_in,
           hg_lower_bounds, hg_norm, na_rpb, na_norm, ssm_conv_w, ssm_conv_b, ssm_a_log,
           ssm_dt_bias, ssm_d, ssm_norm, w_out, norm_ffn2, ffn2_w13, ffn2_w2, final_norm):
    nb, seq, d = x.shape
    depth = w_mod.shape[0]
    assert d == D_MODEL and ctx.shape[1] == CTX_LEN and nb + 1 <= MOD_ROWS
    assert seq % 512 == 0 and seq // GRID_W >= WIN_R
    n_lat = nb * seq
    n_all = n_lat + nb * CTX_LEN

    lb_soft = jax.nn.softmax(hg_lower_bounds.astype(F32), axis=1)
    lower_bounds = jnp.cumsum(lb_soft, axis=1) - lb_soft[:, :1]

    c_rows = jnp.zeros((MOD_ROWS, d), F32).at[:nb].set(c).at[nb].set(c_ctx)
    mods = _mods_call(c_rows, w_mod, b_mod).reshape(depth, MOD_ROWS, N_MOD, d)

    x_all = x.reshape(n_lat, d)
    x_ctx = ctx.reshape(nb * CTX_LEN, d)

    tri_f, ones_g = _block_tri(GLA_TILE, GLA_SUB, False)
    tri_b, _ = _block_tri(GLA_TILE, GLA_SUB, True)
    ind = _head_block_ones(D_HG, HG_DK)
    ssd_tri_f, _ = _block_tri(SSD_Q, SSD_Q, False)
    ssd_tri_b, _ = _block_tri(SSD_Q, SSD_Q, True)
    e_f = _ssd_expand(0)
    e_b = _ssd_expand(SSM_HEADS)
    one_row = lambda v: v.reshape(1, -1).astype(F32)
    common = dict(seq=seq, nb=nb)

    for layer in range(depth):
        last = layer == depth - 1
        mods_l = mods[layer]
        w13_1 = ffn1_w13[layer].astype(BF16)
        w2_1 = ffn1_w2[layer].astype(BF16)
        w13_2 = ffn2_w13[layer].astype(BF16)
        w2_2 = ffn2_w2[layer].astype(BF16)
        w_in_b = jnp.pad(w_in[layer], ((0, 0), (0, D_IN_PAD - w_in.shape[2]))).astype(BF16)
        wo_b = w_out[layer].astype(BF16)
        fw = one_row(final_norm)

        x_all = _ffn_call(x_all, mods_l, one_row(norm_ffn1[layer]), w13_1, w2_1, fw,
                          n_rows=n_all, ks=(0, 1, 2), final=False, x_ctx=x_ctx if layer == 0 else None,
                          **common)
        hg, na, z, xbc_c, dt_raw = _inproj_call(x_all, mods_l, one_row(norm_mix[layer]), w_in_b,
                                                ssm_conv_w[layer].astype(F32), one_row(ssm_conv_b[layer]),
                                                n_rows=n_all, **common)

        o_f, o_b = _gla_call(hg, one_row(lower_bounds[0, layer]), one_row(lower_bounds[1, layer]),
                             tri_f, tri_b, ones_g, ind, n_rows=n_all, **common)

        bias = _na_bias_table(na_rpb[layer])
        nw = one_row(na_norm[layer])
        y_na = _na_call(na, bias, nw, **common)
        if not last:
            y_na = jnp.concatenate([y_na, _ctxattn_call(na, nw, **common)], axis=0)

        a_neg = -jnp.exp(ssm_a_log[layer].astype(F32))
        y_f, y_b = _ssd_call(xbc_c, dt_raw, _lane_row(a_neg.reshape(-1)),
                             _lane_row(ssm_dt_bias[layer].reshape(-1)),
                             e_f, e_b, ssd_tri_f, ssd_tri_b, n_rows=n_all, **common)

        n_out = n_lat if last else n_all
        dsk = jnp.repeat(ssm_d[layer].astype(F32), SSM_P).reshape(1, D_SSM)
        x_all = _outproj_call(x_all, mods_l, o_f, o_b, hg, y_na, y_f, y_b, xbc_c, z,
                              one_row(hg_norm[layer]), ind, dsk, one_row(ssm_norm[layer]), wo_b,
                              n_rows=n_out, **common)
        x_all = _ffn_call(x_all, mods_l, one_row(norm_ffn2[layer]), w13_2, w2_2, fw,
                          n_rows=n_out, ks=(6, 7, 8), final=last, **common)

    return x_all.reshape(nb, seq, d)
```

```python
import functools
import math

import jax
import jax.numpy as jnp
import numpy as np
from jax import lax
from jax.experimental import pallas as pl
from jax.experimental.pallas import tpu as pltpu

F32 = jnp.float32
BF16 = jnp.bfloat16

D_MODEL = 1024
GRID_W = 64
CTX_LEN = 256
EPS = 1e-6
N_MOD = 9
MASK_VALUE = -1e30
TINY = 1e-20
LOG2E = 1.4426950408889634

D_HG = 256
HG_HEADS = 4
HG_DK = 64
D_NA = 256
NA_HEADS = 4
NA_DH = 64
WIN_R = 8
WIN_C = 16
D_SSM = 512
SSM_HEADS = 8
SSM_P = 64
SSM_N = 128
SSM_G = 2
CONV_W = 5
CONV_DIM = D_SSM + 2 * SSM_G * SSM_N
D_FF = 2816
D_IN_PAD = 3712
LANES = 128
SUBLANES = 8
MXU_COLS = 256
MOD_ROWS = 8

GLA_TILE = 256
GLA_SUB = 32
GLA_SAFE_EXPONENT = 60.0
_GLA_GROUPS = GLA_SUB // SUBLANES
GLA_P_ROWS = (GLA_TILE // GLA_SUB) * SUBLANES * SUBLANES * _GLA_GROUPS * (_GLA_GROUPS + 1) // 2
SSD_Q = 128
SSD_STEP = 256
NA_ROWS = 8
OUTPROJ_CHUNKS = 2

VMEM_LIMIT = 56 * 1024 * 1024


def _cparams(sem):
    return pltpu.CompilerParams(dimension_semantics=sem, vmem_limit_bytes=VMEM_LIMIT)


def _silu(x):
    return x * jax.nn.sigmoid(x)


def _split3(x):
    hi = x.astype(BF16)
    r1 = x - hi.astype(F32)
    mid = r1.astype(BF16)
    lo = (r1 - mid.astype(F32)).astype(BF16)
    return hi, mid, lo


def _ldot3(a_bf16, x):
    hi, mid, lo = _split3(x)
    n = x.shape[1]
    if n % MXU_COLS == 0:
        return (jnp.dot(a_bf16, hi, preferred_element_type=F32)
                + jnp.dot(a_bf16, mid, preferred_element_type=F32)
                + jnp.dot(a_bf16, lo, preferred_element_type=F32))
    r = jnp.dot(a_bf16, jnp.concatenate([hi, mid, lo], axis=1), preferred_element_type=F32)
    return r[:, 0:n] + r[:, n:2 * n] + r[:, 2 * n:3 * n]


def _rdot3(x, e3_bf16):
    return jnp.dot(jnp.concatenate(_split3(x), axis=1), e3_bf16, preferred_element_type=F32)


def _dot_nt(a, b):
    return lax.dot_general(a, b, (((1,), (1,)), ((), ())), preferred_element_type=F32)


def _dot_tn(a, b):
    return lax.dot_general(a, b, (((0,), (0,)), ((), ())), preferred_element_type=F32)


def _modulated(x, nw, shift, scale):
    ms = jnp.mean(x * x, axis=-1, keepdims=True)
    y = x * lax.rsqrt(ms + EPS)
    return (y * nw) * (1.0 + scale) + shift


def _cast_kernel(w_ref, o_ref):
    cols = w_ref.shape[1]
    if o_ref.shape[1] == cols:
        o_ref[...] = w_ref[...].astype(BF16)
    else:
        o_ref[:, 0:cols] = w_ref[...].astype(BF16)
        o_ref[:, cols:] = jnp.zeros((o_ref.shape[0], o_ref.shape[1] - cols), BF16)


def _cast_call(w, layer, out_cols=None, tr=256):
    _, rows, cols = w.shape
    out_cols = out_cols or cols
    return pl.pallas_call(
        _cast_kernel,
        grid=(rows // tr,),
        in_specs=[pl.BlockSpec((None, tr, cols), lambda i: (layer, i, 0))],
        out_specs=pl.BlockSpec((tr, out_cols), lambda i: (i, 0)),
        out_shape=jax.ShapeDtypeStruct((rows, out_cols), BF16),
        compiler_params=_cparams(("parallel",)),
        name="cast",
    )(w)


def _mods_kernel(c_ref, w_ref, b_ref, o_ref):
    sc = _silu(c_ref[...]).astype(BF16)
    o_ref[...] = jnp.dot(sc, w_ref[...].astype(BF16), preferred_element_type=F32) + b_ref[...]


def _mods_call(c_rows, w_mod, b_mod):
    depth = w_mod.shape[0]
    tn = 1152
    n = N_MOD * D_MODEL
    return pl.pallas_call(
        _mods_kernel,
        grid=(depth, n // tn),
        in_specs=[
            pl.BlockSpec((MOD_ROWS, D_MODEL), lambda l, j: (0, 0)),
            pl.BlockSpec((None, D_MODEL, tn), lambda l, j: (l, 0, j)),
            pl.BlockSpec((None, 1, tn), lambda l, j: (l, 0, j)),
        ],
        out_specs=pl.BlockSpec((None, MOD_ROWS, tn), lambda l, j: (l, 0, j)),
        out_shape=jax.ShapeDtypeStruct((depth, MOD_ROWS, n), F32),
        compiler_params=_cparams(("parallel", "parallel")),
        name="mods",
    )(c_rows, w_mod, b_mod.reshape(depth, 1, n))


def _mod_spec(tm, seq, nb):
    tpb = seq // tm
    return pl.BlockSpec((None, N_MOD, D_MODEL), lambda i: (jnp.minimum(i // tpb, nb), 0, 0))


def _ffn_kernel(*refs, k_shift, k_scale, k_gate, final, n_lat_tiles):
    if n_lat_tiles is None:
        x_ref, mod_ref, nw_ref, wu_ref, wg_ref, w2_ref, fw_ref, o_ref, act_ref = refs
        x = x_ref[...]
    else:
        x_ref, xc_ref, mod_ref, nw_ref, wu_ref, wg_ref, w2_ref, fw_ref, o_ref, act_ref = refs
        x = jnp.where(pl.program_id(0) >= n_lat_tiles, xc_ref[...], x_ref[...])
    h = _modulated(x, nw_ref[...], mod_ref[k_shift:k_shift + 1, :],
                   mod_ref[k_scale:k_scale + 1, :]).astype(BF16)
    cw = 256
    for c in range(D_FF // cw):
        u = jnp.dot(h, wu_ref[:, c * cw:(c + 1) * cw], preferred_element_type=F32)
        g = jnp.dot(h, wg_ref[:, c * cw:(c + 1) * cw], preferred_element_type=F32)
        act_ref[:, c * cw:(c + 1) * cw] = (_silu(g) * u).astype(BF16)
    y = jnp.dot(act_ref[...], w2_ref[...], preferred_element_type=F32)
    out = x + (0.5 * mod_ref[k_gate:k_gate + 1, :]) * y
    if final:
        ms = jnp.mean(out * out, axis=-1, keepdims=True)
        out = (out * lax.rsqrt(ms + EPS)) * fw_ref[...]
    o_ref[...] = out


def _ffn_call(x_all, mods_l, nw, w13b, w2b, fw, *, n_rows, seq, nb, ks, final, x_ctx=None, tm=512):
    if x_ctx is None:
        n_lat_tiles = None
        x_args = (x_all,)
        x_specs = [pl.BlockSpec((tm, D_MODEL), lambda i: (i, 0))]
    else:
        n_lat_tiles = nb * seq // tm
        x_args = (x_all, x_ctx)
        x_specs = [pl.BlockSpec((tm, D_MODEL), lambda i: (jnp.minimum(i, n_lat_tiles - 1), 0)),
                   pl.BlockSpec((tm, D_MODEL), lambda i: (jnp.maximum(i - n_lat_tiles, 0), 0))]
    kern = functools.partial(_ffn_kernel, k_shift=ks[0], k_scale=ks[1], k_gate=ks[2], final=final,
                             n_lat_tiles=n_lat_tiles)
    return pl.pallas_call(
        kern,
        grid=(n_rows // tm,),
        in_specs=x_specs + [
            _mod_spec(tm, seq, nb),
            pl.BlockSpec((1, D_MODEL), lambda i: (0, 0)),
            pl.BlockSpec((D_MODEL, D_FF), lambda i: (0, 0), pipeline_mode=pl.Buffered(1)),
            pl.BlockSpec((D_MODEL, D_FF), lambda i: (0, 1), pipeline_mode=pl.Buffered(1)),
            pl.BlockSpec((D_FF, D_MODEL), lambda i: (0, 0), pipeline_mode=pl.Buffered(1)),
            pl.BlockSpec((1, D_MODEL), lambda i: (0, 0)),
        ],
        out_specs=pl.BlockSpec((tm, D_MODEL), lambda i: (i, 0)),
        out_shape=jax.ShapeDtypeStruct((n_rows, D_MODEL), F32),
        scratch_shapes=[pltpu.VMEM((tm, D_FF), BF16)],
        compiler_params=_cparams(("parallel",)),
        name="ffn",
    )(*x_args, mods_l, nw, w13b, w13b, w2b, fw)


_IN_SPLITS = ((0, 1280), (1280, 2048), (2048, 2560), (2560, 3584), (3584, 3712))


_XBC_COLS = _IN_SPLITS[3]
HALO = 8


def _inproj_kernel(x_ref, xp_ref, xn_ref, mod_ref, nw_ref, w_ref, cw_ref, cb_ref,
                   hg_ref, na_ref, z_ref, xbc_ref, dt_ref, ext_ref, *, n_lat_tiles, tpb):
    tm = x_ref.shape[0]
    i = pl.program_id(0)
    is_ctx = i >= n_lat_tiles
    first = jnp.logical_or(is_ctx, (i % tpb) == 0)
    last = jnp.logical_or(is_ctx, (i % tpb) == tpb - 1)
    nw = nw_ref[...]
    shift = mod_ref[3:4, :]
    scale = mod_ref[4:5, :]
    h = _modulated(x_ref[...], nw, shift, scale).astype(BF16)
    x_halo = jnp.concatenate([xp_ref[...], xn_ref[...]], axis=0)
    h_halo = _modulated(x_halo, nw, shift, scale).astype(BF16)

    a, b = _XBC_COLS
    wx = w_ref[:, a:b]
    halo = jnp.dot(h_halo, wx, preferred_element_type=F32)
    ext_ref[0:HALO, :] = jnp.where(first, 0.0, halo[0:HALO, :])
    ext_ref[HALO:HALO + tm, :] = jnp.dot(h, wx, preferred_element_type=F32)
    ext_ref[HALO + tm:2 * HALO + tm, :] = jnp.where(last, 0.0, halo[HALO:2 * HALO, :])
    pos = lax.broadcasted_iota(jnp.int32, (tm, 1), 0) % CTX_LEN
    acc = jnp.zeros((tm, CONV_DIM), F32) + cb_ref[...]
    for j in range(CONV_W):
        d = j - CONV_W // 2
        tap = ext_ref[HALO + d:HALO + d + tm, :] * cw_ref[j:j + 1, :]
        if d != 0:
            crosses = jnp.logical_and(is_ctx, jnp.logical_or(pos + d < 0, pos + d >= CTX_LEN))
            tap = jnp.where(crosses, 0.0, tap)
        acc = acc + tap
    xbc_ref[...] = _silu(acc)

    for (a, b), o_ref in zip(_IN_SPLITS, (hg_ref, na_ref, z_ref, None, dt_ref)):
        if o_ref is not None:
            o_ref[...] = jnp.dot(h, w_ref[:, a:b], preferred_element_type=F32).astype(o_ref.dtype)


def _inproj_call(x_all, mods_l, nw, w_in_b, conv_w, conv_b, *, n_rows, seq, nb, tm=512):
    widths = [b - a for a, b in _IN_SPLITS]
    dtypes = [F32, BF16, F32, F32, F32]
    r8 = tm // HALO
    last8 = n_rows // HALO - 1
    kern = functools.partial(_inproj_kernel, n_lat_tiles=nb * seq // tm, tpb=seq // tm)
    return pl.pallas_call(
        kern,
        grid=(n_rows // tm,),
        in_specs=[
            pl.BlockSpec((tm, D_MODEL), lambda i: (i, 0)),
            pl.BlockSpec((HALO, D_MODEL), lambda i: (jnp.maximum(i * r8 - 1, 0), 0)),
            pl.BlockSpec((HALO, D_MODEL), lambda i: (jnp.minimum((i + 1) * r8, last8), 0)),
            _mod_spec(tm, seq, nb),
            pl.BlockSpec((1, D_MODEL), lambda i: (0, 0)),
            pl.BlockSpec((D_MODEL, D_IN_PAD), lambda i: (0, 0), pipeline_mode=pl.Buffered(1)),
            pl.BlockSpec((CONV_W, CONV_DIM), lambda i: (0, 0)),
            pl.BlockSpec((1, CONV_DIM), lambda i: (0, 0)),
        ],
        out_specs=[pl.BlockSpec((tm, w), lambda i: (i, 0)) for w in widths],
        out_shape=[jax.ShapeDtypeStruct((n_rows, w), dt) for w, dt in zip(widths, dtypes)],
        scratch_shapes=[pltpu.VMEM((tm + 2 * HALO, CONV_DIM), F32)],
        compiler_params=_cparams(("parallel",)),
        name="inproj",
    )(x_all, x_all, x_all, mods_l, nw, w_in_b, conv_w, conv_b)


def _scan_block_map(*, seq, nb, q, rev):
    nc_ctx = CTX_LEN // q
    nc_lat = seq // q
    ctx_base = nb * seq // q

    def row_block(b, j):
        cj = j
        lj = j - nc_ctx
        if rev:
            cj = nc_ctx - 1 - cj
            lj = nc_lat - 1 - lj
        return jnp.where(j < nc_ctx, ctx_base + b * nc_ctx + cj, b * nc_lat + lj)

    return row_block, nc_ctx + nc_lat


def _round_robin(stage_generators):
    live = list(stage_generators)
    while live:
        for gen in list(live):
            try:
                next(gen)
            except StopIteration:
                live.remove(gen)


def _ssd_direction(xbc_ref, dt_ref, arow_ref, brow_ref, e_ref, tri_ref, y_ref, st_ref, *, rev, lane0):
    q = SSD_Q
    nch = SSD_STEP // q
    gw = D_SSM // SSM_G
    t_idx = lax.broadcasted_iota(jnp.int32, (q, q), 0)
    s_idx = lax.broadcasted_iota(jnp.int32, (q, q), 1)
    causal = (s_idx >= t_idx) if rev else (s_idx <= t_idx)
    lane = lax.broadcasted_iota(jnp.int32, (q, LANES), 1)
    low_half = lane < SSM_P
    e = e_ref[...]
    tri = tri_ref[...]
    edge = 0 if rev else q - 1
    chunks = [dict(rows=slice(ci * q, (ci + 1) * q))
              for ci in (range(nch - 1, -1, -1) if rev else range(nch))]

    def stage_cumsum(ch):
        rows = ch["rows"]
        dt = jax.nn.softplus(dt_ref[rows, :] + brow_ref[...])
        ch["dt"] = dt
        ch["cum"] = _ldot3(tri, dt * arow_ref[...])

    def stage_expand(ch):
        rows = ch["rows"]
        cum = ch["cum"]
        ch["cum_t"] = cum.T
        cum_e = _rdot3(cum, e)
        ch["cum_e"] = cum_e
        ch["tot_e"] = cum_e[edge:edge + 1, :]
        ch["xdt"] = xbc_ref[rows, 0:D_SSM] * _rdot3(ch["dt"], e)
        ch["bg"] = []
        ch["cg"] = []
        ch["cb"] = []
        for g in range(SSM_G):
            b0 = D_SSM + g * SSM_N
            c0 = D_SSM + SSM_G * SSM_N + g * SSM_N
            bg = xbc_ref[rows, b0:b0 + SSM_N].astype(BF16)
            cg = xbc_ref[rows, c0:c0 + SSM_N].astype(BF16)
            ch["bg"].append(bg)
            ch["cg"].append(cg)
            ch["cb"].append(_dot_nt(cg, bg))

    def stage_local(ch):
        cum, cum_t, cum_e, tot_e, xdt = ch["cum"], ch["cum_t"], ch["cum_e"], ch["tot_e"], ch["xdt"]
        ch["y_diag"] = []
        ch["upd"] = []
        for g in range(SSM_G):
            gl = g * gw
            for hp in range(2):
                pl0 = gl + hp * LANES
                xpair = xdt[:, pl0:pl0 + LANES]
                acc = jnp.zeros((q, LANES), F32)
                for hh in range(2):
                    idx = lane0 + g * 4 + hp * 2 + hh
                    diff = cum[:, idx:idx + 1] - cum_t[idx:idx + 1, :]
                    m = ch["cb"][g] * jnp.exp(jnp.where(causal, diff, MASK_VALUE))
                    xm = jnp.where(low_half if hh == 0 else jnp.logical_not(low_half), xpair, 0.0)
                    acc = acc + jnp.dot(m.astype(BF16), xm.astype(BF16), preferred_element_type=F32)
                ch["y_diag"].append(acc)
            xw = (xdt[:, gl:gl + gw] * jnp.exp(tot_e[:, gl:gl + gw] - cum_e[:, gl:gl + gw])).astype(BF16)
            ch["upd"].append(_dot_tn(ch["bg"][g], xw))
        ch["off_scale"] = jnp.exp(cum_e)
        ch["dec"] = jnp.exp(tot_e)

    def stage_state():
        st = [st_ref[:, g * gw:(g + 1) * gw] for g in range(SSM_G)]
        for ch in chunks:
            y_parts = []
            for g in range(SSM_G):
                gl = g * gw
                y_off = jnp.dot(ch["cg"][g], st[g].astype(BF16), preferred_element_type=F32)
                y_parts.append(jnp.concatenate(ch["y_diag"][2 * g:2 * g + 2], axis=1)
                               + y_off * ch["off_scale"][:, gl:gl + gw])
                st[g] = st[g] * ch["dec"][:, gl:gl + gw] + ch["upd"][g]
            y_ref[ch["rows"], :] = jnp.concatenate(y_parts, axis=1)
        for g in range(SSM_G):
            st_ref[:, g * gw:(g + 1) * gw] = st[g]

    for ch in chunks:
        yield stage_cumsum(ch)
    for ch in chunks:
        yield stage_expand(ch)
    for ch in chunks:
        yield stage_local(ch)
    yield stage_state()


def _ssd_kernel(xf_ref, xb_ref, dtf_ref, dtb_ref, arow_ref, brow_ref, ef_ref, eb_ref, trif_ref, trib_ref,
                yf_ref, yb_ref, stf_ref, stb_ref):
    @pl.when(pl.program_id(1) == 0)
    def _():
        stf_ref[...] = jnp.zeros_like(stf_ref)
        stb_ref[...] = jnp.zeros_like(stb_ref)

    _round_robin([
        _ssd_direction(xf_ref, dtf_ref, arow_ref, brow_ref, ef_ref, trif_ref, yf_ref, stf_ref,
                       rev=False, lane0=0),
        _ssd_direction(xb_ref, dtb_ref, arow_ref, brow_ref, eb_ref, trib_ref, yb_ref, stb_ref,
                       rev=True, lane0=SSM_HEADS)])


def _ssd_call(xbc_c, dt_raw, a_row, b_row, e_f, e_b, tri_f, tri_b, *, n_rows, seq, nb):
    q = SSD_STEP
    blk_f, steps = _scan_block_map(seq=seq, nb=nb, q=q, rev=False)
    blk_b, _ = _scan_block_map(seq=seq, nb=nb, q=q, rev=True)
    const = lambda b, j: (0, 0)
    return pl.pallas_call(
        _ssd_kernel,
        grid=(nb, steps),
        in_specs=[
            pl.BlockSpec((q, CONV_DIM), lambda b, j: (blk_f(b, j), 0)),
            pl.BlockSpec((q, CONV_DIM), lambda b, j: (blk_b(b, j), 0)),
            pl.BlockSpec((q, LANES), lambda b, j: (blk_f(b, j), 0)),
            pl.BlockSpec((q, LANES), lambda b, j: (blk_b(b, j), 0)),
            pl.BlockSpec((1, LANES), const),
            pl.BlockSpec((1, LANES), const),
            pl.BlockSpec((3 * LANES, D_SSM), const),
            pl.BlockSpec((3 * LANES, D_SSM), const),
            pl.BlockSpec((SSD_Q, SSD_Q), const),
            pl.BlockSpec((SSD_Q, SSD_Q), const),
        ],
        out_specs=[pl.BlockSpec((q, D_SSM), lambda b, j: (blk_f(b, j), 0)),
                   pl.BlockSpec((q, D_SSM), lambda b, j: (blk_b(b, j), 0))],
        out_shape=[jax.ShapeDtypeStruct((n_rows, D_SSM), F32)] * 2,
        scratch_shapes=[pltpu.VMEM((SSM_N, D_SSM), F32)] * 2,
        compiler_params=_cparams(("arbitrary", "arbitrary")),
        name="ssd",
    )(xbc_c, xbc_c, dt_raw, dt_raw, a_row, b_row, e_f, e_b, tri_f, tri_b)


def _gla_direction(q_ref, f_ref, v_ref, lb_ref, tri_ref, same_blk, ind_ref, o_ref, p_s, upd_s, oi_s, qkb_s,
                   st_ref, *, rev):
    t = GLA_TILE
    c = GLA_SUB
    nsub = t // c
    lb = lb_ref[...]
    fr = f_ref[...]
    f = lb + (1.0 - lb) * jax.nn.sigmoid(fr)
    logf = jnp.log(jnp.maximum(f, TINY))
    k = (1.0 - lb) * jax.nn.sigmoid(-fr)
    qv = _silu(q_ref[...])
    vv = v_ref[...]
    yield
    brel = _ldot3(tri_ref[...], logf)
    tot = _ldot3(same_blk, logf)
    qkb_s[0] = qv
    qkb_s[1] = k
    qkb_s[2] = brel
    yield

    qt = (qv * jnp.exp(brel)).astype(BF16)
    kt = k * jnp.exp(tot - brel)
    dec_blk = jnp.exp(tot)
    lane = lax.broadcasted_iota(jnp.int32, (t, D_HG), 1)
    head_masks = [(lane >= h * HG_DK) & (lane < (h + 1) * HG_DK) for h in range(HG_HEADS)]
    kxs = [jnp.where(m, kt, 0.0).astype(BF16) for m in head_masks]
    vxs = [jnp.where(m, vv, 0.0).astype(BF16) for m in head_masks]
    for blk in range(nsub):
        r0 = blk * c
        kx = jnp.concatenate([a[r0:r0 + c, :] for a in kxs], axis=0)
        vx = jnp.concatenate([a[r0:r0 + c, :] for a in vxs], axis=0)
        upd_s[blk] = _dot_tn(vx, kx)
        if blk % 4 == 3:
            yield

    mid = c // 2
    bref = jnp.concatenate([jnp.broadcast_to(brel[b * c + mid:b * c + mid + 1, :], (c, D_HG))
                            for b in range(nsub)], axis=0)
    dev = brel - bref
    worst = jnp.max(jnp.max(jnp.abs(dev), axis=1, keepdims=True), axis=0, keepdims=True)
    safe_v = worst <= GLA_SAFE_EXPONENT
    safe = worst[0, 0] <= GLA_SAFE_EXPONENT
    qh = qv * jnp.exp(dev)
    kh = (k * jnp.exp(-dev)).astype(BF16)
    qx = jnp.concatenate([jnp.where(m, qh, 0.0).astype(BF16) for m in head_masks], axis=0)
    yield
    sc = _dot_nt(qx, kh)
    visible = tri_ref[...].astype(F32) > 0.5
    pm = jnp.concatenate([jnp.where(visible, sc[h * t:(h + 1) * t, :], 0.0).astype(BF16)
                          for h in range(HG_HEADS)], axis=0)
    yield
    oh = jnp.dot(pm, vv.astype(BF16), preferred_element_type=F32)
    o_fast = jnp.where(head_masks[0], oh[0:t, :], 0.0)
    for h in range(1, HG_HEADS):
        o_fast = o_fast + jnp.where(head_masks[h], oh[h * t:(h + 1) * t, :], 0.0)
    oi_s[...] = jnp.where(safe_v, o_fast, 0.0)
    yield

    @pl.when(jnp.logical_not(safe))
    def _():
        _gla_intra_pairwise(qkb_s, v_ref, ind_ref, p_s, oi_s, rev=rev)

    yield

    st = st_ref[...]
    o_inter = [None] * nsub
    for blk in (range(nsub - 1, -1, -1) if rev else range(nsub)):
        r0 = blk * c
        o_inter[blk] = _dot_nt(qt[r0:r0 + c, :], st.astype(BF16))
        st = st * dec_blk[r0:r0 + 1, :] + upd_s[blk]
        yield
    st_ref[...] = st
    o_ref[...] = oi_s[...] + jnp.concatenate(o_inter, axis=0)


def _gla_intra_pairwise(qkb_s, v_ref, ind_ref, p_s, oi_s, *, rev):
    t = GLA_TILE
    c = GLA_SUB
    nsub = t // c
    sub = SUBLANES
    qv, k, brel = qkb_s[0], qkb_s[1], qkb_s[2]
    vv = v_ref[...]
    t_idx = lax.broadcasted_iota(jnp.int32, (sub, D_HG), 0)
    pieces = [(s, g) for s in range(c) for g in range(c // sub)
              if (g <= s // sub if rev else g >= s // sub)]
    rows_per_blk = len(pieces) * sub

    brel2 = brel * LOG2E
    for blk in range(nsub):
        r0 = blk * c
        qb = qv[r0:r0 + c, :]
        kb = k[r0:r0 + c, :]
        bb = brel2[r0:r0 + c, :]
        for u in range(0, len(pieces), 2):
            rows = []
            for s, g in pieces[u:u + 2]:
                tt = t_idx + g * sub
                keep = (tt <= s) if rev else (tt >= s)
                dec = jnp.exp2(jnp.where(keep, bb[g * sub:(g + 1) * sub, :] - bb[s:s + 1, :], MASK_VALUE))
                rows.append(qb[g * sub:(g + 1) * sub, :] * kb[s:s + 1, :] * dec)
            po = blk * rows_per_blk + u * sub
            p_s[po:po + 2 * sub, :] = jnp.concatenate(rows, axis=0).astype(BF16)
    r = jnp.dot(p_s[...], ind_ref[...], preferred_element_type=F32)
    for blk in range(nsub):
        r0 = blk * c
        vb = vv[r0:r0 + c, :]
        accs = [jnp.zeros((sub, D_HG), F32) for _ in range(c // sub)]
        for n, (s, g) in enumerate(pieces):
            po = blk * rows_per_blk + n * sub
            accs[g] = accs[g] + r[po:po + sub, :] * vb[s:s + 1, :]
        oi_s[r0:r0 + c, :] = jnp.concatenate(accs, axis=0)


def _gla_kernel(qf_ref, ff_ref, vf_ref, qb_ref, fb_ref, vb_ref, lbf_ref, lbb_ref, trif_ref, trib_ref,
                ind_ref, of_ref, ob_ref, pf_s, pb_s, updf_s, updb_s, oif_s, oib_s, qkbf_s, qkbb_s,
                stf_ref, stb_ref):
    @pl.when(pl.program_id(1) == 0)
    def _():
        stf_ref[...] = jnp.zeros_like(stf_ref)
        stb_ref[...] = jnp.zeros_like(stb_ref)

    same_blk = jnp.maximum(trif_ref[...], trib_ref[...])
    _round_robin([
        _gla_direction(qf_ref, ff_ref, vf_ref, lbf_ref, trif_ref, same_blk, ind_ref, of_ref,
                       pf_s, updf_s, oif_s, qkbf_s, stf_ref, rev=False),
        _gla_direction(qb_ref, fb_ref, vb_ref, lbb_ref, trib_ref, same_blk, ind_ref, ob_ref,
                       pb_s, updb_s, oib_s, qkbb_s, stb_ref, rev=True)])


def _gla_call(hg, lb_f, lb_b, tri_f, tri_b, ind, *, n_rows, seq, nb):
    t = GLA_TILE
    blk_f, steps = _scan_block_map(seq=seq, nb=nb, q=t, rev=False)
    blk_b, _ = _scan_block_map(seq=seq, nb=nb, q=t, rev=True)
    const = lambda b, j: (0, 0)
    col = lambda blk, cidx: pl.BlockSpec((t, D_HG), lambda b, j: (blk(b, j), cidx))
    return pl.pallas_call(
        _gla_kernel,
        grid=(nb, steps),
        in_specs=[
            col(blk_f, 0), col(blk_f, 1), col(blk_f, 3),
            col(blk_b, 0), col(blk_b, 2), col(blk_b, 3),
            pl.BlockSpec((1, D_HG), const),
            pl.BlockSpec((1, D_HG), const),
            pl.BlockSpec((t, t), const),
            pl.BlockSpec((t, t), const),
            pl.BlockSpec((D_HG, D_HG), const),
        ],
        out_specs=[col(blk_f, 0), col(blk_b, 0)],
        out_shape=[jax.ShapeDtypeStruct((n_rows, D_HG), F32)] * 2,
        scratch_shapes=(
            [pltpu.VMEM((GLA_P_ROWS, D_HG), BF16)] * 2
            + [pltpu.VMEM((GLA_TILE // GLA_SUB, D_HG, D_HG), F32)] * 2
            + [pltpu.VMEM((GLA_TILE, D_HG), F32)] * 2
            + [pltpu.VMEM((3, GLA_TILE, D_HG), F32)] * 2
            + [pltpu.VMEM((D_HG, D_HG), F32)] * 2),
        compiler_params=_cparams(("arbitrary", "arbitrary")),
        name="gla",
    )(hg, hg, hg, hg, hg, hg, lb_f, lb_b, tri_f, tri_b, ind)


def _expand_heads(x, masks):
    return jnp.concatenate([jnp.where(m, x, jnp.zeros_like(x)) for m in masks], axis=0)


def _collapse_heads(o, masks, t):
    acc = jnp.where(masks[0], o[0:t, :], 0.0)
    for h in range(1, NA_HEADS):
        acc = acc + jnp.where(masks[h], o[h * t:(h + 1) * t, :], 0.0)
    return acc


def _na_finish(o, nw):
    ms = jnp.mean(o * o, axis=-1, keepdims=True)
    return (o * lax.rsqrt(ms + EPS)) * nw


def _na_kernel(q_ref, k_ref, v_ref, kc_ref, vc_ref, bias_ref, nw_ref, o_ref, *, n_grid_rows):
    w = GRID_W
    i = pl.program_id(1)
    lane = lax.broadcasted_iota(jnp.int32, (w, D_NA), 1)
    masks = [(lane >= h * NA_DH) & (lane < (h + 1) * NA_DH) for h in range(NA_HEADS)]
    kc = kc_ref[...]
    vc = vc_ref[...]
    nw = nw_ref[...]
    scale = NA_DH ** -0.5

    def body(jr, carry):
        r = i * NA_ROWS + jr
        r0 = jnp.clip(r - WIN_R // 2, 0, n_grid_rows - WIN_R)
        var = r - r0
        k0 = pl.multiple_of(r0 * w, w)
        q0 = pl.multiple_of(jr * w, w)
        qx = _expand_heads(q_ref[pl.ds(q0, w), :] * scale, masks)
        kw = k_ref[pl.ds(k0, WIN_R * w), :]
        vw = v_ref[pl.ds(k0, WIN_R * w), :]
        s_loc = _dot_nt(qx, kw) + bias_ref[var]
        s_ctx = _dot_nt(qx, kc)
        m = jnp.maximum(jnp.max(s_loc, axis=-1, keepdims=True), jnp.max(s_ctx, axis=-1, keepdims=True))
        p_loc = jnp.exp(s_loc - m)
        p_ctx = jnp.exp(s_ctx - m)
        den = jnp.sum(p_loc, axis=-1, keepdims=True) + jnp.sum(p_ctx, axis=-1, keepdims=True)
        o = (jnp.dot(p_loc.astype(BF16), vw, preferred_element_type=F32)
             + jnp.dot(p_ctx.astype(BF16), vc, preferred_element_type=F32)) / den
        o_ref[pl.ds(q0, w), :] = _na_finish(_collapse_heads(o, masks, w), nw)
        return carry

    lax.fori_loop(0, NA_ROWS, body, 0, unroll=True)


def _na_call(na, bias, nw, *, seq, nb):
    rows = seq // GRID_W
    tq = NA_ROWS * GRID_W
    qpb = seq // tq
    ctx_blk = nb * seq // CTX_LEN
    kern = functools.partial(_na_kernel, n_grid_rows=rows)
    return pl.pallas_call(
        kern,
        grid=(nb, qpb),
        in_specs=[
            pl.BlockSpec((tq, D_NA), lambda b, i: (b * qpb + i, 0)),
            pl.BlockSpec((seq, D_NA), lambda b, i: (b, 1)),
            pl.BlockSpec((seq, D_NA), lambda b, i: (b, 2)),
            pl.BlockSpec((CTX_LEN, D_NA), lambda b, i: (ctx_blk + b, 1)),
            pl.BlockSpec((CTX_LEN, D_NA), lambda b, i: (ctx_blk + b, 2)),
            pl.BlockSpec((WIN_R, NA_HEADS * GRID_W, WIN_R * GRID_W), lambda b, i: (0, 0, 0)),
            pl.BlockSpec((1, D_NA), lambda b, i: (0, 0)),
        ],
        out_specs=pl.BlockSpec((tq, D_NA), lambda b, i: (b * qpb + i, 0)),
        out_shape=jax.ShapeDtypeStruct((nb * seq, D_NA), F32),
        compiler_params=_cparams(("parallel", "arbitrary")),
        name="natten",
    )(na, na, na, na, na, bias, nw)


def _ctxattn_kernel(q_ref, k_ref, v_ref, nw_ref, o_ref):
    t = CTX_LEN
    lane = lax.broadcasted_iota(jnp.int32, (t, D_NA), 1)
    masks = [(lane >= h * NA_DH) & (lane < (h + 1) * NA_DH) for h in range(NA_HEADS)]
    qx = _expand_heads(q_ref[...] * (NA_DH ** -0.5), masks)
    s = _dot_nt(qx, k_ref[...])
    m = jnp.max(s, axis=-1, keepdims=True)
    p = jnp.exp(s - m)
    den = jnp.sum(p, axis=-1, keepdims=True)
    o = jnp.dot(p.astype(BF16), v_ref[...], preferred_element_type=F32) / den
    o_ref[...] = _na_finish(_collapse_heads(o, masks, t), nw_ref[...])


def _ctxattn_call(na, nw, *, seq, nb):
    ctx_blk = nb * seq // CTX_LEN
    return pl.pallas_call(
        _ctxattn_kernel,
        grid=(nb,),
        in_specs=[
            pl.BlockSpec((CTX_LEN, D_NA), lambda b: (ctx_blk + b, 0)),
            pl.BlockSpec((CTX_LEN, D_NA), lambda b: (ctx_blk + b, 1)),
            pl.BlockSpec((CTX_LEN, D_NA), lambda b: (ctx_blk + b, 2)),
            pl.BlockSpec((1, D_NA), lambda b: (0, 0)),
        ],
        out_specs=pl.BlockSpec((CTX_LEN, D_NA), lambda b: (b, 0)),
        out_shape=jax.ShapeDtypeStruct((nb * CTX_LEN, D_NA), F32),
        compiler_params=_cparams(("parallel",)),
        name="ctxattn",
    )(na, na, na, nw)


def _outproj_kernel(x_ref, mod_ref, of_ref, ob_ref, g_ref, na_ref, yf_ref, yb_ref, xs_ref, z_ref,
                    hgw_ref, hm_ref, dsk_ref, sw_ref, wo_ref, o_ref):
    hm = hm_ref[...]
    tm = x_ref.shape[0]
    rc = tm // OUTPROJ_CHUNKS
    for n in range(OUTPROJ_CHUNKS):
        rows = slice(n * rc, (n + 1) * rc)
        o = of_ref[rows, :] + ob_ref[rows, :]
        sq = o * o
        hi = sq.astype(BF16)
        lo = (sq - hi.astype(F32)).astype(BF16)
        ms = (jnp.dot(hi, hm, preferred_element_type=F32)
              + jnp.dot(lo, hm, preferred_element_type=F32)) * (1.0 / HG_DK)
        hg = (o * lax.rsqrt(ms + EPS)) * hgw_ref[...] * _silu(g_ref[rows, :])
        ys = (yf_ref[rows, :] + yb_ref[rows, :] + dsk_ref[...] * xs_ref[rows, :]) * _silu(z_ref[rows, :])
        ms2 = jnp.mean(ys * ys, axis=-1, keepdims=True)
        ssm = (ys * lax.rsqrt(ms2 + EPS)) * sw_ref[...]
        mix = (jnp.dot(hg.astype(BF16), wo_ref[0:D_HG, :], preferred_element_type=F32)
               + jnp.dot(na_ref[rows, :].astype(BF16), wo_ref[D_HG:D_HG + D_NA, :], preferred_element_type=F32)
               + jnp.dot(ssm.astype(BF16), wo_ref[D_HG + D_NA:, :], preferred_element_type=F32))
        o_ref[rows, :] = x_ref[rows, :] + mod_ref[5:6, :] * mix


def _outproj_call(x_all, mods_l, o_f, o_b, hg, y_na, y_f, y_b, xbc_c, z, hgw, hm, dsk, sw, wo_b,
                  *, n_rows, seq, nb, tm=512):
    row = lambda i: (i, 0)
    const = lambda i: (0, 0)
    return pl.pallas_call(
        _outproj_kernel,
        grid=(n_rows // tm,),
        in_specs=[
            pl.BlockSpec((tm, D_MODEL), row),
            _mod_spec(tm, seq, nb),
            pl.BlockSpec((tm, D_HG), row),
            pl.BlockSpec((tm, D_HG), row),
            pl.BlockSpec((tm, D_HG), lambda i: (i, 4)),
            pl.BlockSpec((tm, D_NA), row),
            pl.BlockSpec((tm, D_SSM), row),
            pl.BlockSpec((tm, D_SSM), row),
            pl.BlockSpec((tm, D_SSM), row),
            pl.BlockSpec((tm, D_SSM), row),
            pl.BlockSpec((1, D_HG), const),
            pl.BlockSpec((D_HG, D_HG), const),
            pl.BlockSpec((1, D_SSM), const),
            pl.BlockSpec((1, D_SSM), const),
            pl.BlockSpec((D_MODEL, D_MODEL), const),
        ],
        out_specs=pl.BlockSpec((tm, D_MODEL), row),
        out_shape=jax.ShapeDtypeStruct((n_rows, D_MODEL), F32),
        compiler_params=_cparams(("parallel",)),
        name="outproj",
    )(x_all, mods_l, o_f, o_b, hg, y_na, y_f, y_b, xbc_c, z, hgw, hm, dsk, sw, wo_b)


def _na_bias_table(rpb):
    w = GRID_W
    ndr = 2 * WIN_R - 1
    ndc = 2 * WIN_C - 1
    col = np.arange(w)
    c0 = np.clip(col - WIN_C // 2, 0, w - WIN_C)
    col_in = (col[None, :] >= c0[:, None]) & (col[None, :] < c0[:, None] + WIN_C)
    dc = np.clip(col[None, :] - col[:, None], -(WIN_C - 1), WIN_C - 1) + (WIN_C - 1)
    onehot = (dc.reshape(1, -1) == np.arange(ndc)[:, None]).astype(np.float32)
    t = jnp.dot(rpb.reshape(NA_HEADS * ndr, ndc).astype(F32), onehot,
                precision=lax.Precision.HIGHEST).reshape(NA_HEADS, ndr, w, w)
    t = jnp.where(col_in[None, None], t, MASK_VALUE)
    b = jnp.stack([t[:, WIN_R - 1 - var:2 * WIN_R - 1 - var] for var in range(WIN_R)], axis=0)
    b = jnp.transpose(b, (0, 1, 3, 2, 4))
    return b.reshape(WIN_R, NA_HEADS * w, WIN_R * w)


def _block_tri(n, c, rev):
    r = jnp.arange(n)
    same = (r[:, None] // c) == (r[None, :] // c)
    tri = (r[None, :] >= r[:, None]) if rev else (r[None, :] <= r[:, None])
    return (same & tri).astype(BF16)


def _head_block_ones(n, hd):
    r = jnp.arange(n)
    return ((r[:, None] // hd) == (r[None, :] // hd)).astype(BF16)


def _ssd_expand(lane0):
    r = jnp.arange(LANES)[:, None]
    cidx = jnp.arange(D_SSM)[None, :]
    e = (r == lane0 + cidx // SSM_P).astype(BF16)
    return jnp.concatenate([e, e, e], axis=0)


def _lane_row(vals):
    return jnp.zeros((1, LANES), F32).at[0, :vals.shape[0]].set(vals.astype(F32))


def kernel(x, c, ctx, c_ctx, w_mod, b_mod, norm_ffn1, ffn1_w13, ffn1_w2, norm_mix, w_in,
           hg_lower_bounds, hg_norm, na_rpb, na_norm, ssm_conv_w, ssm_conv_b, ssm_a_log,
           ssm_dt_bias, ssm_d, ssm_norm, w_out, norm_ffn2, ffn2_w13, ffn2_w2, final_norm):
    nb, seq, d = x.shape
    depth = w_mod.shape[0]
    assert d == D_MODEL and ctx.shape[1] == CTX_LEN and nb + 1 <= MOD_ROWS
    assert seq % 512 == 0 and seq // GRID_W >= WIN_R
    n_lat = nb * seq
    n_all = n_lat + nb * CTX_LEN

    lb_soft = jax.nn.softmax(hg_lower_bounds.astype(F32), axis=1)
    lower_bounds = jnp.cumsum(lb_soft, axis=1) - lb_soft[:, :1]

    c_rows = jnp.zeros((MOD_ROWS, d), F32).at[:nb].set(c).at[nb].set(c_ctx)
    mods = _mods_call(c_rows, w_mod, b_mod).reshape(depth, MOD_ROWS, N_MOD, d)

    x_all = x.reshape(n_lat, d)
    x_ctx = ctx.reshape(nb * CTX_LEN, d)

    tri_f = _block_tri(GLA_TILE, GLA_SUB, False)
    tri_b = _block_tri(GLA_TILE, GLA_SUB, True)
    ind = _head_block_ones(D_HG, HG_DK)
    ssd_tri_f = _block_tri(SSD_Q, SSD_Q, False)
    ssd_tri_b = _block_tri(SSD_Q, SSD_Q, True)
    e_f = _ssd_expand(0)
    e_b = _ssd_expand(SSM_HEADS)
    one_row = lambda v: v.reshape(1, -1).astype(F32)
    common = dict(seq=seq, nb=nb)

    for layer in range(depth):
        last = layer == depth - 1
        mods_l = mods[layer]
        w13_1 = _cast_call(ffn1_w13, layer)
        w2_1 = _cast_call(ffn1_w2, layer)
        w13_2 = _cast_call(ffn2_w13, layer)
        w2_2 = _cast_call(ffn2_w2, layer)
        w_in_b = _cast_call(w_in, layer, out_cols=D_IN_PAD)
        wo_b = _cast_call(w_out, layer)
        fw = one_row(final_norm)

        x_all = _ffn_call(x_all, mods_l, one_row(norm_ffn1[layer]), w13_1, w2_1, fw,
                          n_rows=n_all, ks=(0, 1, 2), final=False, x_ctx=x_ctx if layer == 0 else None,
                          **common)
        hg, na, z, xbc_c, dt_raw = _inproj_call(x_all, mods_l, one_row(norm_mix[layer]), w_in_b,
                                                ssm_conv_w[layer].astype(F32), one_row(ssm_conv_b[layer]),
                                                n_rows=n_all, **common)

        o_f, o_b = _gla_call(hg, one_row(lower_bounds[0, layer]), one_row(lower_bounds[1, layer]),
                             tri_f, tri_b, ind, n_rows=n_all, **common)

        bias = _na_bias_table(na_rpb[layer])
        nw = one_row(na_norm[layer])
        y_na = _na_call(na, bias, nw, **common)
        if not last:
            y_na = jnp.concatenate([y_na, _ctxattn_call(na, nw, **common)], axis=0)

        a_neg = -jnp.exp(ssm_a_log[layer].astype(F32))
        y_f, y_b = _ssd_call(xbc_c, dt_raw, _lane_row(a_neg.reshape(-1)),
                             _lane_row(ssm_dt_bias[layer].reshape(-1)),
                             e_f, e_b, ssd_tri_f, ssd_tri_b, n_rows=n_all, **common)

        n_out = n_lat if last else n_all
        dsk = jnp.repeat(ssm_d[layer].astype(F32), SSM_P).reshape(1, D_SSM)
        x_all = _outproj_call(x_all, mods_l, o_f, o_b, hg, y_na, y_f, y_b, xbc_c, z,
                              one_row(hg_norm[layer]), ind, dsk, one_row(ssm_norm[layer]), wo_b,
                              n_rows=n_out, **common)
        x_all = _ffn_call(x_all, mods_l, one_row(norm_ffn2[layer]), w13_2, w2_2, fw,
                          n_rows=n_out, ks=(6, 7, 8), final=last, **common)

    return x_all.reshape(nb, seq, d)
```

```python
import functools
import math

import jax
import jax.numpy as jnp
import numpy as np
from jax import lax
from jax.experimental import pallas as pl
from jax.experimental.pallas import tpu as pltpu

F32 = jnp.float32
BF16 = jnp.bfloat16

D_MODEL = 1024
GRID_W = 64
CTX_LEN = 256
EPS = 1e-6
N_MOD = 9
MASK_VALUE = -1e30
TINY = 1e-20
LOG2E = 1.4426950408889634

D_HG = 256
HG_HEADS = 4
HG_DK = 64
D_NA = 256
NA_HEADS = 4
NA_DH = 64
WIN_R = 8
WIN_C = 16
D_SSM = 512
SSM_HEADS = 8
SSM_P = 64
SSM_N = 128
SSM_G = 2
CONV_W = 5
CONV_DIM = D_SSM + 2 * SSM_G * SSM_N
D_FF = 2816
D_IN_PAD = 3712
LANES = 128
SUBLANES = 8
MXU_COLS = 256
MOD_ROWS = 8

GLA_TILE = 256
GLA_SUB = 32
GLA_SAFE_EXPONENT = 60.0
_GLA_GROUPS = GLA_SUB // SUBLANES
GLA_P_ROWS = (GLA_TILE // GLA_SUB) * SUBLANES * SUBLANES * _GLA_GROUPS * (_GLA_GROUPS + 1) // 2
SSD_Q = 128
SSD_STEP = 256
NA_ROWS = 8

VMEM_LIMIT = 56 * 1024 * 1024


def _cparams(sem):
    return pltpu.CompilerParams(dimension_semantics=sem, vmem_limit_bytes=VMEM_LIMIT)


def _silu(x):
    return x * jax.nn.sigmoid(x)


def _split3(x):
    hi = x.astype(BF16)
    r1 = x - hi.astype(F32)
    mid = r1.astype(BF16)
    lo = (r1 - mid.astype(F32)).astype(BF16)
    return hi, mid, lo


def _ldot3(a_bf16, x):
    hi, mid, lo = _split3(x)
    n = x.shape[1]
    if n % MXU_COLS == 0:
        return (jnp.dot(a_bf16, hi, preferred_element_type=F32)
                + jnp.dot(a_bf16, mid, preferred_element_type=F32)
                + jnp.dot(a_bf16, lo, preferred_element_type=F32))
    r = jnp.dot(a_bf16, jnp.concatenate([hi, mid, lo], axis=1), preferred_element_type=F32)
    return r[:, 0:n] + r[:, n:2 * n] + r[:, 2 * n:3 * n]


def _rdot3(x, e3_bf16):
    return jnp.dot(jnp.concatenate(_split3(x), axis=1), e3_bf16, preferred_element_type=F32)


def _dot_nt(a, b):
    return lax.dot_general(a, b, (((1,), (1,)), ((), ())), preferred_element_type=F32)


def _dot_tn(a, b):
    return lax.dot_general(a, b, (((0,), (0,)), ((), ())), preferred_element_type=F32)


def _modulated(x, nw, shift, scale):
    ms = jnp.mean(x * x, axis=-1, keepdims=True)
    y = x * lax.rsqrt(ms + EPS)
    return (y * nw) * (1.0 + scale) + shift


def _cast_kernel(w_ref, o_ref):
    cols = w_ref.shape[1]
    if o_ref.shape[1] == cols:
        o_ref[...] = w_ref[...].astype(BF16)
    else:
        o_ref[:, 0:cols] = w_ref[...].astype(BF16)
        o_ref[:, cols:] = jnp.zeros((o_ref.shape[0], o_ref.shape[1] - cols), BF16)


def _cast_call(w, layer, out_cols=None, tr=256):
    _, rows, cols = w.shape
    out_cols = out_cols or cols
    return pl.pallas_call(
        _cast_kernel,
        grid=(rows // tr,),
        in_specs=[pl.BlockSpec((None, tr, cols), lambda i: (layer, i, 0))],
        out_specs=pl.BlockSpec((tr, out_cols), lambda i: (i, 0)),
        out_shape=jax.ShapeDtypeStruct((rows, out_cols), BF16),
        compiler_params=_cparams(("parallel",)),
        name="cast",
    )(w)


def _mods_kernel(c_ref, w_ref, b_ref, o_ref):
    sc = _silu(c_ref[...]).astype(BF16)
    o_ref[...] = jnp.dot(sc, w_ref[...].astype(BF16), preferred_element_type=F32) + b_ref[...]


def _mods_call(c_rows, w_mod, b_mod):
    depth = w_mod.shape[0]
    tn = 1152
    n = N_MOD * D_MODEL
    return pl.pallas_call(
        _mods_kernel,
        grid=(depth, n // tn),
        in_specs=[
            pl.BlockSpec((MOD_ROWS, D_MODEL), lambda l, j: (0, 0)),
            pl.BlockSpec((None, D_MODEL, tn), lambda l, j: (l, 0, j)),
            pl.BlockSpec((None, 1, tn), lambda l, j: (l, 0, j)),
        ],
        out_specs=pl.BlockSpec((None, MOD_ROWS, tn), lambda l, j: (l, 0, j)),
        out_shape=jax.ShapeDtypeStruct((depth, MOD_ROWS, n), F32),
        compiler_params=_cparams(("parallel", "parallel")),
        name="mods",
    )(c_rows, w_mod, b_mod.reshape(depth, 1, n))


def _mod_spec(tm, seq, nb):
    tpb = seq // tm
    return pl.BlockSpec((None, N_MOD, D_MODEL), lambda i: (jnp.minimum(i // tpb, nb), 0, 0))


N_MIX_OPERANDS = 13


def _ffn_kernel(*refs, k_shift, k_scale, k_gate, final, n_lat_tiles, mixed):
    if mixed:
        x_ref, mod_ref = refs[0:2]
        mix_refs = refs[2:2 + N_MIX_OPERANDS]
        nw_ref, wu_ref, wg_ref, w2_ref, fw_ref, o_ref, act_ref = refs[2 + N_MIX_OPERANDS:]
        x = _mixed_residual(x_ref[...], mod_ref, *mix_refs)
    elif n_lat_tiles is None:
        x_ref, mod_ref, nw_ref, wu_ref, wg_ref, w2_ref, fw_ref, o_ref, act_ref = refs
        x = x_ref[...]
    else:
        x_ref, xc_ref, mod_ref, nw_ref, wu_ref, wg_ref, w2_ref, fw_ref, o_ref, act_ref = refs
        x = jnp.where(pl.program_id(0) >= n_lat_tiles, xc_ref[...], x_ref[...])
    h = _modulated(x, nw_ref[...], mod_ref[k_shift:k_shift + 1, :],
                   mod_ref[k_scale:k_scale + 1, :]).astype(BF16)
    cw = MXU_COLS
    for c in range(D_FF // cw):
        u = jnp.dot(h, wu_ref[:, c * cw:(c + 1) * cw], preferred_element_type=F32)
        g = jnp.dot(h, wg_ref[:, c * cw:(c + 1) * cw], preferred_element_type=F32)
        act_ref[:, c * cw:(c + 1) * cw] = (_silu(g) * u).astype(BF16)
    y = jnp.dot(act_ref[...], w2_ref[...], preferred_element_type=F32)
    out = x + (0.5 * mod_ref[k_gate:k_gate + 1, :]) * y
    if final:
        ms = jnp.mean(out * out, axis=-1, keepdims=True)
        out = (out * lax.rsqrt(ms + EPS)) * fw_ref[...]
    o_ref[...] = out


def _ffn_call(x_all, mods_l, nw, w13b, w2b, fw, *, n_rows, seq, nb, ks, final, x_ctx=None, mix=None, tm=512):
    mod_spec = _mod_spec(tm, seq, nb)
    if mix is not None:
        n_lat_tiles = None
        lead_args = (x_all, mods_l) + tuple(mix)
        lead_specs = [pl.BlockSpec((tm, D_MODEL), lambda i: (i, 0)), mod_spec] + _mix_specs(tm)
    elif x_ctx is None:
        n_lat_tiles = None
        lead_args = (x_all, mods_l)
        lead_specs = [pl.BlockSpec((tm, D_MODEL), lambda i: (i, 0)), mod_spec]
    else:
        n_lat_tiles = nb * seq // tm
        lead_args = (x_all, x_ctx, mods_l)
        lead_specs = [pl.BlockSpec((tm, D_MODEL), lambda i: (jnp.minimum(i, n_lat_tiles - 1), 0)),
                      pl.BlockSpec((tm, D_MODEL), lambda i: (jnp.maximum(i - n_lat_tiles, 0), 0)), mod_spec]
    kern = functools.partial(_ffn_kernel, k_shift=ks[0], k_scale=ks[1], k_gate=ks[2], final=final,
                             n_lat_tiles=n_lat_tiles, mixed=mix is not None)
    return pl.pallas_call(
        kern,
        grid=(n_rows // tm,),
        in_specs=lead_specs + [
            pl.BlockSpec((1, D_MODEL), lambda i: (0, 0)),
            pl.BlockSpec((D_MODEL, D_FF), lambda i: (0, 0), pipeline_mode=pl.Buffered(1)),
            pl.BlockSpec((D_MODEL, D_FF), lambda i: (0, 1), pipeline_mode=pl.Buffered(1)),
            pl.BlockSpec((D_FF, D_MODEL), lambda i: (0, 0), pipeline_mode=pl.Buffered(1)),
            pl.BlockSpec((1, D_MODEL), lambda i: (0, 0)),
        ],
        out_specs=pl.BlockSpec((tm, D_MODEL), lambda i: (i, 0)),
        out_shape=jax.ShapeDtypeStruct((n_rows, D_MODEL), F32),
        scratch_shapes=[pltpu.VMEM((tm, D_FF), BF16)],
        compiler_params=_cparams(("parallel",)),
        name="mixffn" if mix is not None else "ffn",
    )(*lead_args, nw, w13b, w13b, w2b, fw)


_IN_SPLITS = ((0, 1280), (1280, 2048), (2048, 2560), (2560, 3584), (3584, 3712))


_XBC_COLS = _IN_SPLITS[3]
HALO = 8


def _inproj_kernel(x_ref, xp_ref, xn_ref, mod_ref, nw_ref, w_ref, cw_ref, cb_ref,
                   hg_ref, na_ref, z_ref, xbc_ref, dt_ref, ext_ref, *, n_lat_tiles, tpb):
    tm = x_ref.shape[0]
    i = pl.program_id(0)
    is_ctx = i >= n_lat_tiles
    first = jnp.logical_or(is_ctx, (i % tpb) == 0)
    last = jnp.logical_or(is_ctx, (i % tpb) == tpb - 1)
    nw = nw_ref[...]
    shift = mod_ref[3:4, :]
    scale = mod_ref[4:5, :]
    h = _modulated(x_ref[...], nw, shift, scale).astype(BF16)
    x_halo = jnp.concatenate([xp_ref[...], xn_ref[...]], axis=0)
    h_halo = _modulated(x_halo, nw, shift, scale).astype(BF16)

    a, b = _XBC_COLS
    wx = w_ref[:, a:b]
    halo = jnp.dot(h_halo, wx, preferred_element_type=F32)
    ext_ref[0:HALO, :] = jnp.where(first, 0.0, halo[0:HALO, :])
    ext_ref[HALO:HALO + tm, :] = jnp.dot(h, wx, preferred_element_type=F32)
    ext_ref[HALO + tm:2 * HALO + tm, :] = jnp.where(last, 0.0, halo[HALO:2 * HALO, :])
    pos = lax.broadcasted_iota(jnp.int32, (tm, 1), 0) % CTX_LEN
    acc = jnp.zeros((tm, CONV_DIM), F32) + cb_ref[...]
    for j in range(CONV_W):
        d = j - CONV_W // 2
        tap = ext_ref[HALO + d:HALO + d + tm, :] * cw_ref[j:j + 1, :]
        if d != 0:
            crosses = jnp.logical_and(is_ctx, jnp.logical_or(pos + d < 0, pos + d >= CTX_LEN))
            tap = jnp.where(crosses, 0.0, tap)
        acc = acc + tap
    xbc_ref[...] = _silu(acc)

    for (a, b), o_ref in zip(_IN_SPLITS, (hg_ref, na_ref, z_ref, None, dt_ref)):
        if o_ref is not None:
            o_ref[...] = jnp.dot(h, w_ref[:, a:b], preferred_element_type=F32).astype(o_ref.dtype)


def _inproj_call(x_all, mods_l, nw, w_in_b, conv_w, conv_b, *, n_rows, seq, nb, tm=512):
    widths = [b - a for a, b in _IN_SPLITS]
    dtypes = [F32, BF16, F32, F32, F32]
    r8 = tm // HALO
    last8 = n_rows // HALO - 1
    kern = functools.partial(_inproj_kernel, n_lat_tiles=nb * seq // tm, tpb=seq // tm)
    return pl.pallas_call(
        kern,
        grid=(n_rows // tm,),
        in_specs=[
            pl.BlockSpec((tm, D_MODEL), lambda i: (i, 0)),
            pl.BlockSpec((HALO, D_MODEL), lambda i: (jnp.maximum(i * r8 - 1, 0), 0)),
            pl.BlockSpec((HALO, D_MODEL), lambda i: (jnp.minimum((i + 1) * r8, last8), 0)),
            _mod_spec(tm, seq, nb),
            pl.BlockSpec((1, D_MODEL), lambda i: (0, 0)),
            pl.BlockSpec((D_MODEL, D_IN_PAD), lambda i: (0, 0), pipeline_mode=pl.Buffered(1)),
            pl.BlockSpec((CONV_W, CONV_DIM), lambda i: (0, 0)),
            pl.BlockSpec((1, CONV_DIM), lambda i: (0, 0)),
        ],
        out_specs=[pl.BlockSpec((tm, w), lambda i: (i, 0)) for w in widths],
        out_shape=[jax.ShapeDtypeStruct((n_rows, w), dt) for w, dt in zip(widths, dtypes)],
        scratch_shapes=[pltpu.VMEM((tm + 2 * HALO, CONV_DIM), F32)],
        compiler_params=_cparams(("parallel",)),
        name="inproj",
    )(x_all, x_all, x_all, mods_l, nw, w_in_b, conv_w, conv_b)


def _scan_block_map(*, seq, nb, q, rev):
    nc_ctx = CTX_LEN // q
    nc_lat = seq // q
    ctx_base = nb * seq // q

    def row_block(b, j):
        cj = j
        lj = j - nc_ctx
        if rev:
            cj = nc_ctx - 1 - cj
            lj = nc_lat - 1 - lj
        return jnp.where(j < nc_ctx, ctx_base + b * nc_ctx + cj, b * nc_lat + lj)

    return row_block, nc_ctx + nc_lat


def _round_robin(stage_generators):
    live = list(stage_generators)
    while live:
        for gen in list(live):
            try:
                next(gen)
            except StopIteration:
                live.remove(gen)


def _ssd_direction(xbc_ref, dt_ref, arow_ref, brow_ref, e_ref, tri_ref, y_ref, st_ref, *, rev, lane0):
    q = SSD_Q
    nch = SSD_STEP // q
    gw = D_SSM // SSM_G
    t_idx = lax.broadcasted_iota(jnp.int32, (q, q), 0)
    s_idx = lax.broadcasted_iota(jnp.int32, (q, q), 1)
    causal = (s_idx >= t_idx) if rev else (s_idx <= t_idx)
    lane = lax.broadcasted_iota(jnp.int32, (q, LANES), 1)
    low_half = lane < SSM_P
    e = e_ref[...]
    tri = tri_ref[...]
    edge = 0 if rev else q - 1
    chunks = [dict(rows=slice(ci * q, (ci + 1) * q))
              for ci in (range(nch - 1, -1, -1) if rev else range(nch))]

    def stage_cumsum(ch):
        rows = ch["rows"]
        dt = jax.nn.softplus(dt_ref[rows, :] + brow_ref[...])
        ch["dt"] = dt
        ch["cum"] = _ldot3(tri, dt * arow_ref[...])

    def stage_expand(ch):
        rows = ch["rows"]
        cum = ch["cum"]
        ch["cum_t"] = cum.T
        cum_e = _rdot3(cum, e)
        ch["cum_e"] = cum_e
        ch["tot_e"] = cum_e[edge:edge + 1, :]
        ch["xdt"] = xbc_ref[rows, 0:D_SSM] * _rdot3(ch["dt"], e)
        ch["bg"] = []
        ch["cg"] = []
        ch["cb"] = []
        for g in range(SSM_G):
            b0 = D_SSM + g * SSM_N
            c0 = D_SSM + SSM_G * SSM_N + g * SSM_N
            bg = xbc_ref[rows, b0:b0 + SSM_N].astype(BF16)
            cg = xbc_ref[rows, c0:c0 + SSM_N].astype(BF16)
            ch["bg"].append(bg)
            ch["cg"].append(cg)
            ch["cb"].append(_dot_nt(cg, bg))

    def stage_local(ch):
        cum, cum_t, cum_e, tot_e, xdt = ch["cum"], ch["cum_t"], ch["cum_e"], ch["tot_e"], ch["xdt"]
        ch["y_diag"] = []
        ch["upd"] = []
        for g in range(SSM_G):
            gl = g * gw
            for hp in range(2):
                pl0 = gl + hp * LANES
                xpair = xdt[:, pl0:pl0 + LANES]
                acc = jnp.zeros((q, LANES), F32)
                for hh in range(2):
                    idx = lane0 + g * 4 + hp * 2 + hh
                    diff = cum[:, idx:idx + 1] - cum_t[idx:idx + 1, :]
                    m = ch["cb"][g] * jnp.exp(jnp.where(causal, diff, MASK_VALUE))
                    xm = jnp.where(low_half if hh == 0 else jnp.logical_not(low_half), xpair, 0.0)
                    acc = acc + jnp.dot(m.astype(BF16), xm.astype(BF16), preferred_element_type=F32)
                ch["y_diag"].append(acc)
            xw = (xdt[:, gl:gl + gw] * jnp.exp(tot_e[:, gl:gl + gw] - cum_e[:, gl:gl + gw])).astype(BF16)
            ch["upd"].append(_dot_tn(ch["bg"][g], xw))
        ch["off_scale"] = jnp.exp(cum_e)
        ch["dec"] = jnp.exp(tot_e)

    def stage_state():
        st = [st_ref[:, g * gw:(g + 1) * gw] for g in range(SSM_G)]
        for ch in chunks:
            y_parts = []
            for g in range(SSM_G):
                gl = g * gw
                y_off = jnp.dot(ch["cg"][g], st[g].astype(BF16), preferred_element_type=F32)
                y_parts.append(jnp.concatenate(ch["y_diag"][2 * g:2 * g + 2], axis=1)
                               + y_off * ch["off_scale"][:, gl:gl + gw])
                st[g] = st[g] * ch["dec"][:, gl:gl + gw] + ch["upd"][g]
            y_ref[ch["rows"], :] = jnp.concatenate(y_parts, axis=1)
        for g in range(SSM_G):
            st_ref[:, g * gw:(g + 1) * gw] = st[g]

    for ch in chunks:
        yield stage_cumsum(ch)
    for ch in chunks:
        yield stage_expand(ch)
    for ch in chunks:
        yield stage_local(ch)
    yield stage_state()


def _ssd_kernel(xf_ref, xb_ref, dtf_ref, dtb_ref, arow_ref, brow_ref, ef_ref, eb_ref, trif_ref, trib_ref,
                yf_ref, yb_ref, stf_ref, stb_ref):
    @pl.when(pl.program_id(1) == 0)
    def _():
        stf_ref[...] = jnp.zeros_like(stf_ref)
        stb_ref[...] = jnp.zeros_like(stb_ref)

    _round_robin([
        _ssd_direction(xf_ref, dtf_ref, arow_ref, brow_ref, ef_ref, trif_ref, yf_ref, stf_ref,
                       rev=False, lane0=0),
        _ssd_direction(xb_ref, dtb_ref, arow_ref, brow_ref, eb_ref, trib_ref, yb_ref, stb_ref,
                       rev=True, lane0=SSM_HEADS)])


def _ssd_call(xbc_c, dt_raw, a_row, b_row, e_f, e_b, tri_f, tri_b, *, n_rows, seq, nb):
    q = SSD_STEP
    blk_f, steps = _scan_block_map(seq=seq, nb=nb, q=q, rev=False)
    blk_b, _ = _scan_block_map(seq=seq, nb=nb, q=q, rev=True)
    const = lambda b, j: (0, 0)
    return pl.pallas_call(
        _ssd_kernel,
        grid=(nb, steps),
        in_specs=[
            pl.BlockSpec((q, CONV_DIM), lambda b, j: (blk_f(b, j), 0)),
            pl.BlockSpec((q, CONV_DIM), lambda b, j: (blk_b(b, j), 0)),
            pl.BlockSpec((q, LANES), lambda b, j: (blk_f(b, j), 0)),
            pl.BlockSpec((q, LANES), lambda b, j: (blk_b(b, j), 0)),
            pl.BlockSpec((1, LANES), const),
            pl.BlockSpec((1, LANES), const),
            pl.BlockSpec((3 * LANES, D_SSM), const),
            pl.BlockSpec((3 * LANES, D_SSM), const),
            pl.BlockSpec((SSD_Q, SSD_Q), const),
            pl.BlockSpec((SSD_Q, SSD_Q), const),
        ],
        out_specs=[pl.BlockSpec((q, D_SSM), lambda b, j: (blk_f(b, j), 0)),
                   pl.BlockSpec((q, D_SSM), lambda b, j: (blk_b(b, j), 0))],
        out_shape=[jax.ShapeDtypeStruct((n_rows, D_SSM), F32)] * 2,
        scratch_shapes=[pltpu.VMEM((SSM_N, D_SSM), F32)] * 2,
        compiler_params=_cparams(("arbitrary", "arbitrary")),
        name="ssd",
    )(xbc_c, xbc_c, dt_raw, dt_raw, a_row, b_row, e_f, e_b, tri_f, tri_b)


def _gla_direction(q_ref, f_ref, v_ref, lb_ref, tri_ref, same_blk, ind_ref, o_ref, p_s, upd_s, oi_s, qkb_s,
                   st_ref, *, rev):
    t = GLA_TILE
    c = GLA_SUB
    nsub = t // c
    lb = lb_ref[...]
    fr = f_ref[...]
    f = lb + (1.0 - lb) * jax.nn.sigmoid(fr)
    logf = jnp.log(jnp.maximum(f, TINY))
    k = (1.0 - lb) * jax.nn.sigmoid(-fr)
    qv = _silu(q_ref[...])
    vv = v_ref[...]
    yield
    brel = _ldot3(tri_ref[...], logf)
    tot = _ldot3(same_blk, logf)
    qkb_s[0] = qv
    qkb_s[1] = k
    qkb_s[2] = brel
    yield

    qt = (qv * jnp.exp(brel)).astype(BF16)
    kt = k * jnp.exp(tot - brel)
    dec_blk = jnp.exp(tot)
    lane = lax.broadcasted_iota(jnp.int32, (t, D_HG), 1)
    head_masks = [(lane >= h * HG_DK) & (lane < (h + 1) * HG_DK) for h in range(HG_HEADS)]
    kxs = [jnp.where(m, kt, 0.0).astype(BF16) for m in head_masks]
    vxs = [jnp.where(m, vv, 0.0).astype(BF16) for m in head_masks]
    for blk in range(nsub):
        r0 = blk * c
        kx = jnp.concatenate([a[r0:r0 + c, :] for a in kxs], axis=0)
        vx = jnp.concatenate([a[r0:r0 + c, :] for a in vxs], axis=0)
        upd_s[blk] = _dot_tn(vx, kx)
        if blk % 4 == 3:
            yield

    mid = c // 2
    bref = jnp.concatenate([jnp.broadcast_to(brel[b * c + mid:b * c + mid + 1, :], (c, D_HG))
                            for b in range(nsub)], axis=0)
    dev = brel - bref
    worst = jnp.max(jnp.max(jnp.abs(dev), axis=1, keepdims=True), axis=0, keepdims=True)
    safe_v = worst <= GLA_SAFE_EXPONENT
    safe = worst[0, 0] <= GLA_SAFE_EXPONENT
    qh = qv * jnp.exp(dev)
    kh = (k * jnp.exp(-dev)).astype(BF16)
    qx = jnp.concatenate([jnp.where(m, qh, 0.0).astype(BF16) for m in head_masks], axis=0)
    yield
    sc = _dot_nt(qx, kh)
    visible = tri_ref[...].astype(F32) > 0.5
    pm = jnp.concatenate([jnp.where(visible, sc[h * t:(h + 1) * t, :], 0.0).astype(BF16)
                          for h in range(HG_HEADS)], axis=0)
    yield
    oh = jnp.dot(pm, vv.astype(BF16), preferred_element_type=F32)
    o_fast = jnp.where(head_masks[0], oh[0:t, :], 0.0)
    for h in range(1, HG_HEADS):
        o_fast = o_fast + jnp.where(head_masks[h], oh[h * t:(h + 1) * t, :], 0.0)
    oi_s[...] = jnp.where(safe_v, o_fast, 0.0)
    yield

    @pl.when(jnp.logical_not(safe))
    def _():
        _gla_intra_pairwise(qkb_s, v_ref, ind_ref, p_s, oi_s, rev=rev)

    yield

    st = st_ref[...]
    o_inter = [None] * nsub
    for blk in (range(nsub - 1, -1, -1) if rev else range(nsub)):
        r0 = blk * c
        o_inter[blk] = _dot_nt(qt[r0:r0 + c, :], st.astype(BF16))
        st = st * dec_blk[r0:r0 + 1, :] + upd_s[blk]
        yield
    st_ref[...] = st
    o_ref[...] = oi_s[...] + jnp.concatenate(o_inter, axis=0)


def _gla_intra_pairwise(qkb_s, v_ref, ind_ref, p_s, oi_s, *, rev):
    t = GLA_TILE
    c = GLA_SUB
    nsub = t // c
    sub = SUBLANES
    qv, k, brel = qkb_s[0], qkb_s[1], qkb_s[2]
    vv = v_ref[...]
    t_idx = lax.broadcasted_iota(jnp.int32, (sub, D_HG), 0)
    pieces = [(s, g) for s in range(c) for g in range(c // sub)
              if (g <= s // sub if rev else g >= s // sub)]
    rows_per_blk = len(pieces) * sub

    brel2 = brel * LOG2E
    for blk in range(nsub):
        r0 = blk * c
        qb = qv[r0:r0 + c, :]
        kb = k[r0:r0 + c, :]
        bb = brel2[r0:r0 + c, :]
        for u in range(0, len(pieces), 2):
            rows = []
            for s, g in pieces[u:u + 2]:
                tt = t_idx + g * sub
                keep = (tt <= s) if rev else (tt >= s)
                dec = jnp.exp2(jnp.where(keep, bb[g * sub:(g + 1) * sub, :] - bb[s:s + 1, :], MASK_VALUE))
                rows.append(qb[g * sub:(g + 1) * sub, :] * kb[s:s + 1, :] * dec)
            po = blk * rows_per_blk + u * sub
            p_s[po:po + 2 * sub, :] = jnp.concatenate(rows, axis=0).astype(BF16)
    r = jnp.dot(p_s[...], ind_ref[...], preferred_element_type=F32)
    for blk in range(nsub):
        r0 = blk * c
        vb = vv[r0:r0 + c, :]
        accs = [jnp.zeros((sub, D_HG), F32) for _ in range(c // sub)]
        for n, (s, g) in enumerate(pieces):
            po = blk * rows_per_blk + n * sub
            accs[g] = accs[g] + r[po:po + sub, :] * vb[s:s + 1, :]
        oi_s[r0:r0 + c, :] = jnp.concatenate(accs, axis=0)


def _gla_kernel(qf_ref, ff_ref, vf_ref, qb_ref, fb_ref, vb_ref, lbf_ref, lbb_ref, trif_ref, trib_ref,
                ind_ref, of_ref, ob_ref, pf_s, pb_s, updf_s, updb_s, oif_s, oib_s, qkbf_s, qkbb_s,
                stf_ref, stb_ref):
    @pl.when(pl.program_id(1) == 0)
    def _():
        stf_ref[...] = jnp.zeros_like(stf_ref)
        stb_ref[...] = jnp.zeros_like(stb_ref)

    same_blk = jnp.maximum(trif_ref[...], trib_ref[...])
    _round_robin([
        _gla_direction(qf_ref, ff_ref, vf_ref, lbf_ref, trif_ref, same_blk, ind_ref, of_ref,
                       pf_s, updf_s, oif_s, qkbf_s, stf_ref, rev=False),
        _gla_direction(qb_ref, fb_ref, vb_ref, lbb_ref, trib_ref, same_blk, ind_ref, ob_ref,
                       pb_s, updb_s, oib_s, qkbb_s, stb_ref, rev=True)])


def _gla_call(hg, lb_f, lb_b, tri_f, tri_b, ind, *, n_rows, seq, nb):
    t = GLA_TILE
    blk_f, steps = _scan_block_map(seq=seq, nb=nb, q=t, rev=False)
    blk_b, _ = _scan_block_map(seq=seq, nb=nb, q=t, rev=True)
    const = lambda b, j: (0, 0)
    col = lambda blk, cidx: pl.BlockSpec((t, D_HG), lambda b, j: (blk(b, j), cidx))
    return pl.pallas_call(
        _gla_kernel,
        grid=(nb, steps),
        in_specs=[
            col(blk_f, 0), col(blk_f, 1), col(blk_f, 3),
            col(blk_b, 0), col(blk_b, 2), col(blk_b, 3),
            pl.BlockSpec((1, D_HG), const),
            pl.BlockSpec((1, D_HG), const),
            pl.BlockSpec((t, t), const),
            pl.BlockSpec((t, t), const),
            pl.BlockSpec((D_HG, D_HG), const),
        ],
        out_specs=[col(blk_f, 0), col(blk_b, 0)],
        out_shape=[jax.ShapeDtypeStruct((n_rows, D_HG), F32)] * 2,
        scratch_shapes=(
            [pltpu.VMEM((GLA_P_ROWS, D_HG), BF16)] * 2
            + [pltpu.VMEM((GLA_TILE // GLA_SUB, D_HG, D_HG), F32)] * 2
            + [pltpu.VMEM((GLA_TILE, D_HG), F32)] * 2
            + [pltpu.VMEM((3, GLA_TILE, D_HG), F32)] * 2
            + [pltpu.VMEM((D_HG, D_HG), F32)] * 2),
        compiler_params=_cparams(("arbitrary", "arbitrary")),
        name="gla",
    )(hg, hg, hg, hg, hg, hg, lb_f, lb_b, tri_f, tri_b, ind)


def _expand_heads(x, masks):
    return jnp.concatenate([jnp.where(m, x, jnp.zeros_like(x)) for m in masks], axis=0)


def _collapse_heads(o, masks, t):
    acc = jnp.where(masks[0], o[0:t, :], 0.0)
    for h in range(1, NA_HEADS):
        acc = acc + jnp.where(masks[h], o[h * t:(h + 1) * t, :], 0.0)
    return acc


def _na_finish(o, nw):
    ms = jnp.mean(o * o, axis=-1, keepdims=True)
    return (o * lax.rsqrt(ms + EPS)) * nw


def _na_kernel(q_ref, k_ref, v_ref, kc_ref, vc_ref, bias_ref, nw_ref, o_ref, *, n_grid_rows):
    w = GRID_W
    i = pl.program_id(1)
    lane = lax.broadcasted_iota(jnp.int32, (w, D_NA), 1)
    masks = [(lane >= h * NA_DH) & (lane < (h + 1) * NA_DH) for h in range(NA_HEADS)]
    kc = kc_ref[...]
    vc = vc_ref[...]
    nw = nw_ref[...]
    scale = NA_DH ** -0.5

    def body(jr, carry):
        r = i * NA_ROWS + jr
        r0 = jnp.clip(r - WIN_R // 2, 0, n_grid_rows - WIN_R)
        var = r - r0
        k0 = pl.multiple_of(r0 * w, w)
        q0 = pl.multiple_of(jr * w, w)
        qx = _expand_heads(q_ref[pl.ds(q0, w), :] * scale, masks)
        kw = k_ref[pl.ds(k0, WIN_R * w), :]
        vw = v_ref[pl.ds(k0, WIN_R * w), :]
        s_loc = _dot_nt(qx, kw) + bias_ref[var]
        s_ctx = _dot_nt(qx, kc)
        m = jnp.maximum(jnp.max(s_loc, axis=-1, keepdims=True), jnp.max(s_ctx, axis=-1, keepdims=True))
        p_loc = jnp.exp(s_loc - m)
        p_ctx = jnp.exp(s_ctx - m)
        den = jnp.sum(p_loc, axis=-1, keepdims=True) + jnp.sum(p_ctx, axis=-1, keepdims=True)
        o = (jnp.dot(p_loc.astype(BF16), vw, preferred_element_type=F32)
             + jnp.dot(p_ctx.astype(BF16), vc, preferred_element_type=F32)) / den
        o_ref[pl.ds(q0, w), :] = _na_finish(_collapse_heads(o, masks, w), nw)
        return carry

    lax.fori_loop(0, NA_ROWS, body, 0, unroll=True)


def _na_call(na, bias, nw, *, seq, nb):
    rows = seq // GRID_W
    tq = NA_ROWS * GRID_W
    qpb = seq // tq
    ctx_blk = nb * seq // CTX_LEN
    kern = functools.partial(_na_kernel, n_grid_rows=rows)
    return pl.pallas_call(
        kern,
        grid=(nb, qpb),
        in_specs=[
            pl.BlockSpec((tq, D_NA), lambda b, i: (b * qpb + i, 0)),
            pl.BlockSpec((seq, D_NA), lambda b, i: (b, 1)),
            pl.BlockSpec((seq, D_NA), lambda b, i: (b, 2)),
            pl.BlockSpec((CTX_LEN, D_NA), lambda b, i: (ctx_blk + b, 1)),
            pl.BlockSpec((CTX_LEN, D_NA), lambda b, i: (ctx_blk + b, 2)),
            pl.BlockSpec((WIN_R, NA_HEADS * GRID_W, WIN_R * GRID_W), lambda b, i: (0, 0, 0)),
            pl.BlockSpec((1, D_NA), lambda b, i: (0, 0)),
        ],
        out_specs=pl.BlockSpec((tq, D_NA), lambda b, i: (b * qpb + i, 0)),
        out_shape=jax.ShapeDtypeStruct((nb * seq, D_NA), F32),
        compiler_params=_cparams(("parallel", "arbitrary")),
        name="natten",
    )(na, na, na, na, na, bias, nw)


def _ctxattn_kernel(q_ref, k_ref, v_ref, nw_ref, o_ref):
    t = CTX_LEN
    lane = lax.broadcasted_iota(jnp.int32, (t, D_NA), 1)
    masks = [(lane >= h * NA_DH) & (lane < (h + 1) * NA_DH) for h in range(NA_HEADS)]
    qx = _expand_heads(q_ref[...] * (NA_DH ** -0.5), masks)
    s = _dot_nt(qx, k_ref[...])
    m = jnp.max(s, axis=-1, keepdims=True)
    p = jnp.exp(s - m)
    den = jnp.sum(p, axis=-1, keepdims=True)
    o = jnp.dot(p.astype(BF16), v_ref[...], preferred_element_type=F32) / den
    o_ref[...] = _na_finish(_collapse_heads(o, masks, t), nw_ref[...])


def _ctxattn_call(na, nw, *, seq, nb):
    ctx_blk = nb * seq // CTX_LEN
    return pl.pallas_call(
        _ctxattn_kernel,
        grid=(nb,),
        in_specs=[
            pl.BlockSpec((CTX_LEN, D_NA), lambda b: (ctx_blk + b, 0)),
            pl.BlockSpec((CTX_LEN, D_NA), lambda b: (ctx_blk + b, 1)),
            pl.BlockSpec((CTX_LEN, D_NA), lambda b: (ctx_blk + b, 2)),
            pl.BlockSpec((1, D_NA), lambda b: (0, 0)),
        ],
        out_specs=pl.BlockSpec((CTX_LEN, D_NA), lambda b: (b, 0)),
        out_shape=jax.ShapeDtypeStruct((nb * CTX_LEN, D_NA), F32),
        compiler_params=_cparams(("parallel",)),
        name="ctxattn",
    )(na, na, na, nw)


def _mixed_residual(x, mod_ref, of_ref, ob_ref, g_ref, na_ref, yf_ref, yb_ref, xs_ref, z_ref,
                    hgw_ref, hm_ref, dsk_ref, sw_ref, wo_ref):
    hm = hm_ref[...]
    o = of_ref[...] + ob_ref[...]
    sq = o * o
    hi = sq.astype(BF16)
    lo = (sq - hi.astype(F32)).astype(BF16)
    ms = (jnp.dot(hi, hm, preferred_element_type=F32)
          + jnp.dot(lo, hm, preferred_element_type=F32)) * (1.0 / HG_DK)
    hg = (o * lax.rsqrt(ms + EPS)) * hgw_ref[...] * _silu(g_ref[...])
    ys = (yf_ref[...] + yb_ref[...] + dsk_ref[...] * xs_ref[...]) * _silu(z_ref[...])
    ms2 = jnp.mean(ys * ys, axis=-1, keepdims=True)
    ssm = (ys * lax.rsqrt(ms2 + EPS)) * sw_ref[...]
    mix = (jnp.dot(hg.astype(BF16), wo_ref[0:D_HG, :], preferred_element_type=F32)
           + jnp.dot(na_ref[...].astype(BF16), wo_ref[D_HG:D_HG + D_NA, :], preferred_element_type=F32)
           + jnp.dot(ssm.astype(BF16), wo_ref[D_HG + D_NA:, :], preferred_element_type=F32))
    return x + mod_ref[5:6, :] * mix


def _mix_specs(tm):
    row = lambda i: (i, 0)
    const = lambda i: (0, 0)
    return [
        pl.BlockSpec((tm, D_HG), row),
        pl.BlockSpec((tm, D_HG), row),
        pl.BlockSpec((tm, D_HG), lambda i: (i, 4)),
        pl.BlockSpec((tm, D_NA), row),
        pl.BlockSpec((tm, D_SSM), row),
        pl.BlockSpec((tm, D_SSM), row),
        pl.BlockSpec((tm, D_SSM), row),
        pl.BlockSpec((tm, D_SSM), row),
        pl.BlockSpec((1, D_HG), const),
        pl.BlockSpec((D_HG, D_HG), const),
        pl.BlockSpec((1, D_SSM), const),
        pl.BlockSpec((1, D_SSM), const),
        pl.BlockSpec((D_MODEL, D_MODEL), const, pipeline_mode=pl.Buffered(1)),
    ]


def _na_bias_table(rpb):
    w = GRID_W
    ndr = 2 * WIN_R - 1
    ndc = 2 * WIN_C - 1
    col = np.arange(w)
    c0 = np.clip(col - WIN_C // 2, 0, w - WIN_C)
    col_in = (col[None, :] >= c0[:, None]) & (col[None, :] < c0[:, None] + WIN_C)
    dc = np.clip(col[None, :] - col[:, None], -(WIN_C - 1), WIN_C - 1) + (WIN_C - 1)
    onehot = (dc.reshape(1, -1) == np.arange(ndc)[:, None]).astype(np.float32)
    t = jnp.dot(rpb.reshape(NA_HEADS * ndr, ndc).astype(F32), onehot,
                precision=lax.Precision.HIGHEST).reshape(NA_HEADS, ndr, w, w)
    t = jnp.where(col_in[None, None], t, MASK_VALUE)
    b = jnp.stack([t[:, WIN_R - 1 - var:2 * WIN_R - 1 - var] for var in range(WIN_R)], axis=0)
    b = jnp.transpose(b, (0, 1, 3, 2, 4))
    return b.reshape(WIN_R, NA_HEADS * w, WIN_R * w)


def _block_tri(n, c, rev):
    r = jnp.arange(n)
    same = (r[:, None] // c) == (r[None, :] // c)
    tri = (r[None, :] >= r[:, None]) if rev else (r[None, :] <= r[:, None])
    return (same & tri).astype(BF16)


def _head_block_ones(n, hd):
    r = jnp.arange(n)
    return ((r[:, None] // hd) == (r[None, :] // hd)).astype(BF16)


def _ssd_expand(lane0):
    r = jnp.arange(LANES)[:, None]
    cidx = jnp.arange(D_SSM)[None, :]
    e = (r == lane0 + cidx // SSM_P).astype(BF16)
    return jnp.concatenate([e, e, e], axis=0)


def _lane_row(vals):
    return jnp.zeros((1, LANES), F32).at[0, :vals.shape[0]].set(vals.astype(F32))


def kernel(x, c, ctx, c_ctx, w_mod, b_mod, norm_ffn1, ffn1_w13, ffn1_w2, norm_mix, w_in,
           hg_lower_bounds, hg_norm, na_rpb, na_norm, ssm_conv_w, ssm_conv_b, ssm_a_log,
           ssm_dt_bias, ssm_d, ssm_norm, w_out, norm_ffn2, ffn2_w13, ffn2_w2, final_norm):
    nb, seq, d = x.shape
    depth = w_mod.shape[0]
    assert d == D_MODEL and ctx.shape[1] == CTX_LEN and nb + 1 <= MOD_ROWS
    assert seq % 512 == 0 and seq // GRID_W >= WIN_R
    n_lat = nb * seq
    n_all = n_lat + nb * CTX_LEN

    lb_soft = jax.nn.softmax(hg_lower_bounds.astype(F32), axis=1)
    lower_bounds = jnp.cumsum(lb_soft, axis=1) - lb_soft[:, :1]

    c_rows = jnp.zeros((MOD_ROWS, d), F32).at[:nb].set(c).at[nb].set(c_ctx)
    mods = _mods_call(c_rows, w_mod, b_mod).reshape(depth, MOD_ROWS, N_MOD, d)

    x_all = x.reshape(n_lat, d)
    x_ctx = ctx.reshape(nb * CTX_LEN, d)

    tri_f = _block_tri(GLA_TILE, GLA_SUB, False)
    tri_b = _block_tri(GLA_TILE, GLA_SUB, True)
    ind = _head_block_ones(D_HG, HG_DK)
    ssd_tri_f = _block_tri(SSD_Q, SSD_Q, False)
    ssd_tri_b = _block_tri(SSD_Q, SSD_Q, True)
    e_f = _ssd_expand(0)
    e_b = _ssd_expand(SSM_HEADS)
    one_row = lambda v: v.reshape(1, -1).astype(F32)
    common = dict(seq=seq, nb=nb)

    for layer in range(depth):
        last = layer == depth - 1
        mods_l = mods[layer]
        w13_1 = _cast_call(ffn1_w13, layer)
        w2_1 = _cast_call(ffn1_w2, layer)
        w13_2 = _cast_call(ffn2_w13, layer)
        w2_2 = _cast_call(ffn2_w2, layer)
        w_in_b = _cast_call(w_in, layer, out_cols=D_IN_PAD)
        wo_b = _cast_call(w_out, layer)
        fw = one_row(final_norm)

        x_all = _ffn_call(x_all, mods_l, one_row(norm_ffn1[layer]), w13_1, w2_1, fw,
                          n_rows=n_all, ks=(0, 1, 2), final=False, x_ctx=x_ctx if layer == 0 else None,
                          **common)
        hg, na, z, xbc_c, dt_raw = _inproj_call(x_all, mods_l, one_row(norm_mix[layer]), w_in_b,
                                                ssm_conv_w[layer].astype(F32), one_row(ssm_conv_b[layer]),
                                                n_rows=n_all, **common)

        o_f, o_b = _gla_call(hg, one_row(lower_bounds[0, layer]), one_row(lower_bounds[1, layer]),
                             tri_f, tri_b, ind, n_rows=n_all, **common)

        bias = _na_bias_table(na_rpb[layer])
        nw = one_row(na_norm[layer])
        y_na = _na_call(na, bias, nw, **common)
        if not last:
            y_na = jnp.concatenate([y_na, _ctxattn_call(na, nw, **common)], axis=0)

        a_neg = -jnp.exp(ssm_a_log[layer].astype(F32))
        y_f, y_b = _ssd_call(xbc_c, dt_raw, _lane_row(a_neg.reshape(-1)),
                             _lane_row(ssm_dt_bias[layer].reshape(-1)),
                             e_f, e_b, ssd_tri_f, ssd_tri_b, n_rows=n_all, **common)

        n_out = n_lat if last else n_all
        dsk = jnp.repeat(ssm_d[layer].astype(F32), SSM_P).reshape(1, D_SSM)
        mix = (o_f, o_b, hg, y_na, y_f, y_b, xbc_c, z,
               one_row(hg_norm[layer]), ind, dsk, one_row(ssm_norm[layer]), wo_b)
        assert len(mix) == N_MIX_OPERANDS
        x_all = _ffn_call(x_all, mods_l, one_row(norm_ffn2[layer]), w13_2, w2_2, fw,
                          n_rows=n_out, ks=(6, 7, 8), final=last, mix=mix, **common)

    return x_all.reshape(nb, seq, d)
```

```python
import functools
import math

import jax
import jax.numpy as jnp
import numpy as np
from jax import lax
from jax.experimental import pallas as pl
from jax.experimental.pallas import tpu as pltpu

F32 = jnp.float32
BF16 = jnp.bfloat16

D_MODEL = 1024
GRID_W = 64
CTX_LEN = 256
EPS = 1e-6
N_MOD = 9
MASK_VALUE = -1e30
TINY = 1e-20
LOG2E = 1.4426950408889634

D_HG = 256
HG_HEADS = 4
HG_DK = 64
D_NA = 256
NA_HEADS = 4
NA_DH = 64
WIN_R = 8
WIN_C = 16
D_SSM = 512
SSM_HEADS = 8
SSM_P = 64
SSM_N = 128
SSM_G = 2
CONV_W = 5
CONV_DIM = D_SSM + 2 * SSM_G * SSM_N
D_FF = 2816
D_IN_PAD = 3712
LANES = 128
SUBLANES = 8
MXU_COLS = 256
MOD_ROWS = 8

GLA_TILE = 256
GLA_SUB = 32
GLA_SAFE_EXPONENT = 60.0
_GLA_GROUPS = GLA_SUB // SUBLANES
GLA_P_ROWS = (GLA_TILE // GLA_SUB) * SUBLANES * SUBLANES * _GLA_GROUPS * (_GLA_GROUPS + 1) // 2
SSD_Q = 128
SSD_STEP = 256
NA_ROWS = 8

VMEM_LIMIT = 56 * 1024 * 1024


def _cparams(sem):
    return pltpu.CompilerParams(dimension_semantics=sem, vmem_limit_bytes=VMEM_LIMIT)


def _silu(x):
    return x * jax.nn.sigmoid(x)


def _split3(x):
    hi = x.astype(BF16)
    r1 = x - hi.astype(F32)
    mid = r1.astype(BF16)
    lo = (r1 - mid.astype(F32)).astype(BF16)
    return hi, mid, lo


def _ldot3(a_bf16, x):
    hi, mid, lo = _split3(x)
    n = x.shape[1]
    if n % MXU_COLS == 0:
        return (jnp.dot(a_bf16, hi, preferred_element_type=F32)
                + jnp.dot(a_bf16, mid, preferred_element_type=F32)
                + jnp.dot(a_bf16, lo, preferred_element_type=F32))
    r = jnp.dot(a_bf16, jnp.concatenate([hi, mid, lo], axis=1), preferred_element_type=F32)
    return r[:, 0:n] + r[:, n:2 * n] + r[:, 2 * n:3 * n]


def _rdot3(x, e3_bf16):
    return jnp.dot(jnp.concatenate(_split3(x), axis=1), e3_bf16, preferred_element_type=F32)


def _dot_nt(a, b):
    return lax.dot_general(a, b, (((1,), (1,)), ((), ())), preferred_element_type=F32)


def _dot_tn(a, b):
    return lax.dot_general(a, b, (((0,), (0,)), ((), ())), preferred_element_type=F32)


def _modulated(x, nw, shift, scale):
    ms = jnp.mean(x * x, axis=-1, keepdims=True)
    y = x * lax.rsqrt(ms + EPS)
    return (y * nw) * (1.0 + scale) + shift


def _cast_kernel(w_ref, o_ref):
    cols = w_ref.shape[1]
    if o_ref.shape[1] == cols:
        o_ref[...] = w_ref[...].astype(BF16)
    else:
        o_ref[:, 0:cols] = w_ref[...].astype(BF16)
        o_ref[:, cols:] = jnp.zeros((o_ref.shape[0], o_ref.shape[1] - cols), BF16)


def _cast_call(w, layer, out_cols=None, tr=256):
    _, rows, cols = w.shape
    out_cols = out_cols or cols
    return pl.pallas_call(
        _cast_kernel,
        grid=(rows // tr,),
        in_specs=[pl.BlockSpec((None, tr, cols), lambda i: (layer, i, 0))],
        out_specs=pl.BlockSpec((tr, out_cols), lambda i: (i, 0)),
        out_shape=jax.ShapeDtypeStruct((rows, out_cols), BF16),
        compiler_params=_cparams(("parallel",)),
        name="cast",
    )(w)


def _mods_kernel(c_ref, w_ref, b_ref, o_ref):
    sc = _silu(c_ref[...]).astype(BF16)
    o_ref[...] = jnp.dot(sc, w_ref[...].astype(BF16), preferred_element_type=F32) + b_ref[...]


def _mods_call(c_rows, w_mod, b_mod):
    depth = w_mod.shape[0]
    tn = 1152
    n = N_MOD * D_MODEL
    return pl.pallas_call(
        _mods_kernel,
        grid=(depth, n // tn),
        in_specs=[
            pl.BlockSpec((MOD_ROWS, D_MODEL), lambda l, j: (0, 0)),
            pl.BlockSpec((None, D_MODEL, tn), lambda l, j: (l, 0, j)),
            pl.BlockSpec((None, 1, tn), lambda l, j: (l, 0, j)),
        ],
        out_specs=pl.BlockSpec((None, MOD_ROWS, tn), lambda l, j: (l, 0, j)),
        out_shape=jax.ShapeDtypeStruct((depth, MOD_ROWS, n), F32),
        compiler_params=_cparams(("parallel", "parallel")),
        name="mods",
    )(c_rows, w_mod, b_mod.reshape(depth, 1, n))


def _mod_spec(tm, seq, nb):
    tpb = seq // tm
    return pl.BlockSpec((None, N_MOD, D_MODEL), lambda i: (jnp.minimum(i // tpb, nb), 0, 0))


MIXFFN_CHUNKS = 2
N_MIX_OPERANDS = 13


def _ffn_kernel(*refs, k_shift, k_scale, k_gate, final, n_lat_tiles, mixed):
    if mixed:
        x_ref, mod_ref = refs[0:2]
        mix_refs = refs[2:2 + N_MIX_OPERANDS]
        nw_ref, wu_ref, wg_ref, w2_ref, fw_ref, o_ref, act_ref = refs[2 + N_MIX_OPERANDS:]
        x = None
    elif n_lat_tiles is None:
        x_ref, mod_ref, nw_ref, wu_ref, wg_ref, w2_ref, fw_ref, o_ref, act_ref = refs
        x = x_ref[...]
    else:
        x_ref, xc_ref, mod_ref, nw_ref, wu_ref, wg_ref, w2_ref, fw_ref, o_ref, act_ref = refs
        x = jnp.where(pl.program_id(0) >= n_lat_tiles, xc_ref[...], x_ref[...])
    tm = o_ref.shape[0]
    cw = MXU_COLS
    chunks = MIXFFN_CHUNKS if mixed else 1
    rc = tm // chunks
    for n in range(chunks):
        rows = slice(n * rc, (n + 1) * rc)
        xr = _mixed_residual(x_ref[rows, :], mod_ref, *mix_refs, rows=rows) if mixed else x
        h = _modulated(xr, nw_ref[...], mod_ref[k_shift:k_shift + 1, :],
                       mod_ref[k_scale:k_scale + 1, :]).astype(BF16)
        for c in range(D_FF // cw):
            u = jnp.dot(h, wu_ref[:, c * cw:(c + 1) * cw], preferred_element_type=F32)
            g = jnp.dot(h, wg_ref[:, c * cw:(c + 1) * cw], preferred_element_type=F32)
            act_ref[rows, c * cw:(c + 1) * cw] = (_silu(g) * u).astype(BF16)
        y = jnp.dot(act_ref[rows, :], w2_ref[...], preferred_element_type=F32)
        out = xr + (0.5 * mod_ref[k_gate:k_gate + 1, :]) * y
        if final:
            ms = jnp.mean(out * out, axis=-1, keepdims=True)
            out = (out * lax.rsqrt(ms + EPS)) * fw_ref[...]
        o_ref[rows, :] = out


def _ffn_call(x_all, mods_l, nw, w13b, w2b, fw, *, n_rows, seq, nb, ks, final, x_ctx=None, mix=None, tm=512):
    mod_spec = _mod_spec(tm, seq, nb)
    if mix is not None:
        n_lat_tiles = None
        lead_args = (x_all, mods_l) + tuple(mix)
        lead_specs = [pl.BlockSpec((tm, D_MODEL), lambda i: (i, 0)), mod_spec] + _mix_specs(tm)
    elif x_ctx is None:
        n_lat_tiles = None
        lead_args = (x_all, mods_l)
        lead_specs = [pl.BlockSpec((tm, D_MODEL), lambda i: (i, 0)), mod_spec]
    else:
        n_lat_tiles = nb * seq // tm
        lead_args = (x_all, x_ctx, mods_l)
        lead_specs = [pl.BlockSpec((tm, D_MODEL), lambda i: (jnp.minimum(i, n_lat_tiles - 1), 0)),
                      pl.BlockSpec((tm, D_MODEL), lambda i: (jnp.maximum(i - n_lat_tiles, 0), 0)), mod_spec]
    kern = functools.partial(_ffn_kernel, k_shift=ks[0], k_scale=ks[1], k_gate=ks[2], final=final,
                             n_lat_tiles=n_lat_tiles, mixed=mix is not None)
    return pl.pallas_call(
        kern,
        grid=(n_rows // tm,),
        in_specs=lead_specs + [
            pl.BlockSpec((1, D_MODEL), lambda i: (0, 0)),
            pl.BlockSpec((D_MODEL, D_FF), lambda i: (0, 0), pipeline_mode=pl.Buffered(1)),
            pl.BlockSpec((D_MODEL, D_FF), lambda i: (0, 1), pipeline_mode=pl.Buffered(1)),
            pl.BlockSpec((D_FF, D_MODEL), lambda i: (0, 0), pipeline_mode=pl.Buffered(1)),
            pl.BlockSpec((1, D_MODEL), lambda i: (0, 0)),
        ],
        out_specs=pl.BlockSpec((tm, D_MODEL), lambda i: (i, 0)),
        out_shape=jax.ShapeDtypeStruct((n_rows, D_MODEL), F32),
        scratch_shapes=[pltpu.VMEM((tm, D_FF), BF16)],
        compiler_params=_cparams(("parallel",)),
        name="mixffn" if mix is not None else "ffn",
    )(*lead_args, nw, w13b, w13b, w2b, fw)


_IN_SPLITS = ((0, 1280), (1280, 2048), (2048, 2560), (2560, 3584), (3584, 3712))


_XBC_COLS = _IN_SPLITS[3]
HALO = 8


def _inproj_kernel(x_ref, xp_ref, xn_ref, mod_ref, nw_ref, w_ref, cw_ref, cb_ref,
                   hg_ref, na_ref, z_ref, xbc_ref, dt_ref, ext_ref, *, n_lat_tiles, tpb):
    tm = x_ref.shape[0]
    i = pl.program_id(0)
    is_ctx = i >= n_lat_tiles
    first = jnp.logical_or(is_ctx, (i % tpb) == 0)
    last = jnp.logical_or(is_ctx, (i % tpb) == tpb - 1)
    nw = nw_ref[...]
    shift = mod_ref[3:4, :]
    scale = mod_ref[4:5, :]
    h = _modulated(x_ref[...], nw, shift, scale).astype(BF16)
    x_halo = jnp.concatenate([xp_ref[...], xn_ref[...]], axis=0)
    h_halo = _modulated(x_halo, nw, shift, scale).astype(BF16)

    a, b = _XBC_COLS
    wx = w_ref[:, a:b]
    halo = jnp.dot(h_halo, wx, preferred_element_type=F32)
    ext_ref[0:HALO, :] = jnp.where(first, 0.0, halo[0:HALO, :])
    ext_ref[HALO:HALO + tm, :] = jnp.dot(h, wx, preferred_element_type=F32)
    ext_ref[HALO + tm:2 * HALO + tm, :] = jnp.where(last, 0.0, halo[HALO:2 * HALO, :])
    pos = lax.broadcasted_iota(jnp.int32, (tm, 1), 0) % CTX_LEN
    acc = jnp.zeros((tm, CONV_DIM), F32) + cb_ref[...]
    for j in range(CONV_W):
        d = j - CONV_W // 2
        tap = ext_ref[HALO + d:HALO + d + tm, :] * cw_ref[j:j + 1, :]
        if d != 0:
            crosses = jnp.logical_and(is_ctx, jnp.logical_or(pos + d < 0, pos + d >= CTX_LEN))
            tap = jnp.where(crosses, 0.0, tap)
        acc = acc + tap
    xbc_ref[...] = _silu(acc)

    for (a, b), o_ref in zip(_IN_SPLITS, (hg_ref, na_ref, z_ref, None, dt_ref)):
        if o_ref is not None:
            o_ref[...] = jnp.dot(h, w_ref[:, a:b], preferred_element_type=F32).astype(o_ref.dtype)


def _inproj_call(x_all, mods_l, nw, w_in_b, conv_w, conv_b, *, n_rows, seq, nb, tm=512):
    widths = [b - a for a, b in _IN_SPLITS]
    dtypes = [F32, BF16, F32, F32, F32]
    r8 = tm // HALO
    last8 = n_rows // HALO - 1
    kern = functools.partial(_inproj_kernel, n_lat_tiles=nb * seq // tm, tpb=seq // tm)
    return pl.pallas_call(
        kern,
        grid=(n_rows // tm,),
        in_specs=[
            pl.BlockSpec((tm, D_MODEL), lambda i: (i, 0)),
            pl.BlockSpec((HALO, D_MODEL), lambda i: (jnp.maximum(i * r8 - 1, 0), 0)),
            pl.BlockSpec((HALO, D_MODEL), lambda i: (jnp.minimum((i + 1) * r8, last8), 0)),
            _mod_spec(tm, seq, nb),
            pl.BlockSpec((1, D_MODEL), lambda i: (0, 0)),
            pl.BlockSpec((D_MODEL, D_IN_PAD), lambda i: (0, 0), pipeline_mode=pl.Buffered(1)),
            pl.BlockSpec((CONV_W, CONV_DIM), lambda i: (0, 0)),
            pl.BlockSpec((1, CONV_DIM), lambda i: (0, 0)),
        ],
        out_specs=[pl.BlockSpec((tm, w), lambda i: (i, 0)) for w in widths],
        out_shape=[jax.ShapeDtypeStruct((n_rows, w), dt) for w, dt in zip(widths, dtypes)],
        scratch_shapes=[pltpu.VMEM((tm + 2 * HALO, CONV_DIM), F32)],
        compiler_params=_cparams(("parallel",)),
        name="inproj",
    )(x_all, x_all, x_all, mods_l, nw, w_in_b, conv_w, conv_b)


def _scan_block_map(*, seq, nb, q, rev):
    nc_ctx = CTX_LEN // q
    nc_lat = seq // q
    ctx_base = nb * seq // q

    def row_block(b, j):
        cj = j
        lj = j - nc_ctx
        if rev:
            cj = nc_ctx - 1 - cj
            lj = nc_lat - 1 - lj
        return jnp.where(j < nc_ctx, ctx_base + b * nc_ctx + cj, b * nc_lat + lj)

    return row_block, nc_ctx + nc_lat


def _round_robin(stage_generators):
    live = list(stage_generators)
    while live:
        for gen in list(live):
            try:
                next(gen)
            except StopIteration:
                live.remove(gen)


def _ssd_direction(xbc_ref, dt_ref, arow_ref, brow_ref, e_ref, tri_ref, y_ref, st_ref, *, rev, lane0):
    q = SSD_Q
    nch = SSD_STEP // q
    gw = D_SSM // SSM_G
    t_idx = lax.broadcasted_iota(jnp.int32, (q, q), 0)
    s_idx = lax.broadcasted_iota(jnp.int32, (q, q), 1)
    causal = (s_idx >= t_idx) if rev else (s_idx <= t_idx)
    lane = lax.broadcasted_iota(jnp.int32, (q, LANES), 1)
    low_half = lane < SSM_P
    e = e_ref[...]
    tri = tri_ref[...]
    edge = 0 if rev else q - 1
    chunks = [dict(rows=slice(ci * q, (ci + 1) * q))
              for ci in (range(nch - 1, -1, -1) if rev else range(nch))]

    def stage_cumsum(ch):
        rows = ch["rows"]
        dt = jax.nn.softplus(dt_ref[rows, :] + brow_ref[...])
        ch["dt"] = dt
        ch["cum"] = _ldot3(tri, dt * arow_ref[...])

    def stage_expand(ch):
        rows = ch["rows"]
        cum = ch["cum"]
        ch["cum_t"] = cum.T
        cum_e = _rdot3(cum, e)
        ch["cum_e"] = cum_e
        ch["tot_e"] = cum_e[edge:edge + 1, :]
        ch["xdt"] = xbc_ref[rows, 0:D_SSM] * _rdot3(ch["dt"], e)
        ch["bg"] = []
        ch["cg"] = []
        ch["cb"] = []
        for g in range(SSM_G):
            b0 = D_SSM + g * SSM_N
            c0 = D_SSM + SSM_G * SSM_N + g * SSM_N
            bg = xbc_ref[rows, b0:b0 + SSM_N].astype(BF16)
            cg = xbc_ref[rows, c0:c0 + SSM_N].astype(BF16)
            ch["bg"].append(bg)
            ch["cg"].append(cg)
            ch["cb"].append(_dot_nt(cg, bg))

    def stage_local(ch):
        cum, cum_t, cum_e, tot_e, xdt = ch["cum"], ch["cum_t"], ch["cum_e"], ch["tot_e"], ch["xdt"]
        ch["y_diag"] = []
        ch["upd"] = []
        for g in range(SSM_G):
            gl = g * gw
            for hp in range(2):
                pl0 = gl + hp * LANES
                xpair = xdt[:, pl0:pl0 + LANES]
                acc = jnp.zeros((q, LANES), F32)
                for hh in range(2):
                    idx = lane0 + g * 4 + hp * 2 + hh
                    diff = cum[:, idx:idx + 1] - cum_t[idx:idx + 1, :]
                    m = ch["cb"][g] * jnp.exp(jnp.where(causal, diff, MASK_VALUE))
                    xm = jnp.where(low_half if hh == 0 else jnp.logical_not(low_half), xpair, 0.0)
                    acc = acc + jnp.dot(m.astype(BF16), xm.astype(BF16), preferred_element_type=F32)
                ch["y_diag"].append(acc)
            xw = (xdt[:, gl:gl + gw] * jnp.exp(tot_e[:, gl:gl + gw] - cum_e[:, gl:gl + gw])).astype(BF16)
            ch["upd"].append(_dot_tn(ch["bg"][g], xw))
        ch["off_scale"] = jnp.exp(cum_e)
        ch["dec"] = jnp.exp(tot_e)

    def stage_state():
        st = [st_ref[:, g * gw:(g + 1) * gw] for g in range(SSM_G)]
        for ch in chunks:
            y_parts = []
            for g in range(SSM_G):
                gl = g * gw
                y_off = jnp.dot(ch["cg"][g], st[g].astype(BF16), preferred_element_type=F32)
                y_parts.append(jnp.concatenate(ch["y_diag"][2 * g:2 * g + 2], axis=1)
                               + y_off * ch["off_scale"][:, gl:gl + gw])
                st[g] = st[g] * ch["dec"][:, gl:gl + gw] + ch["upd"][g]
            y_ref[ch["rows"], :] = jnp.concatenate(y_parts, axis=1)
        for g in range(SSM_G):
            st_ref[:, g * gw:(g + 1) * gw] = st[g]

    for ch in chunks:
        yield stage_cumsum(ch)
    for ch in chunks:
        yield stage_expand(ch)
    for ch in chunks:
        yield stage_local(ch)
    yield stage_state()


def _ssd_kernel(xf_ref, xb_ref, dtf_ref, dtb_ref, arow_ref, brow_ref, ef_ref, eb_ref, trif_ref, trib_ref,
                yf_ref, yb_ref, stf_ref, stb_ref):
    @pl.when(pl.program_id(1) == 0)
    def _():
        stf_ref[...] = jnp.zeros_like(stf_ref)
        stb_ref[...] = jnp.zeros_like(stb_ref)

    _round_robin([
        _ssd_direction(xf_ref, dtf_ref, arow_ref, brow_ref, ef_ref, trif_ref, yf_ref, stf_ref,
                       rev=False, lane0=0),
        _ssd_direction(xb_ref, dtb_ref, arow_ref, brow_ref, eb_ref, trib_ref, yb_ref, stb_ref,
                       rev=True, lane0=SSM_HEADS)])


def _ssd_call(xbc_c, dt_raw, a_row, b_row, e_f, e_b, tri_f, tri_b, *, n_rows, seq, nb):
    q = SSD_STEP
    blk_f, steps = _scan_block_map(seq=seq, nb=nb, q=q, rev=False)
    blk_b, _ = _scan_block_map(seq=seq, nb=nb, q=q, rev=True)
    const = lambda b, j: (0, 0)
    return pl.pallas_call(
        _ssd_kernel,
        grid=(nb, steps),
        in_specs=[
            pl.BlockSpec((q, CONV_DIM), lambda b, j: (blk_f(b, j), 0)),
            pl.BlockSpec((q, CONV_DIM), lambda b, j: (blk_b(b, j), 0)),
            pl.BlockSpec((q, LANES), lambda b, j: (blk_f(b, j), 0)),
            pl.BlockSpec((q, LANES), lambda b, j: (blk_b(b, j), 0)),
            pl.BlockSpec((1, LANES), const),
            pl.BlockSpec((1, LANES), const),
            pl.BlockSpec((3 * LANES, D_SSM), const),
            pl.BlockSpec((3 * LANES, D_SSM), const),
            pl.BlockSpec((SSD_Q, SSD_Q), const),
            pl.BlockSpec((SSD_Q, SSD_Q), const),
        ],
        out_specs=[pl.BlockSpec((q, D_SSM), lambda b, j: (blk_f(b, j), 0)),
                   pl.BlockSpec((q, D_SSM), lambda b, j: (blk_b(b, j), 0))],
        out_shape=[jax.ShapeDtypeStruct((n_rows, D_SSM), F32)] * 2,
        scratch_shapes=[pltpu.VMEM((SSM_N, D_SSM), F32)] * 2,
        compiler_params=_cparams(("arbitrary", "arbitrary")),
        name="ssd",
    )(xbc_c, xbc_c, dt_raw, dt_raw, a_row, b_row, e_f, e_b, tri_f, tri_b)


def _gla_direction(q_ref, f_ref, v_ref, lb_ref, tri_ref, same_blk, ind_ref, o_ref, p_s, upd_s, oi_s, qkb_s,
                   st_ref, *, rev):
    t = GLA_TILE
    c = GLA_SUB
    nsub = t // c
    lb = lb_ref[...]
    fr = f_ref[...]
    f = lb + (1.0 - lb) * jax.nn.sigmoid(fr)
    logf = jnp.log(jnp.maximum(f, TINY))
    k = (1.0 - lb) * jax.nn.sigmoid(-fr)
    qv = _silu(q_ref[...])
    vv = v_ref[...]
    yield
    brel = _ldot3(tri_ref[...], logf)
    tot = _ldot3(same_blk, logf)
    qkb_s[0] = qv
    qkb_s[1] = k
    qkb_s[2] = brel
    yield

    qt = (qv * jnp.exp(brel)).astype(BF16)
    kt = k * jnp.exp(tot - brel)
    dec_blk = jnp.exp(tot)
    lane = lax.broadcasted_iota(jnp.int32, (t, D_HG), 1)
    head_masks = [(lane >= h * HG_DK) & (lane < (h + 1) * HG_DK) for h in range(HG_HEADS)]
    kxs = [jnp.where(m, kt, 0.0).astype(BF16) for m in head_masks]
    vxs = [jnp.where(m, vv, 0.0).astype(BF16) for m in head_masks]
    for blk in range(nsub):
        r0 = blk * c
        kx = jnp.concatenate([a[r0:r0 + c, :] for a in kxs], axis=0)
        vx = jnp.concatenate([a[r0:r0 + c, :] for a in vxs], axis=0)
        upd_s[blk] = _dot_tn(vx, kx)
        if blk % 4 == 3:
            yield

    mid = c // 2
    bref = jnp.concatenate([jnp.broadcast_to(brel[b * c + mid:b * c + mid + 1, :], (c, D_HG))
                            for b in range(nsub)], axis=0)
    dev = brel - bref
    worst = jnp.max(jnp.max(jnp.abs(dev), axis=1, keepdims=True), axis=0, keepdims=True)
    safe_v = worst <= GLA_SAFE_EXPONENT
    safe = worst[0, 0] <= GLA_SAFE_EXPONENT
    qh = qv * jnp.exp(dev)
    kh = (k * jnp.exp(-dev)).astype(BF16)
    qx = jnp.concatenate([jnp.where(m, qh, 0.0).astype(BF16) for m in head_masks], axis=0)
    yield
    sc = _dot_nt(qx, kh)
    visible = tri_ref[...].astype(F32) > 0.5
    pm = jnp.concatenate([jnp.where(visible, sc[h * t:(h + 1) * t, :], 0.0).astype(BF16)
                          for h in range(HG_HEADS)], axis=0)
    yield
    oh = jnp.dot(pm, vv.astype(BF16), preferred_element_type=F32)
    o_fast = jnp.where(head_masks[0], oh[0:t, :], 0.0)
    for h in range(1, HG_HEADS):
        o_fast = o_fast + jnp.where(head_masks[h], oh[h * t:(h + 1) * t, :], 0.0)
    oi_s[...] = jnp.where(safe_v, o_fast, 0.0)
    yield

    @pl.when(jnp.logical_not(safe))
    def _():
        _gla_intra_pairwise(qkb_s, v_ref, ind_ref, p_s, oi_s, rev=rev)

    yield

    st = st_ref[...]
    o_inter = [None] * nsub
    for blk in (range(nsub - 1, -1, -1) if rev else range(nsub)):
        r0 = blk * c
        o_inter[blk] = _dot_nt(qt[r0:r0 + c, :], st.astype(BF16))
        st = st * dec_blk[r0:r0 + 1, :] + upd_s[blk]
        yield
    st_ref[...] = st
    o_ref[...] = oi_s[...] + jnp.concatenate(o_inter, axis=0)


def _gla_intra_pairwise(qkb_s, v_ref, ind_ref, p_s, oi_s, *, rev):
    t = GLA_TILE
    c = GLA_SUB
    nsub = t // c
    sub = SUBLANES
    qv, k, brel = qkb_s[0], qkb_s[1], qkb_s[2]
    vv = v_ref[...]
    t_idx = lax.broadcasted_iota(jnp.int32, (sub, D_HG), 0)
    pieces = [(s, g) for s in range(c) for g in range(c // sub)
              if (g <= s // sub if rev else g >= s // sub)]
    rows_per_blk = len(pieces) * sub

    brel2 = brel * LOG2E
    for blk in range(nsub):
        r0 = blk * c
        qb = qv[r0:r0 + c, :]
        kb = k[r0:r0 + c, :]
        bb = brel2[r0:r0 + c, :]
        for u in range(0, len(pieces), 2):
            rows = []
            for s, g in pieces[u:u + 2]:
                tt = t_idx + g * sub
                keep = (tt <= s) if rev else (tt >= s)
                dec = jnp.exp2(jnp.where(keep, bb[g * sub:(g + 1) * sub, :] - bb[s:s + 1, :], MASK_VALUE))
                rows.append(qb[g * sub:(g + 1) * sub, :] * kb[s:s + 1, :] * dec)
            po = blk * rows_per_blk + u * sub
            p_s[po:po + 2 * sub, :] = jnp.concatenate(rows, axis=0).astype(BF16)
    r = jnp.dot(p_s[...], ind_ref[...], preferred_element_type=F32)
    for blk in range(nsub):
        r0 = blk * c
        vb = vv[r0:r0 + c, :]
        accs = [jnp.zeros((sub, D_HG), F32) for _ in range(c // sub)]
        for n, (s, g) in enumerate(pieces):
            po = blk * rows_per_blk + n * sub
            accs[g] = accs[g] + r[po:po + sub, :] * vb[s:s + 1, :]
        oi_s[r0:r0 + c, :] = jnp.concatenate(accs, axis=0)


def _gla_kernel(qf_ref, ff_ref, vf_ref, qb_ref, fb_ref, vb_ref, lbf_ref, lbb_ref, trif_ref, trib_ref,
                ind_ref, of_ref, ob_ref, pf_s, pb_s, updf_s, updb_s, oif_s, oib_s, qkbf_s, qkbb_s,
                stf_ref, stb_ref):
    @pl.when(pl.program_id(1) == 0)
    def _():
        stf_ref[...] = jnp.zeros_like(stf_ref)
        stb_ref[...] = jnp.zeros_like(stb_ref)

    same_blk = jnp.maximum(trif_ref[...], trib_ref[...])
    _round_robin([
        _gla_direction(qf_ref, ff_ref, vf_ref, lbf_ref, trif_ref, same_blk, ind_ref, of_ref,
                       pf_s, updf_s, oif_s, qkbf_s, stf_ref, rev=False),
        _gla_direction(qb_ref, fb_ref, vb_ref, lbb_ref, trib_ref, same_blk, ind_ref, ob_ref,
                       pb_s, updb_s, oib_s, qkbb_s, stb_ref, rev=True)])


def _gla_call(hg, lb_f, lb_b, tri_f, tri_b, ind, *, n_rows, seq, nb):
    t = GLA_TILE
    blk_f, steps = _scan_block_map(seq=seq, nb=nb, q=t, rev=False)
    blk_b, _ = _scan_block_map(seq=seq, nb=nb, q=t, rev=True)
    const = lambda b, j: (0, 0)
    col = lambda blk, cidx: pl.BlockSpec((t, D_HG), lambda b, j: (blk(b, j), cidx))
    return pl.pallas_call(
        _gla_kernel,
        grid=(nb, steps),
        in_specs=[
            col(blk_f, 0), col(blk_f, 1), col(blk_f, 3),
            col(blk_b, 0), col(blk_b, 2), col(blk_b, 3),
            pl.BlockSpec((1, D_HG), const),
            pl.BlockSpec((1, D_HG), const),
            pl.BlockSpec((t, t), const),
            pl.BlockSpec((t, t), const),
            pl.BlockSpec((D_HG, D_HG), const),
        ],
        out_specs=[col(blk_f, 0), col(blk_b, 0)],
        out_shape=[jax.ShapeDtypeStruct((n_rows, D_HG), F32)] * 2,
        scratch_shapes=(
            [pltpu.VMEM((GLA_P_ROWS, D_HG), BF16)] * 2
            + [pltpu.VMEM((GLA_TILE // GLA_SUB, D_HG, D_HG), F32)] * 2
            + [pltpu.VMEM((GLA_TILE, D_HG), F32)] * 2
            + [pltpu.VMEM((3, GLA_TILE, D_HG), F32)] * 2
            + [pltpu.VMEM((D_HG, D_HG), F32)] * 2),
        compiler_params=_cparams(("arbitrary", "arbitrary")),
        name="gla",
    )(hg, hg, hg, hg, hg, hg, lb_f, lb_b, tri_f, tri_b, ind)


def _expand_heads(x, masks):
    return jnp.concatenate([jnp.where(m, x, jnp.zeros_like(x)) for m in masks], axis=0)


def _collapse_heads(o, masks, t):
    acc = jnp.where(masks[0], o[0:t, :], 0.0)
    for h in range(1, NA_HEADS):
        acc = acc + jnp.where(masks[h], o[h * t:(h + 1) * t, :], 0.0)
    return acc


def _na_finish(o, nw):
    ms = jnp.mean(o * o, axis=-1, keepdims=True)
    return (o * lax.rsqrt(ms + EPS)) * nw


def _na_kernel(q_ref, k_ref, v_ref, kc_ref, vc_ref, bias_ref, nw_ref, o_ref, *, n_grid_rows):
    w = GRID_W
    i = pl.program_id(1)
    lane = lax.broadcasted_iota(jnp.int32, (w, D_NA), 1)
    masks = [(lane >= h * NA_DH) & (lane < (h + 1) * NA_DH) for h in range(NA_HEADS)]
    kc = kc_ref[...]
    vc = vc_ref[...]
    nw = nw_ref[...]
    scale = NA_DH ** -0.5

    def body(jr, carry):
        r = i * NA_ROWS + jr
        r0 = jnp.clip(r - WIN_R // 2, 0, n_grid_rows - WIN_R)
        var = r - r0
        k0 = pl.multiple_of(r0 * w, w)
        q0 = pl.multiple_of(jr * w, w)
        qx = _expand_heads(q_ref[pl.ds(q0, w), :] * scale, masks)
        kw = k_ref[pl.ds(k0, WIN_R * w), :]
        vw = v_ref[pl.ds(k0, WIN_R * w), :]
        s_loc = _dot_nt(qx, kw) + bias_ref[var]
        s_ctx = _dot_nt(qx, kc)
        m = jnp.maximum(jnp.max(s_loc, axis=-1, keepdims=True), jnp.max(s_ctx, axis=-1, keepdims=True))
        p_loc = jnp.exp(s_loc - m)
        p_ctx = jnp.exp(s_ctx - m)
        den = jnp.sum(p_loc, axis=-1, keepdims=True) + jnp.sum(p_ctx, axis=-1, keepdims=True)
        o = (jnp.dot(p_loc.astype(BF16), vw, preferred_element_type=F32)
             + jnp.dot(p_ctx.astype(BF16), vc, preferred_element_type=F32)) / den
        o_ref[pl.ds(q0, w), :] = _na_finish(_collapse_heads(o, masks, w), nw)
        return carry

    lax.fori_loop(0, NA_ROWS, body, 0, unroll=True)


def _na_call(na, bias, nw, *, seq, nb):
    rows = seq // GRID_W
    tq = NA_ROWS * GRID_W
    qpb = seq // tq
    ctx_blk = nb * seq // CTX_LEN
    kern = functools.partial(_na_kernel, n_grid_rows=rows)
    return pl.pallas_call(
        kern,
        grid=(nb, qpb),
        in_specs=[
            pl.BlockSpec((tq, D_NA), lambda b, i: (b * qpb + i, 0)),
            pl.BlockSpec((seq, D_NA), lambda b, i: (b, 1)),
            pl.BlockSpec((seq, D_NA), lambda b, i: (b, 2)),
            pl.BlockSpec((CTX_LEN, D_NA), lambda b, i: (ctx_blk + b, 1)),
            pl.BlockSpec((CTX_LEN, D_NA), lambda b, i: (ctx_blk + b, 2)),
            pl.BlockSpec((WIN_R, NA_HEADS * GRID_W, WIN_R * GRID_W), lambda b, i: (0, 0, 0)),
            pl.BlockSpec((1, D_NA), lambda b, i: (0, 0)),
        ],
        out_specs=pl.BlockSpec((tq, D_NA), lambda b, i: (b * qpb + i, 0)),
        out_shape=jax.ShapeDtypeStruct((nb * seq, D_NA), F32),
        compiler_params=_cparams(("parallel", "arbitrary")),
        name="natten",
    )(na, na, na, na, na, bias, nw)


def _ctxattn_kernel(q_ref, k_ref, v_ref, nw_ref, o_ref):
    t = CTX_LEN
    lane = lax.broadcasted_iota(jnp.int32, (t, D_NA), 1)
    masks = [(lane >= h * NA_DH) & (lane < (h + 1) * NA_DH) for h in range(NA_HEADS)]
    qx = _expand_heads(q_ref[...] * (NA_DH ** -0.5), masks)
    s = _dot_nt(qx, k_ref[...])
    m = jnp.max(s, axis=-1, keepdims=True)
    p = jnp.exp(s - m)
    den = jnp.sum(p, axis=-1, keepdims=True)
    o = jnp.dot(p.astype(BF16), v_ref[...], preferred_element_type=F32) / den
    o_ref[...] = _na_finish(_collapse_heads(o, masks, t), nw_ref[...])


def _ctxattn_call(na, nw, *, seq, nb):
    ctx_blk = nb * seq // CTX_LEN
    return pl.pallas_call(
        _ctxattn_kernel,
        grid=(nb,),
        in_specs=[
            pl.BlockSpec((CTX_LEN, D_NA), lambda b: (ctx_blk + b, 0)),
            pl.BlockSpec((CTX_LEN, D_NA), lambda b: (ctx_blk + b, 1)),
            pl.BlockSpec((CTX_LEN, D_NA), lambda b: (ctx_blk + b, 2)),
            pl.BlockSpec((1, D_NA), lambda b: (0, 0)),
        ],
        out_specs=pl.BlockSpec((CTX_LEN, D_NA), lambda b: (b, 0)),
        out_shape=jax.ShapeDtypeStruct((nb * CTX_LEN, D_NA), F32),
        compiler_params=_cparams(("parallel",)),
        name="ctxattn",
    )(na, na, na, nw)


def _mixed_residual(x, mod_ref, of_ref, ob_ref, g_ref, na_ref, yf_ref, yb_ref, xs_ref, z_ref,
                    hgw_ref, hm_ref, dsk_ref, sw_ref, wo_ref, *, rows):
    hm = hm_ref[...]
    o = of_ref[rows, :] + ob_ref[rows, :]
    sq = o * o
    hi = sq.astype(BF16)
    lo = (sq - hi.astype(F32)).astype(BF16)
    ms = (jnp.dot(hi, hm, preferred_element_type=F32)
          + jnp.dot(lo, hm, preferred_element_type=F32)) * (1.0 / HG_DK)
    hg = (o * lax.rsqrt(ms + EPS)) * hgw_ref[...] * _silu(g_ref[rows, :])
    ys = (yf_ref[rows, :] + yb_ref[rows, :] + dsk_ref[...] * xs_ref[rows, :]) * _silu(z_ref[rows, :])
    ms2 = jnp.mean(ys * ys, axis=-1, keepdims=True)
    ssm = (ys * lax.rsqrt(ms2 + EPS)) * sw_ref[...]
    mix = (jnp.dot(hg.astype(BF16), wo_ref[0:D_HG, :], preferred_element_type=F32)
           + jnp.dot(na_ref[rows, :].astype(BF16), wo_ref[D_HG:D_HG + D_NA, :], preferred_element_type=F32)
           + jnp.dot(ssm.astype(BF16), wo_ref[D_HG + D_NA:, :], preferred_element_type=F32))
    return x + mod_ref[5:6, :] * mix


def _mix_specs(tm):
    row = lambda i: (i, 0)
    const = lambda i: (0, 0)
    return [
        pl.BlockSpec((tm, D_HG), row),
        pl.BlockSpec((tm, D_HG), row),
        pl.BlockSpec((tm, D_HG), lambda i: (i, 4)),
        pl.BlockSpec((tm, D_NA), row),
        pl.BlockSpec((tm, D_SSM), row),
        pl.BlockSpec((tm, D_SSM), row),
        pl.BlockSpec((tm, D_SSM), row),
        pl.BlockSpec((tm, D_SSM), row),
        pl.BlockSpec((1, D_HG), const),
        pl.BlockSpec((D_HG, D_HG), const),
        pl.BlockSpec((1, D_SSM), const),
        pl.BlockSpec((1, D_SSM), const),
        pl.BlockSpec((D_MODEL, D_MODEL), const, pipeline_mode=pl.Buffered(1)),
    ]


def _na_bias_table(rpb):
    w = GRID_W
    ndr = 2 * WIN_R - 1
    ndc = 2 * WIN_C - 1
    col = np.arange(w)
    c0 = np.clip(col - WIN_C // 2, 0, w - WIN_C)
    col_in = (col[None, :] >= c0[:, None]) & (col[None, :] < c0[:, None] + WIN_C)
    dc = np.clip(col[None, :] - col[:, None], -(WIN_C - 1), WIN_C - 1) + (WIN_C - 1)
    onehot = (dc.reshape(1, -1) == np.arange(ndc)[:, None]).astype(np.float32)
    t = jnp.dot(rpb.reshape(NA_HEADS * ndr, ndc).astype(F32), onehot,
                precision=lax.Precision.HIGHEST).reshape(NA_HEADS, ndr, w, w)
    t = jnp.where(col_in[None, None], t, MASK_VALUE)
    b = jnp.stack([t[:, WIN_R - 1 - var:2 * WIN_R - 1 - var] for var in range(WIN_R)], axis=0)
    b = jnp.transpose(b, (0, 1, 3, 2, 4))
    return b.reshape(WIN_R, NA_HEADS * w, WIN_R * w)


def _block_tri(n, c, rev):
    r = jnp.arange(n)
    same = (r[:, None] // c) == (r[None, :] // c)
    tri = (r[None, :] >= r[:, None]) if rev else (r[None, :] <= r[:, None])
    return (same & tri).astype(BF16)


def _head_block_ones(n, hd):
    r = jnp.arange(n)
    return ((r[:, None] // hd) == (r[None, :] // hd)).astype(BF16)


def _ssd_expand(lane0):
    r = jnp.arange(LANES)[:, None]
    cidx = jnp.arange(D_SSM)[None, :]
    e = (r == lane0 + cidx // SSM_P).astype(BF16)
    return jnp.concatenate([e, e, e], axis=0)


def _lane_row(vals):
    return jnp.zeros((1, LANES), F32).at[0, :vals.shape[0]].set(vals.astype(F32))


def kernel(x, c, ctx, c_ctx, w_mod, b_mod, norm_ffn1, ffn1_w13, ffn1_w2, norm_mix, w_in,
           hg_lower_bounds, hg_norm, na_rpb, na_norm, ssm_conv_w, ssm_conv_b, ssm_a_log,
           ssm_dt_bias, ssm_d, ssm_norm, w_out, norm_ffn2, ffn2_w13, ffn2_w2, final_norm):
    nb, seq, d = x.shape
    depth = w_mod.shape[0]
    assert d == D_MODEL and ctx.shape[1] == CTX_LEN and nb + 1 <= MOD_ROWS
    assert seq % 512 == 0 and seq // GRID_W >= WIN_R
    n_lat = nb * seq
    n_all = n_lat + nb * CTX_LEN

    lb_soft = jax.nn.softmax(hg_lower_bounds.astype(F32), axis=1)
    lower_bounds = jnp.cumsum(lb_soft, axis=1) - lb_soft[:, :1]

    c_rows = jnp.zeros((MOD_ROWS, d), F32).at[:nb].set(c).at[nb].set(c_ctx)
    mods = _mods_call(c_rows, w_mod, b_mod).reshape(depth, MOD_ROWS, N_MOD, d)

    x_all = x.reshape(n_lat, d)
    x_ctx = ctx.reshape(nb * CTX_LEN, d)

    tri_f = _block_tri(GLA_TILE, GLA_SUB, False)
    tri_b = _block_tri(GLA_TILE, GLA_SUB, True)
    ind = _head_block_ones(D_HG, HG_DK)
    ssd_tri_f = _block_tri(SSD_Q, SSD_Q, False)
    ssd_tri_b = _block_tri(SSD_Q, SSD_Q, True)
    e_f = _ssd_expand(0)
    e_b = _ssd_expand(SSM_HEADS)
    one_row = lambda v: v.reshape(1, -1).astype(F32)
    common = dict(seq=seq, nb=nb)

    for layer in range(depth):
        last = layer == depth - 1
        mods_l = mods[layer]
        w13_1 = _cast_call(ffn1_w13, layer)
        w2_1 = _cast_call(ffn1_w2, layer)
        w13_2 = _cast_call(ffn2_w13, layer)
        w2_2 = _cast_call(ffn2_w2, layer)
        w_in_b = _cast_call(w_in, layer, out_cols=D_IN_PAD)
        wo_b = _cast_call(w_out, layer)
        fw = one_row(final_norm)

        x_all = _ffn_call(x_all, mods_l, one_row(norm_ffn1[layer]), w13_1, w2_1, fw,
                          n_rows=n_all, ks=(0, 1, 2), final=False, x_ctx=x_ctx if layer == 0 else None,
                          **common)
        hg, na, z, xbc_c, dt_raw = _inproj_call(x_all, mods_l, one_row(norm_mix[layer]), w_in_b,
                                                ssm_conv_w[layer].astype(F32), one_row(ssm_conv_b[layer]),
                                                n_rows=n_all, **common)

        o_f, o_b = _gla_call(hg, one_row(lower_bounds[0, layer]), one_row(lower_bounds[1, layer]),
                             tri_f, tri_b, ind, n_rows=n_all, **common)

        bias = _na_bias_table(na_rpb[layer])
        nw = one_row(na_norm[layer])
        y_na = _na_call(na, bias, nw, **common)
        if not last:
            y_na = jnp.concatenate([y_na, _ctxattn_call(na, nw, **common)], axis=0)

        a_neg = -jnp.exp(ssm_a_log[layer].astype(F32))
        y_f, y_b = _ssd_call(xbc_c, dt_raw, _lane_row(a_neg.reshape(-1)),
                             _lane_row(ssm_dt_bias[layer].reshape(-1)),
                             e_f, e_b, ssd_tri_f, ssd_tri_b, n_rows=n_all, **common)

        n_out = n_lat if last else n_all
        dsk = jnp.repeat(ssm_d[layer].astype(F32), SSM_P).reshape(1, D_SSM)
        mix = (o_f, o_b, hg, y_na, y_f, y_b, xbc_c, z,
               one_row(hg_norm[layer]), ind, dsk, one_row(ssm_norm[layer]), wo_b)
        assert len(mix) == N_MIX_OPERANDS
        x_all = _ffn_call(x_all, mods_l, one_row(norm_ffn2[layer]), w13_2, w2_2, fw,
                          n_rows=n_out, ks=(6, 7, 8), final=last, mix=mix, **common)

    return x_all.reshape(nb, seq, d)
```

```python
import functools
import math

import jax
import jax.numpy as jnp
import numpy as np
from jax import lax
from jax.experimental import pallas as pl
from jax.experimental.pallas import tpu as pltpu

F32 = jnp.float32
BF16 = jnp.bfloat16

D_MODEL = 1024
GRID_W = 64
CTX_LEN = 256
EPS = 1e-6
N_MOD = 9
MASK_VALUE = -1e30
TINY = 1e-20
LOG2E = 1.4426950408889634

D_HG = 256
HG_HEADS = 4
HG_DK = 64
D_NA = 256
NA_HEADS = 4
NA_DH = 64
WIN_R = 8
WIN_C = 16
D_SSM = 512
SSM_HEADS = 8
SSM_P = 64
SSM_N = 128
SSM_G = 2
CONV_W = 5
CONV_DIM = D_SSM + 2 * SSM_G * SSM_N
D_FF = 2816
D_IN_PAD = 3712
LANES = 128
SUBLANES = 8
MXU_COLS = 256
MOD_ROWS = 8

GLA_TILE = 256
GLA_SUB = 32
GLA_SAFE_EXPONENT = 60.0
_GLA_GROUPS = GLA_SUB // SUBLANES
GLA_P_ROWS = (GLA_TILE // GLA_SUB) * SUBLANES * SUBLANES * _GLA_GROUPS * (_GLA_GROUPS + 1) // 2
SSD_Q = 128
SSD_STEP = 256
NA_ROWS = 8
NA_INTERLEAVE = 4

VMEM_LIMIT = 56 * 1024 * 1024


def _cparams(sem):
    return pltpu.CompilerParams(dimension_semantics=sem, vmem_limit_bytes=VMEM_LIMIT)


def _silu(x):
    return x * jax.nn.sigmoid(x)


def _split3(x):
    hi = x.astype(BF16)
    r1 = x - hi.astype(F32)
    mid = r1.astype(BF16)
    lo = (r1 - mid.astype(F32)).astype(BF16)
    return hi, mid, lo


def _ldot3(a_bf16, x):
    hi, mid, lo = _split3(x)
    n = x.shape[1]
    if n % MXU_COLS == 0:
        return (jnp.dot(a_bf16, hi, preferred_element_type=F32)
                + jnp.dot(a_bf16, mid, preferred_element_type=F32)
                + jnp.dot(a_bf16, lo, preferred_element_type=F32))
    r = jnp.dot(a_bf16, jnp.concatenate([hi, mid, lo], axis=1), preferred_element_type=F32)
    return r[:, 0:n] + r[:, n:2 * n] + r[:, 2 * n:3 * n]


def _rdot3(x, e3_bf16):
    return jnp.dot(jnp.concatenate(_split3(x), axis=1), e3_bf16, preferred_element_type=F32)


def _dot_nt(a, b):
    return lax.dot_general(a, b, (((1,), (1,)), ((), ())), preferred_element_type=F32)


def _dot_tn(a, b):
    return lax.dot_general(a, b, (((0,), (0,)), ((), ())), preferred_element_type=F32)


def _modulated(x, nw, shift, scale):
    ms = jnp.mean(x * x, axis=-1, keepdims=True)
    y = x * lax.rsqrt(ms + EPS)
    return (y * nw) * (1.0 + scale) + shift


def _cast_kernel(w_ref, o_ref):
    cols = w_ref.shape[1]
    if o_ref.shape[1] == cols:
        o_ref[...] = w_ref[...].astype(BF16)
    else:
        o_ref[:, 0:cols] = w_ref[...].astype(BF16)
        o_ref[:, cols:] = jnp.zeros((o_ref.shape[0], o_ref.shape[1] - cols), BF16)


def _cast_call(w, layer, out_cols=None, tr=256):
    _, rows, cols = w.shape
    out_cols = out_cols or cols
    return pl.pallas_call(
        _cast_kernel,
        grid=(rows // tr,),
        in_specs=[pl.BlockSpec((None, tr, cols), lambda i: (layer, i, 0))],
        out_specs=pl.BlockSpec((tr, out_cols), lambda i: (i, 0)),
        out_shape=jax.ShapeDtypeStruct((rows, out_cols), BF16),
        compiler_params=_cparams(("parallel",)),
        name="cast",
    )(w)


def _mods_kernel(c_ref, w_ref, b_ref, o_ref):
    sc = _silu(c_ref[...]).astype(BF16)
    o_ref[...] = jnp.dot(sc, w_ref[...].astype(BF16), preferred_element_type=F32) + b_ref[...]


def _mods_call(c_rows, w_mod, b_mod):
    depth = w_mod.shape[0]
    tn = 1152
    n = N_MOD * D_MODEL
    return pl.pallas_call(
        _mods_kernel,
        grid=(depth, n // tn),
        in_specs=[
            pl.BlockSpec((MOD_ROWS, D_MODEL), lambda l, j: (0, 0)),
            pl.BlockSpec((None, D_MODEL, tn), lambda l, j: (l, 0, j)),
            pl.BlockSpec((None, 1, tn), lambda l, j: (l, 0, j)),
        ],
        out_specs=pl.BlockSpec((None, MOD_ROWS, tn), lambda l, j: (l, 0, j)),
        out_shape=jax.ShapeDtypeStruct((depth, MOD_ROWS, n), F32),
        compiler_params=_cparams(("parallel", "parallel")),
        name="mods",
    )(c_rows, w_mod, b_mod.reshape(depth, 1, n))


def _mod_spec(tm, seq, nb):
    tpb = seq // tm
    return pl.BlockSpec((None, N_MOD, D_MODEL), lambda i: (jnp.minimum(i // tpb, nb), 0, 0))


MIXFFN_CHUNKS = 2
N_MIX_OPERANDS = 13


def _ffn_kernel(*refs, k_shift, k_scale, k_gate, final, n_lat_tiles, mixed):
    if mixed:
        x_ref, mod_ref = refs[0:2]
        mix_refs = refs[2:2 + N_MIX_OPERANDS]
        nw_ref, wu_ref, wg_ref, w2_ref, fw_ref, o_ref, act_ref = refs[2 + N_MIX_OPERANDS:]
        x = None
    elif n_lat_tiles is None:
        x_ref, mod_ref, nw_ref, wu_ref, wg_ref, w2_ref, fw_ref, o_ref, act_ref = refs
        x = x_ref[...]
    else:
        x_ref, xc_ref, mod_ref, nw_ref, wu_ref, wg_ref, w2_ref, fw_ref, o_ref, act_ref = refs
        x = jnp.where(pl.program_id(0) >= n_lat_tiles, xc_ref[...], x_ref[...])
    tm = o_ref.shape[0]
    cw = MXU_COLS
    chunks = MIXFFN_CHUNKS if mixed else 1
    rc = tm // chunks
    for n in range(chunks):
        rows = slice(n * rc, (n + 1) * rc)
        xr = _mixed_residual(x_ref[rows, :], mod_ref, *mix_refs, rows=rows) if mixed else x
        h = _modulated(xr, nw_ref[...], mod_ref[k_shift:k_shift + 1, :],
                       mod_ref[k_scale:k_scale + 1, :]).astype(BF16)
        for c in range(D_FF // cw):
            u = jnp.dot(h, wu_ref[:, c * cw:(c + 1) * cw], preferred_element_type=F32)
            g = jnp.dot(h, wg_ref[:, c * cw:(c + 1) * cw], preferred_element_type=F32)
            act_ref[rows, c * cw:(c + 1) * cw] = (_silu(g) * u).astype(BF16)
        y = jnp.dot(act_ref[rows, :], w2_ref[...], preferred_element_type=F32)
        out = xr + (0.5 * mod_ref[k_gate:k_gate + 1, :]) * y
        if final:
            ms = jnp.mean(out * out, axis=-1, keepdims=True)
            out = (out * lax.rsqrt(ms + EPS)) * fw_ref[...]
        o_ref[rows, :] = out


def _ffn_call(x_all, mods_l, nw, w13b, w2b, fw, *, n_rows, seq, nb, ks, final, x_ctx=None, mix=None, tm=512):
    mod_spec = _mod_spec(tm, seq, nb)
    if mix is not None:
        n_lat_tiles = None
        lead_args = (x_all, mods_l) + tuple(mix)
        lead_specs = [pl.BlockSpec((tm, D_MODEL), lambda i: (i, 0)), mod_spec] + _mix_specs(tm)
    elif x_ctx is None:
        n_lat_tiles = None
        lead_args = (x_all, mods_l)
        lead_specs = [pl.BlockSpec((tm, D_MODEL), lambda i: (i, 0)), mod_spec]
    else:
        n_lat_tiles = nb * seq // tm
        lead_args = (x_all, x_ctx, mods_l)
        lead_specs = [pl.BlockSpec((tm, D_MODEL), lambda i: (jnp.minimum(i, n_lat_tiles - 1), 0)),
                      pl.BlockSpec((tm, D_MODEL), lambda i: (jnp.maximum(i - n_lat_tiles, 0), 0)), mod_spec]
    kern = functools.partial(_ffn_kernel, k_shift=ks[0], k_scale=ks[1], k_gate=ks[2], final=final,
                             n_lat_tiles=n_lat_tiles, mixed=mix is not None)
    return pl.pallas_call(
        kern,
        grid=(n_rows // tm,),
        in_specs=lead_specs + [
            pl.BlockSpec((1, D_MODEL), lambda i: (0, 0)),
            pl.BlockSpec((D_MODEL, D_FF), lambda i: (0, 0), pipeline_mode=pl.Buffered(1)),
            pl.BlockSpec((D_MODEL, D_FF), lambda i: (0, 1), pipeline_mode=pl.Buffered(1)),
            pl.BlockSpec((D_FF, D_MODEL), lambda i: (0, 0), pipeline_mode=pl.Buffered(1)),
            pl.BlockSpec((1, D_MODEL), lambda i: (0, 0)),
        ],
        out_specs=pl.BlockSpec((tm, D_MODEL), lambda i: (i, 0)),
        out_shape=jax.ShapeDtypeStruct((n_rows, D_MODEL), F32),
        scratch_shapes=[pltpu.VMEM((tm, D_FF), BF16)],
        compiler_params=_cparams(("parallel",)),
        name="mixffn" if mix is not None else "ffn",
    )(*lead_args, nw, w13b, w13b, w2b, fw)


_IN_SPLITS = ((0, 1280), (1280, 2048), (2048, 2560), (2560, 3584), (3584, 3712))


_XBC_COLS = _IN_SPLITS[3]
HALO = 8


def _inproj_kernel(x_ref, xp_ref, xn_ref, mod_ref, nw_ref, w_ref, cw_ref, cb_ref,
                   hg_ref, na_ref, z_ref, xbc_ref, dt_ref, ext_ref, *, n_lat_tiles, tpb):
    tm = x_ref.shape[0]
    i = pl.program_id(0)
    is_ctx = i >= n_lat_tiles
    first = jnp.logical_or(is_ctx, (i % tpb) == 0)
    last = jnp.logical_or(is_ctx, (i % tpb) == tpb - 1)
    nw = nw_ref[...]
    shift = mod_ref[3:4, :]
    scale = mod_ref[4:5, :]
    h = _modulated(x_ref[...], nw, shift, scale).astype(BF16)
    x_halo = jnp.concatenate([xp_ref[...], xn_ref[...]], axis=0)
    h_halo = _modulated(x_halo, nw, shift, scale).astype(BF16)

    a, b = _XBC_COLS
    wx = w_ref[:, a:b]
    halo = jnp.dot(h_halo, wx, preferred_element_type=F32)
    ext_ref[0:HALO, :] = jnp.where(first, 0.0, halo[0:HALO, :])
    ext_ref[HALO:HALO + tm, :] = jnp.dot(h, wx, preferred_element_type=F32)
    ext_ref[HALO + tm:2 * HALO + tm, :] = jnp.where(last, 0.0, halo[HALO:2 * HALO, :])
    pos = lax.broadcasted_iota(jnp.int32, (tm, 1), 0) % CTX_LEN
    acc = jnp.zeros((tm, CONV_DIM), F32) + cb_ref[...]
    for j in range(CONV_W):
        d = j - CONV_W // 2
        tap = ext_ref[HALO + d:HALO + d + tm, :] * cw_ref[j:j + 1, :]
        if d != 0:
            crosses = jnp.logical_and(is_ctx, jnp.logical_or(pos + d < 0, pos + d >= CTX_LEN))
            tap = jnp.where(crosses, 0.0, tap)
        acc = acc + tap
    xbc_ref[...] = _silu(acc)

    for (a, b), o_ref in zip(_IN_SPLITS, (hg_ref, na_ref, z_ref, None, dt_ref)):
        if o_ref is not None:
            o_ref[...] = jnp.dot(h, w_ref[:, a:b], preferred_element_type=F32).astype(o_ref.dtype)


def _inproj_call(x_all, mods_l, nw, w_in_b, conv_w, conv_b, *, n_rows, seq, nb, tm=512):
    widths = [b - a for a, b in _IN_SPLITS]
    dtypes = [F32, BF16, F32, F32, F32]
    r8 = tm // HALO
    last8 = n_rows // HALO - 1
    kern = functools.partial(_inproj_kernel, n_lat_tiles=nb * seq // tm, tpb=seq // tm)
    return pl.pallas_call(
        kern,
        grid=(n_rows // tm,),
        in_specs=[
            pl.BlockSpec((tm, D_MODEL), lambda i: (i, 0)),
            pl.BlockSpec((HALO, D_MODEL), lambda i: (jnp.maximum(i * r8 - 1, 0), 0)),
            pl.BlockSpec((HALO, D_MODEL), lambda i: (jnp.minimum((i + 1) * r8, last8), 0)),
            _mod_spec(tm, seq, nb),
            pl.BlockSpec((1, D_MODEL), lambda i: (0, 0)),
            pl.BlockSpec((D_MODEL, D_IN_PAD), lambda i: (0, 0), pipeline_mode=pl.Buffered(1)),
            pl.BlockSpec((CONV_W, CONV_DIM), lambda i: (0, 0)),
            pl.BlockSpec((1, CONV_DIM), lambda i: (0, 0)),
        ],
        out_specs=[pl.BlockSpec((tm, w), lambda i: (i, 0)) for w in widths],
        out_shape=[jax.ShapeDtypeStruct((n_rows, w), dt) for w, dt in zip(widths, dtypes)],
        scratch_shapes=[pltpu.VMEM((tm + 2 * HALO, CONV_DIM), F32)],
        compiler_params=_cparams(("parallel",)),
        name="inproj",
    )(x_all, x_all, x_all, mods_l, nw, w_in_b, conv_w, conv_b)


def _scan_block_map(*, seq, nb, q, rev):
    nc_ctx = CTX_LEN // q
    nc_lat = seq // q
    ctx_base = nb * seq // q

    def row_block(b, j):
        cj = j
        lj = j - nc_ctx
        if rev:
            cj = nc_ctx - 1 - cj
            lj = nc_lat - 1 - lj
        return jnp.where(j < nc_ctx, ctx_base + b * nc_ctx + cj, b * nc_lat + lj)

    return row_block, nc_ctx + nc_lat


def _round_robin(stage_generators):
    live = list(stage_generators)
    while live:
        for gen in list(live):
            try:
                next(gen)
            except StopIteration:
                live.remove(gen)


def _ssd_direction(xbc_ref, dt_ref, arow_ref, brow_ref, e_ref, tri_ref, y_ref, st_ref, *, rev, lane0):
    q = SSD_Q
    nch = SSD_STEP // q
    gw = D_SSM // SSM_G
    t_idx = lax.broadcasted_iota(jnp.int32, (q, q), 0)
    s_idx = lax.broadcasted_iota(jnp.int32, (q, q), 1)
    causal = (s_idx >= t_idx) if rev else (s_idx <= t_idx)
    lane = lax.broadcasted_iota(jnp.int32, (q, LANES), 1)
    low_half = lane < SSM_P
    e = e_ref[...]
    tri = tri_ref[...]
    edge = 0 if rev else q - 1
    chunks = [dict(rows=slice(ci * q, (ci + 1) * q))
              for ci in (range(nch - 1, -1, -1) if rev else range(nch))]

    def stage_cumsum(ch):
        rows = ch["rows"]
        dt = jax.nn.softplus(dt_ref[rows, :] + brow_ref[...])
        ch["dt"] = dt
        ch["cum"] = _ldot3(tri, dt * arow_ref[...])

    def stage_expand(ch):
        rows = ch["rows"]
        cum = ch["cum"]
        ch["cum_t"] = cum.T
        cum_e = _rdot3(cum, e)
        ch["cum_e"] = cum_e
        ch["tot_e"] = cum_e[edge:edge + 1, :]
        ch["xdt"] = xbc_ref[rows, 0:D_SSM] * _rdot3(ch["dt"], e)
        ch["bg"] = []
        ch["cg"] = []
        ch["cb"] = []
        for g in range(SSM_G):
            b0 = D_SSM + g * SSM_N
            c0 = D_SSM + SSM_G * SSM_N + g * SSM_N
            bg = xbc_ref[rows, b0:b0 + SSM_N].astype(BF16)
            cg = xbc_ref[rows, c0:c0 + SSM_N].astype(BF16)
            ch["bg"].append(bg)
            ch["cg"].append(cg)
            ch["cb"].append(_dot_nt(cg, bg))

    def stage_local(ch):
        cum, cum_t, cum_e, tot_e, xdt = ch["cum"], ch["cum_t"], ch["cum_e"], ch["tot_e"], ch["xdt"]
        ch["y_diag"] = []
        ch["upd"] = []
        for g in range(SSM_G):
            gl = g * gw
            for hp in range(2):
                pl0 = gl + hp * LANES
                xpair = xdt[:, pl0:pl0 + LANES]
                acc = jnp.zeros((q, LANES), F32)
                for hh in range(2):
                    idx = lane0 + g * 4 + hp * 2 + hh
                    diff = cum[:, idx:idx + 1] - cum_t[idx:idx + 1, :]
                    m = ch["cb"][g] * jnp.exp(jnp.where(causal, diff, MASK_VALUE))
                    xm = jnp.where(low_half if hh == 0 else jnp.logical_not(low_half), xpair, 0.0)
                    acc = acc + jnp.dot(m.astype(BF16), xm.astype(BF16), preferred_element_type=F32)
                ch["y_diag"].append(acc)
            xw = (xdt[:, gl:gl + gw] * jnp.exp(tot_e[:, gl:gl + gw] - cum_e[:, gl:gl + gw])).astype(BF16)
            ch["upd"].append(_dot_tn(ch["bg"][g], xw))
        ch["off_scale"] = jnp.exp(cum_e)
        ch["dec"] = jnp.exp(tot_e)

    def stage_state():
        st = [st_ref[:, g * gw:(g + 1) * gw] for g in range(SSM_G)]
        for ch in chunks:
            y_parts = []
            for g in range(SSM_G):
                gl = g * gw
                y_off = jnp.dot(ch["cg"][g], st[g].astype(BF16), preferred_element_type=F32)
                y_parts.append(jnp.concatenate(ch["y_diag"][2 * g:2 * g + 2], axis=1)
                               + y_off * ch["off_scale"][:, gl:gl + gw])
                st[g] = st[g] * ch["dec"][:, gl:gl + gw] + ch["upd"][g]
            y_ref[ch["rows"], :] = jnp.concatenate(y_parts, axis=1)
        for g in range(SSM_G):
            st_ref[:, g * gw:(g + 1) * gw] = st[g]

    for ch in chunks:
        yield stage_cumsum(ch)
    for ch in chunks:
        yield stage_expand(ch)
    for ch in chunks:
        yield stage_local(ch)
    yield stage_state()


def _ssd_kernel(xf_ref, xb_ref, dtf_ref, dtb_ref, arow_ref, brow_ref, ef_ref, eb_ref, trif_ref, trib_ref,
                yf_ref, yb_ref, stf_ref, stb_ref):
    @pl.when(pl.program_id(1) == 0)
    def _():
        stf_ref[...] = jnp.zeros_like(stf_ref)
        stb_ref[...] = jnp.zeros_like(stb_ref)

    _round_robin([
        _ssd_direction(xf_ref, dtf_ref, arow_ref, brow_ref, ef_ref, trif_ref, yf_ref, stf_ref,
                       rev=False, lane0=0),
        _ssd_direction(xb_ref, dtb_ref, arow_ref, brow_ref, eb_ref, trib_ref, yb_ref, stb_ref,
                       rev=True, lane0=SSM_HEADS)])


def _ssd_call(xbc_c, dt_raw, a_row, b_row, e_f, e_b, tri_f, tri_b, *, n_rows, seq, nb):
    q = SSD_STEP
    blk_f, steps = _scan_block_map(seq=seq, nb=nb, q=q, rev=False)
    blk_b, _ = _scan_block_map(seq=seq, nb=nb, q=q, rev=True)
    const = lambda b, j: (0, 0)
    return pl.pallas_call(
        _ssd_kernel,
        grid=(nb, steps),
        in_specs=[
            pl.BlockSpec((q, CONV_DIM), lambda b, j: (blk_f(b, j), 0)),
            pl.BlockSpec((q, CONV_DIM), lambda b, j: (blk_b(b, j), 0)),
            pl.BlockSpec((q, LANES), lambda b, j: (blk_f(b, j), 0)),
            pl.BlockSpec((q, LANES), lambda b, j: (blk_b(b, j), 0)),
            pl.BlockSpec((1, LANES), const),
            pl.BlockSpec((1, LANES), const),
            pl.BlockSpec((3 * LANES, D_SSM), const),
            pl.BlockSpec((3 * LANES, D_SSM), const),
            pl.BlockSpec((SSD_Q, SSD_Q), const),
            pl.BlockSpec((SSD_Q, SSD_Q), const),
        ],
        out_specs=[pl.BlockSpec((q, D_SSM), lambda b, j: (blk_f(b, j), 0)),
                   pl.BlockSpec((q, D_SSM), lambda b, j: (blk_b(b, j), 0))],
        out_shape=[jax.ShapeDtypeStruct((n_rows, D_SSM), F32)] * 2,
        scratch_shapes=[pltpu.VMEM((SSM_N, D_SSM), F32)] * 2,
        compiler_params=_cparams(("arbitrary", "arbitrary")),
        name="ssd",
    )(xbc_c, xbc_c, dt_raw, dt_raw, a_row, b_row, e_f, e_b, tri_f, tri_b)


def _gla_direction(q_ref, f_ref, v_ref, lb_ref, tri_ref, same_blk, ind_ref, o_ref, p_s, upd_s, oi_s, qkb_s,
                   st_ref, *, rev):
    t = GLA_TILE
    c = GLA_SUB
    nsub = t // c
    lb = lb_ref[...]
    fr = f_ref[...]
    f = lb + (1.0 - lb) * jax.nn.sigmoid(fr)
    logf = jnp.log(jnp.maximum(f, TINY))
    k = (1.0 - lb) * jax.nn.sigmoid(-fr)
    qv = _silu(q_ref[...])
    vv = v_ref[...]
    yield
    brel = _ldot3(tri_ref[...], logf)
    tot = _ldot3(same_blk, logf)
    qkb_s[0] = qv
    qkb_s[1] = k
    qkb_s[2] = brel
    yield

    qt = (qv * jnp.exp(brel)).astype(BF16)
    kt = k * jnp.exp(tot - brel)
    dec_blk = jnp.exp(tot)
    lane = lax.broadcasted_iota(jnp.int32, (t, D_HG), 1)
    head_masks = [(lane >= h * HG_DK) & (lane < (h + 1) * HG_DK) for h in range(HG_HEADS)]
    kxs = [jnp.where(m, kt, 0.0).astype(BF16) for m in head_masks]
    vxs = [jnp.where(m, vv, 0.0).astype(BF16) for m in head_masks]
    for blk in range(nsub):
        r0 = blk * c
        kx = jnp.concatenate([a[r0:r0 + c, :] for a in kxs], axis=0)
        vx = jnp.concatenate([a[r0:r0 + c, :] for a in vxs], axis=0)
        upd_s[blk] = _dot_tn(vx, kx)
        if blk % 4 == 3:
            yield

    mid = c // 2
    bref = jnp.concatenate([jnp.broadcast_to(brel[b * c + mid:b * c + mid + 1, :], (c, D_HG))
                            for b in range(nsub)], axis=0)
    dev = brel - bref
    worst = jnp.max(jnp.max(jnp.abs(dev), axis=1, keepdims=True), axis=0, keepdims=True)
    safe_v = worst <= GLA_SAFE_EXPONENT
    safe = worst[0, 0] <= GLA_SAFE_EXPONENT
    qh = qv * jnp.exp(dev)
    kh = (k * jnp.exp(-dev)).astype(BF16)
    qx = jnp.concatenate([jnp.where(m, qh, 0.0).astype(BF16) for m in head_masks], axis=0)
    yield
    sc = _dot_nt(qx, kh)
    visible = tri_ref[...].astype(F32) > 0.5
    pm = jnp.concatenate([jnp.where(visible, sc[h * t:(h + 1) * t, :], 0.0).astype(BF16)
                          for h in range(HG_HEADS)], axis=0)
    yield
    oh = jnp.dot(pm, vv.astype(BF16), preferred_element_type=F32)
    o_fast = jnp.where(head_masks[0], oh[0:t, :], 0.0)
    for h in range(1, HG_HEADS):
        o_fast = o_fast + jnp.where(head_masks[h], oh[h * t:(h + 1) * t, :], 0.0)
    oi_s[...] = jnp.where(safe_v, o_fast, 0.0)
    yield

    @pl.when(jnp.logical_not(safe))
    def _():
        _gla_intra_pairwise(qkb_s, v_ref, ind_ref, p_s, oi_s, rev=rev)

    yield

    st = st_ref[...]
    o_inter = [None] * nsub
    for blk in (range(nsub - 1, -1, -1) if rev else range(nsub)):
        r0 = blk * c
        o_inter[blk] = _dot_nt(qt[r0:r0 + c, :], st.astype(BF16))
        st = st * dec_blk[r0:r0 + 1, :] + upd_s[blk]
        yield
    st_ref[...] = st
    o_ref[...] = oi_s[...] + jnp.concatenate(o_inter, axis=0)


def _gla_intra_pairwise(qkb_s, v_ref, ind_ref, p_s, oi_s, *, rev):
    t = GLA_TILE
    c = GLA_SUB
    nsub = t // c
    sub = SUBLANES
    qv, k, brel = qkb_s[0], qkb_s[1], qkb_s[2]
    vv = v_ref[...]
    t_idx = lax.broadcasted_iota(jnp.int32, (sub, D_HG), 0)
    pieces = [(s, g) for s in range(c) for g in range(c // sub)
              if (g <= s // sub if rev else g >= s // sub)]
    rows_per_blk = len(pieces) * sub

    brel2 = brel * LOG2E
    for blk in range(nsub):
        r0 = blk * c
        qb = qv[r0:r0 + c, :]
        kb = k[r0:r0 + c, :]
        bb = brel2[r0:r0 + c, :]
        for u in range(0, len(pieces), 2):
            rows = []
            for s, g in pieces[u:u + 2]:
                tt = t_idx + g * sub
                keep = (tt <= s) if rev else (tt >= s)
                dec = jnp.exp2(jnp.where(keep, bb[g * sub:(g + 1) * sub, :] - bb[s:s + 1, :], MASK_VALUE))
                rows.append(qb[g * sub:(g + 1) * sub, :] * kb[s:s + 1, :] * dec)
            po = blk * rows_per_blk + u * sub
            p_s[po:po + 2 * sub, :] = jnp.concatenate(rows, axis=0).astype(BF16)
    r = jnp.dot(p_s[...], ind_ref[...], preferred_element_type=F32)
    for blk in range(nsub):
        r0 = blk * c
        vb = vv[r0:r0 + c, :]
        accs = [jnp.zeros((sub, D_HG), F32) for _ in range(c // sub)]
        for n, (s, g) in enumerate(pieces):
            po = blk * rows_per_blk + n * sub
            accs[g] = accs[g] + r[po:po + sub, :] * vb[s:s + 1, :]
        oi_s[r0:r0 + c, :] = jnp.concatenate(accs, axis=0)


def _gla_kernel(qf_ref, ff_ref, vf_ref, qb_ref, fb_ref, vb_ref, lbf_ref, lbb_ref, trif_ref, trib_ref,
                ind_ref, of_ref, ob_ref, pf_s, pb_s, updf_s, updb_s, oif_s, oib_s, qkbf_s, qkbb_s,
                stf_ref, stb_ref):
    @pl.when(pl.program_id(1) == 0)
    def _():
        stf_ref[...] = jnp.zeros_like(stf_ref)
        stb_ref[...] = jnp.zeros_like(stb_ref)

    same_blk = jnp.maximum(trif_ref[...], trib_ref[...])
    _round_robin([
        _gla_direction(qf_ref, ff_ref, vf_ref, lbf_ref, trif_ref, same_blk, ind_ref, of_ref,
                       pf_s, updf_s, oif_s, qkbf_s, stf_ref, rev=False),
        _gla_direction(qb_ref, fb_ref, vb_ref, lbb_ref, trib_ref, same_blk, ind_ref, ob_ref,
                       pb_s, updb_s, oib_s, qkbb_s, stb_ref, rev=True)])


def _gla_call(hg, lb_f, lb_b, tri_f, tri_b, ind, *, n_rows, seq, nb):
    t = GLA_TILE
    blk_f, steps = _scan_block_map(seq=seq, nb=nb, q=t, rev=False)
    blk_b, _ = _scan_block_map(seq=seq, nb=nb, q=t, rev=True)
    const = lambda b, j: (0, 0)
    col = lambda blk, cidx: pl.BlockSpec((t, D_HG), lambda b, j: (blk(b, j), cidx))
    return pl.pallas_call(
        _gla_kernel,
        grid=(nb, steps),
        in_specs=[
            col(blk_f, 0), col(blk_f, 1), col(blk_f, 3),
            col(blk_b, 0), col(blk_b, 2), col(blk_b, 3),
            pl.BlockSpec((1, D_HG), const),
            pl.BlockSpec((1, D_HG), const),
            pl.BlockSpec((t, t), const),
            pl.BlockSpec((t, t), const),
            pl.BlockSpec((D_HG, D_HG), const),
        ],
        out_specs=[col(blk_f, 0), col(blk_b, 0)],
        out_shape=[jax.ShapeDtypeStruct((n_rows, D_HG), F32)] * 2,
        scratch_shapes=(
            [pltpu.VMEM((GLA_P_ROWS, D_HG), BF16)] * 2
            + [pltpu.VMEM((GLA_TILE // GLA_SUB, D_HG, D_HG), F32)] * 2
            + [pltpu.VMEM((GLA_TILE, D_HG), F32)] * 2
            + [pltpu.VMEM((3, GLA_TILE, D_HG), F32)] * 2
            + [pltpu.VMEM((D_HG, D_HG), F32)] * 2),
        compiler_params=_cparams(("arbitrary", "arbitrary")),
        name="gla",
    )(hg, hg, hg, hg, hg, hg, lb_f, lb_b, tri_f, tri_b, ind)


def _expand_heads(x, masks):
    return jnp.concatenate([jnp.where(m, x, jnp.zeros_like(x)) for m in masks], axis=0)


def _collapse_heads(o, masks, t):
    acc = jnp.where(masks[0], o[0:t, :], 0.0)
    for h in range(1, NA_HEADS):
        acc = acc + jnp.where(masks[h], o[h * t:(h + 1) * t, :], 0.0)
    return acc


def _na_finish(o, nw):
    ms = jnp.mean(o * o, axis=-1, keepdims=True)
    return (o * lax.rsqrt(ms + EPS)) * nw


def _na_kernel(q_ref, k_ref, v_ref, kc_ref, vc_ref, bias_ref, nw_ref, o_ref, *, n_grid_rows):
    w = GRID_W
    i = pl.program_id(1)
    lane = lax.broadcasted_iota(jnp.int32, (w, D_NA), 1)
    masks = [(lane >= h * NA_DH) & (lane < (h + 1) * NA_DH) for h in range(NA_HEADS)]
    kc = kc_ref[...]
    vc = vc_ref[...]
    nw = nw_ref[...]
    scale = NA_DH ** -0.5

    def query_row(jr):
        r = i * NA_ROWS + jr
        r0 = jnp.clip(r - WIN_R // 2, 0, n_grid_rows - WIN_R)
        var = r - r0
        k0 = pl.multiple_of(r0 * w, w)
        q0 = jr * w
        qx = _expand_heads(q_ref[q0:q0 + w, :] * scale, masks)
        kw = k_ref[pl.ds(k0, WIN_R * w), :]
        vw = v_ref[pl.ds(k0, WIN_R * w), :]
        s_loc = _dot_nt(qx, kw) + bias_ref[var]
        s_ctx = _dot_nt(qx, kc)
        yield
        m = jnp.maximum(jnp.max(s_loc, axis=-1, keepdims=True), jnp.max(s_ctx, axis=-1, keepdims=True))
        p_loc = jnp.exp(s_loc - m)
        p_ctx = jnp.exp(s_ctx - m)
        den = jnp.sum(p_loc, axis=-1, keepdims=True) + jnp.sum(p_ctx, axis=-1, keepdims=True)
        yield
        o = (jnp.dot(p_loc.astype(BF16), vw, preferred_element_type=F32)
             + jnp.dot(p_ctx.astype(BF16), vc, preferred_element_type=F32)) / den
        yield
        o_ref[q0:q0 + w, :] = _na_finish(_collapse_heads(o, masks, w), nw)

    for g0 in range(0, NA_ROWS, NA_INTERLEAVE):
        _round_robin([query_row(jr) for jr in range(g0, g0 + NA_INTERLEAVE)])


def _na_call(na, bias, nw, *, seq, nb):
    rows = seq // GRID_W
    tq = NA_ROWS * GRID_W
    qpb = seq // tq
    ctx_blk = nb * seq // CTX_LEN
    kern = functools.partial(_na_kernel, n_grid_rows=rows)
    return pl.pallas_call(
        kern,
        grid=(nb, qpb),
        in_specs=[
            pl.BlockSpec((tq, D_NA), lambda b, i: (b * qpb + i, 0)),
            pl.BlockSpec((seq, D_NA), lambda b, i: (b, 1)),
            pl.BlockSpec((seq, D_NA), lambda b, i: (b, 2)),
            pl.BlockSpec((CTX_LEN, D_NA), lambda b, i: (ctx_blk + b, 1)),
            pl.BlockSpec((CTX_LEN, D_NA), lambda b, i: (ctx_blk + b, 2)),
            pl.BlockSpec((WIN_R, NA_HEADS * GRID_W, WIN_R * GRID_W), lambda b, i: (0, 0, 0)),
            pl.BlockSpec((1, D_NA), lambda b, i: (0, 0)),
        ],
        out_specs=pl.BlockSpec((tq, D_NA), lambda b, i: (b * qpb + i, 0)),
        out_shape=jax.ShapeDtypeStruct((nb * seq, D_NA), F32),
        compiler_params=_cparams(("parallel", "arbitrary")),
        name="natten",
    )(na, na, na, na, na, bias, nw)


def _ctxattn_kernel(q_ref, k_ref, v_ref, nw_ref, o_ref):
    t = CTX_LEN
    lane = lax.broadcasted_iota(jnp.int32, (t, D_NA), 1)
    masks = [(lane >= h * NA_DH) & (lane < (h + 1) * NA_DH) for h in range(NA_HEADS)]
    qx = _expand_heads(q_ref[...] * (NA_DH ** -0.5), masks)
    s = _dot_nt(qx, k_ref[...])
    m = jnp.max(s, axis=-1, keepdims=True)
    p = jnp.exp(s - m)
    den = jnp.sum(p, axis=-1, keepdims=True)
    o = jnp.dot(p.astype(BF16), v_ref[...], preferred_element_type=F32) / den
    o_ref[...] = _na_finish(_collapse_heads(o, masks, t), nw_ref[...])


def _ctxattn_call(na, nw, *, seq, nb):
    ctx_blk = nb * seq // CTX_LEN
    return pl.pallas_call(
        _ctxattn_kernel,
        grid=(nb,),
        in_specs=[
            pl.BlockSpec((CTX_LEN, D_NA), lambda b: (ctx_blk + b, 0)),
            pl.BlockSpec((CTX_LEN, D_NA), lambda b: (ctx_blk + b, 1)),
            pl.BlockSpec((CTX_LEN, D_NA), lambda b: (ctx_blk + b, 2)),
            pl.BlockSpec((1, D_NA), lambda b: (0, 0)),
        ],
        out_specs=pl.BlockSpec((CTX_LEN, D_NA), lambda b: (b, 0)),
        out_shape=jax.ShapeDtypeStruct((nb * CTX_LEN, D_NA), F32),
        compiler_params=_cparams(("parallel",)),
        name="ctxattn",
    )(na, na, na, nw)


def _mixed_residual(x, mod_ref, of_ref, ob_ref, g_ref, na_ref, yf_ref, yb_ref, xs_ref, z_ref,
                    hgw_ref, hm_ref, dsk_ref, sw_ref, wo_ref, *, rows):
    hm = hm_ref[...]
    o = of_ref[rows, :] + ob_ref[rows, :]
    sq = o * o
    hi = sq.astype(BF16)
    lo = (sq - hi.astype(F32)).astype(BF16)
    ms = (jnp.dot(hi, hm, preferred_element_type=F32)
          + jnp.dot(lo, hm, preferred_element_type=F32)) * (1.0 / HG_DK)
    hg = (o * lax.rsqrt(ms + EPS)) * hgw_ref[...] * _silu(g_ref[rows, :])
    ys = (yf_ref[rows, :] + yb_ref[rows, :] + dsk_ref[...] * xs_ref[rows, :]) * _silu(z_ref[rows, :])
    ms2 = jnp.mean(ys * ys, axis=-1, keepdims=True)
    ssm = (ys * lax.rsqrt(ms2 + EPS)) * sw_ref[...]
    mix = (jnp.dot(hg.astype(BF16), wo_ref[0:D_HG, :], preferred_element_type=F32)
           + jnp.dot(na_ref[rows, :].astype(BF16), wo_ref[D_HG:D_HG + D_NA, :], preferred_element_type=F32)
           + jnp.dot(ssm.astype(BF16), wo_ref[D_HG + D_NA:, :], preferred_element_type=F32))
    return x + mod_ref[5:6, :] * mix


def _mix_specs(tm):
    row = lambda i: (i, 0)
    const = lambda i: (0, 0)
    return [
        pl.BlockSpec((tm, D_HG), row),
        pl.BlockSpec((tm, D_HG), row),
        pl.BlockSpec((tm, D_HG), lambda i: (i, 4)),
        pl.BlockSpec((tm, D_NA), row),
        pl.BlockSpec((tm, D_SSM), row),
        pl.BlockSpec((tm, D_SSM), row),
        pl.BlockSpec((tm, D_SSM), row),
        pl.BlockSpec((tm, D_SSM), row),
        pl.BlockSpec((1, D_HG), const),
        pl.BlockSpec((D_HG, D_HG), const),
        pl.BlockSpec((1, D_SSM), const),
        pl.BlockSpec((1, D_SSM), const),
        pl.BlockSpec((D_MODEL, D_MODEL), const, pipeline_mode=pl.Buffered(1)),
    ]


def _na_bias_table(rpb):
    w = GRID_W
    ndr = 2 * WIN_R - 1
    ndc = 2 * WIN_C - 1
    col = np.arange(w)
    c0 = np.clip(col - WIN_C // 2, 0, w - WIN_C)
    col_in = (col[None, :] >= c0[:, None]) & (col[None, :] < c0[:, None] + WIN_C)
    dc = np.clip(col[None, :] - col[:, None], -(WIN_C - 1), WIN_C - 1) + (WIN_C - 1)
    onehot = (dc.reshape(1, -1) == np.arange(ndc)[:, None]).astype(np.float32)
    t = jnp.dot(rpb.reshape(NA_HEADS * ndr, ndc).astype(F32), onehot,
                precision=lax.Precision.HIGHEST).reshape(NA_HEADS, ndr, w, w)
    t = jnp.where(col_in[None, None], t, MASK_VALUE)
    b = jnp.stack([t[:, WIN_R - 1 - var:2 * WIN_R - 1 - var] for var in range(WIN_R)], axis=0)
    b = jnp.transpose(b, (0, 1, 3, 2, 4))
    return b.reshape(WIN_R, NA_HEADS * w, WIN_R * w)


def _block_tri(n, c, rev):
    r = jnp.arange(n)
    same = (r[:, None] // c) == (r[None, :] // c)
    tri = (r[None, :] >= r[:, None]) if rev else (r[None, :] <= r[:, None])
    return (same & tri).astype(BF16)


def _head_block_ones(n, hd):
    r = jnp.arange(n)
    return ((r[:, None] // hd) == (r[None, :] // hd)).astype(BF16)


def _ssd_expand(lane0):
    r = jnp.arange(LANES)[:, None]
    cidx = jnp.arange(D_SSM)[None, :]
    e = (r == lane0 + cidx // SSM_P).astype(BF16)
    return jnp.concatenate([e, e, e], axis=0)


def _lane_row(vals):
    return jnp.zeros((1, LANES), F32).at[0, :vals.shape[0]].set(vals.astype(F32))


def kernel(x, c, ctx, c_ctx, w_mod, b_mod, norm_ffn1, ffn1_w13, ffn1_w2, norm_mix, w_in,
           hg_lower_bounds, hg_norm, na_rpb, na_norm, ssm_conv_w, ssm_conv_b, ssm_a_log,
           ssm_dt_bias, ssm_d, ssm_norm, w_out, norm_ffn2, ffn2_w13, ffn2_w2, final_norm):
    nb, seq, d = x.shape
    depth = w_mod.shape[0]
    assert d == D_MODEL and ctx.shape[1] == CTX_LEN and nb + 1 <= MOD_ROWS
    assert seq % 512 == 0 and seq // GRID_W >= WIN_R
    n_lat = nb * seq
    n_all = n_lat + nb * CTX_LEN

    lb_soft = jax.nn.softmax(hg_lower_bounds.astype(F32), axis=1)
    lower_bounds = jnp.cumsum(lb_soft, axis=1) - lb_soft[:, :1]

    c_rows = jnp.zeros((MOD_ROWS, d), F32).at[:nb].set(c).at[nb].set(c_ctx)
    mods = _mods_call(c_rows, w_mod, b_mod).reshape(depth, MOD_ROWS, N_MOD, d)

    x_all = x.reshape(n_lat, d)
    x_ctx = ctx.reshape(nb * CTX_LEN, d)

    tri_f = _block_tri(GLA_TILE, GLA_SUB, False)
    tri_b = _block_tri(GLA_TILE, GLA_SUB, True)
    ind = _head_block_ones(D_HG, HG_DK)
    ssd_tri_f = _block_tri(SSD_Q, SSD_Q, False)
    ssd_tri_b = _block_tri(SSD_Q, SSD_Q, True)
    e_f = _ssd_expand(0)
    e_b = _ssd_expand(SSM_HEADS)
    one_row = lambda v: v.reshape(1, -1).astype(F32)
    common = dict(seq=seq, nb=nb)

    for layer in range(depth):
        last = layer == depth - 1
        mods_l = mods[layer]
        w13_1 = _cast_call(ffn1_w13, layer)
        w2_1 = _cast_call(ffn1_w2, layer)
        w13_2 = _cast_call(ffn2_w13, layer)
        w2_2 = _cast_call(ffn2_w2, layer)
        w_in_b = _cast_call(w_in, layer, out_cols=D_IN_PAD)
        wo_b = _cast_call(w_out, layer)
        fw = one_row(final_norm)

        x_all = _ffn_call(x_all, mods_l, one_row(norm_ffn1[layer]), w13_1, w2_1, fw,
                          n_rows=n_all, ks=(0, 1, 2), final=False, x_ctx=x_ctx if layer == 0 else None,
                          **common)
        hg, na, z, xbc_c, dt_raw = _inproj_call(x_all, mods_l, one_row(norm_mix[layer]), w_in_b,
                                                ssm_conv_w[layer].astype(F32), one_row(ssm_conv_b[layer]),
                                                n_rows=n_all, **common)

        o_f, o_b = _gla_call(hg, one_row(lower_bounds[0, layer]), one_row(lower_bounds[1, layer]),
                             tri_f, tri_b, ind, n_rows=n_all, **common)

        bias = _na_bias_table(na_rpb[layer])
        nw = one_row(na_norm[layer])
        y_na = _na_call(na, bias, nw, **common)
        if not last:
            y_na = jnp.concatenate([y_na, _ctxattn_call(na, nw, **common)], axis=0)

        a_neg = -jnp.exp(ssm_a_log[layer].astype(F32))
        y_f, y_b = _ssd_call(xbc_c, dt_raw, _lane_row(a_neg.reshape(-1)),
                             _lane_row(ssm_dt_bias[layer].reshape(-1)),
                             e_f, e_b, ssd_tri_f, ssd_tri_b, n_rows=n_all, **common)

        n_out = n_lat if last else n_all
        dsk = jnp.repeat(ssm_d[layer].astype(F32), SSM_P).reshape(1, D_SSM)
        mix = (o_f, o_b, hg, y_na, y_f, y_b, xbc_c, z,
               one_row(hg_norm[layer]), ind, dsk, one_row(ssm_norm[layer]), wo_b)
        assert len(mix) == N_MIX_OPERANDS
        x_all = _ffn_call(x_all, mods_l, one_row(norm_ffn2[layer]), w13_2, w2_2, fw,
                          n_rows=n_out, ks=(6, 7, 8), final=last, mix=mix, **common)

    return x_all.reshape(nb, seq, d)
```

```python
import functools
import math

import jax
import jax.numpy as jnp
import numpy as np
from jax import lax
from jax.experimental import pallas as pl
from jax.experimental.pallas import tpu as pltpu

F32 = jnp.float32
BF16 = jnp.bfloat16

D_MODEL = 1024
GRID_W = 64
CTX_LEN = 256
EPS = 1e-6
N_MOD = 9
MASK_VALUE = -1e30
TINY = 1e-20
LOG2E = 1.4426950408889634

D_HG = 256
HG_HEADS = 4
HG_DK = 64
D_NA = 256
NA_HEADS = 4
NA_DH = 64
WIN_R = 8
WIN_C = 16
D_SSM = 512
SSM_HEADS = 8
SSM_P = 64
SSM_N = 128
SSM_G = 2
CONV_W = 5
CONV_DIM = D_SSM + 2 * SSM_G * SSM_N
D_FF = 2816
D_IN_PAD = 3712
LANES = 128
SUBLANES = 8
MXU_COLS = 256
MOD_ROWS = 8

GLA_TILE = 256
GLA_SUB = 32
GLA_SAFE_EXPONENT = 60.0
_GLA_GROUPS = GLA_SUB // SUBLANES
GLA_P_ROWS = (GLA_TILE // GLA_SUB) * SUBLANES * SUBLANES * _GLA_GROUPS * (_GLA_GROUPS + 1) // 2
SSD_Q = 128
SSD_STEP = 256
NA_ROWS = 16
NA_INTERLEAVE = 4

VMEM_LIMIT = 56 * 1024 * 1024


def _cparams(sem):
    return pltpu.CompilerParams(dimension_semantics=sem, vmem_limit_bytes=VMEM_LIMIT)


def _silu(x):
    return x * jax.nn.sigmoid(x)


def _split3(x):
    hi = x.astype(BF16)
    r1 = x - hi.astype(F32)
    mid = r1.astype(BF16)
    lo = (r1 - mid.astype(F32)).astype(BF16)
    return hi, mid, lo


def _ldot3(a_bf16, x):
    hi, mid, lo = _split3(x)
    n = x.shape[1]
    if n % MXU_COLS == 0:
        return (jnp.dot(a_bf16, hi, preferred_element_type=F32)
                + jnp.dot(a_bf16, mid, preferred_element_type=F32)
                + jnp.dot(a_bf16, lo, preferred_element_type=F32))
    r = jnp.dot(a_bf16, jnp.concatenate([hi, mid, lo], axis=1), preferred_element_type=F32)
    return r[:, 0:n] + r[:, n:2 * n] + r[:, 2 * n:3 * n]


def _rdot3(x, e3_bf16):
    return jnp.dot(jnp.concatenate(_split3(x), axis=1), e3_bf16, preferred_element_type=F32)


def _dot_nt(a, b):
    return lax.dot_general(a, b, (((1,), (1,)), ((), ())), preferred_element_type=F32)


def _dot_tn(a, b):
    return lax.dot_general(a, b, (((0,), (0,)), ((), ())), preferred_element_type=F32)


def _modulated(x, nw, shift, scale):
    ms = jnp.mean(x * x, axis=-1, keepdims=True)
    y = x * lax.rsqrt(ms + EPS)
    return (y * nw) * (1.0 + scale) + shift


def _cast_kernel(w_ref, o_ref):
    cols = w_ref.shape[1]
    if o_ref.shape[1] == cols:
        o_ref[...] = w_ref[...].astype(BF16)
    else:
        o_ref[:, 0:cols] = w_ref[...].astype(BF16)
        o_ref[:, cols:] = jnp.zeros((o_ref.shape[0], o_ref.shape[1] - cols), BF16)


def _cast_call(w, layer, out_cols=None, tr=256):
    _, rows, cols = w.shape
    out_cols = out_cols or cols
    return pl.pallas_call(
        _cast_kernel,
        grid=(rows // tr,),
        in_specs=[pl.BlockSpec((None, tr, cols), lambda i: (layer, i, 0))],
        out_specs=pl.BlockSpec((tr, out_cols), lambda i: (i, 0)),
        out_shape=jax.ShapeDtypeStruct((rows, out_cols), BF16),
        compiler_params=_cparams(("parallel",)),
        name="cast",
    )(w)


def _mods_kernel(c_ref, w_ref, b_ref, o_ref):
    sc = _silu(c_ref[...]).astype(BF16)
    o_ref[...] = jnp.dot(sc, w_ref[...].astype(BF16), preferred_element_type=F32) + b_ref[...]


def _mods_call(c_rows, w_mod, b_mod):
    depth = w_mod.shape[0]
    tn = 1152
    n = N_MOD * D_MODEL
    return pl.pallas_call(
        _mods_kernel,
        grid=(depth, n // tn),
        in_specs=[
            pl.BlockSpec((MOD_ROWS, D_MODEL), lambda l, j: (0, 0)),
            pl.BlockSpec((None, D_MODEL, tn), lambda l, j: (l, 0, j)),
            pl.BlockSpec((None, 1, tn), lambda l, j: (l, 0, j)),
        ],
        out_specs=pl.BlockSpec((None, MOD_ROWS, tn), lambda l, j: (l, 0, j)),
        out_shape=jax.ShapeDtypeStruct((depth, MOD_ROWS, n), F32),
        compiler_params=_cparams(("parallel", "parallel")),
        name="mods",
    )(c_rows, w_mod, b_mod.reshape(depth, 1, n))


def _mod_spec(tm, seq, nb):
    tpb = seq // tm
    return pl.BlockSpec((None, N_MOD, D_MODEL), lambda i: (jnp.minimum(i // tpb, nb), 0, 0))


MIXFFN_CHUNKS = 2
N_MIX_OPERANDS = 13


def _ffn_kernel(*refs, k_shift, k_scale, k_gate, final, n_lat_tiles, mixed):
    if mixed:
        x_ref, mod_ref = refs[0:2]
        mix_refs = refs[2:2 + N_MIX_OPERANDS]
        nw_ref, wu_ref, wg_ref, w2_ref, fw_ref, o_ref, act_ref = refs[2 + N_MIX_OPERANDS:]
        x = None
    elif n_lat_tiles is None:
        x_ref, mod_ref, nw_ref, wu_ref, wg_ref, w2_ref, fw_ref, o_ref, act_ref = refs
        x = x_ref[...]
    else:
        x_ref, xc_ref, mod_ref, nw_ref, wu_ref, wg_ref, w2_ref, fw_ref, o_ref, act_ref = refs
        x = jnp.where(pl.program_id(0) >= n_lat_tiles, xc_ref[...], x_ref[...])
    tm = o_ref.shape[0]
    cw = MXU_COLS
    chunks = MIXFFN_CHUNKS if mixed else 1
    rc = tm // chunks
    for n in range(chunks):
        rows = slice(n * rc, (n + 1) * rc)
        xr = _mixed_residual(x_ref[rows, :], mod_ref, *mix_refs, rows=rows) if mixed else x
        h = _modulated(xr, nw_ref[...], mod_ref[k_shift:k_shift + 1, :],
                       mod_ref[k_scale:k_scale + 1, :]).astype(BF16)
        for c in range(D_FF // cw):
            u = jnp.dot(h, wu_ref[:, c * cw:(c + 1) * cw], preferred_element_type=F32)
            g = jnp.dot(h, wg_ref[:, c * cw:(c + 1) * cw], preferred_element_type=F32)
            act_ref[rows, c * cw:(c + 1) * cw] = (_silu(g) * u).astype(BF16)
        y = jnp.dot(act_ref[rows, :], w2_ref[...], preferred_element_type=F32)
        out = xr + (0.5 * mod_ref[k_gate:k_gate + 1, :]) * y
        if final:
            ms = jnp.mean(out * out, axis=-1, keepdims=True)
            out = (out * lax.rsqrt(ms + EPS)) * fw_ref[...]
        o_ref[rows, :] = out


def _ffn_call(x_all, mods_l, nw, w13b, w2b, fw, *, n_rows, seq, nb, ks, final, x_ctx=None, mix=None, tm=512):
    mod_spec = _mod_spec(tm, seq, nb)
    if mix is not None:
        n_lat_tiles = None
        lead_args = (x_all, mods_l) + tuple(mix)
        lead_specs = [pl.BlockSpec((tm, D_MODEL), lambda i: (i, 0)), mod_spec] + _mix_specs(tm)
    elif x_ctx is None:
        n_lat_tiles = None
        lead_args = (x_all, mods_l)
        lead_specs = [pl.BlockSpec((tm, D_MODEL), lambda i: (i, 0)), mod_spec]
    else:
        n_lat_tiles = nb * seq // tm
        lead_args = (x_all, x_ctx, mods_l)
        lead_specs = [pl.BlockSpec((tm, D_MODEL), lambda i: (jnp.minimum(i, n_lat_tiles - 1), 0)),
                      pl.BlockSpec((tm, D_MODEL), lambda i: (jnp.maximum(i - n_lat_tiles, 0), 0)), mod_spec]
    kern = functools.partial(_ffn_kernel, k_shift=ks[0], k_scale=ks[1], k_gate=ks[2], final=final,
                             n_lat_tiles=n_lat_tiles, mixed=mix is not None)
    return pl.pallas_call(
        kern,
        grid=(n_rows // tm,),
        in_specs=lead_specs + [
            pl.BlockSpec((1, D_MODEL), lambda i: (0, 0)),
            pl.BlockSpec((D_MODEL, D_FF), lambda i: (0, 0), pipeline_mode=pl.Buffered(1)),
            pl.BlockSpec((D_MODEL, D_FF), lambda i: (0, 1), pipeline_mode=pl.Buffered(1)),
            pl.BlockSpec((D_FF, D_MODEL), lambda i: (0, 0), pipeline_mode=pl.Buffered(1)),
            pl.BlockSpec((1, D_MODEL), lambda i: (0, 0)),
        ],
        out_specs=pl.BlockSpec((tm, D_MODEL), lambda i: (i, 0)),
        out_shape=jax.ShapeDtypeStruct((n_rows, D_MODEL), F32),
        scratch_shapes=[pltpu.VMEM((tm, D_FF), BF16)],
        compiler_params=_cparams(("parallel",)),
        name="mixffn" if mix is not None else "ffn",
    )(*lead_args, nw, w13b, w13b, w2b, fw)


_IN_SPLITS = ((0, 1280), (1280, 2048), (2048, 2560), (2560, 3584), (3584, 3712))


_XBC_COLS = _IN_SPLITS[3]
HALO = 8


def _inproj_kernel(x_ref, xp_ref, xn_ref, mod_ref, nw_ref, w_ref, cw_ref, cb_ref,
                   hg_ref, na_ref, z_ref, xbc_ref, dt_ref, ext_ref, *, n_lat_tiles, tpb):
    tm = x_ref.shape[0]
    i = pl.program_id(0)
    is_ctx = i >= n_lat_tiles
    first = jnp.logical_or(is_ctx, (i % tpb) == 0)
    last = jnp.logical_or(is_ctx, (i % tpb) == tpb - 1)
    nw = nw_ref[...]
    shift = mod_ref[3:4, :]
    scale = mod_ref[4:5, :]
    h = _modulated(x_ref[...], nw, shift, scale).astype(BF16)
    x_halo = jnp.concatenate([xp_ref[...], xn_ref[...]], axis=0)
    h_halo = _modulated(x_halo, nw, shift, scale).astype(BF16)

    a, b = _XBC_COLS
    wx = w_ref[:, a:b]
    halo = jnp.dot(h_halo, wx, preferred_element_type=F32)
    ext_ref[0:HALO, :] = jnp.where(first, 0.0, halo[0:HALO, :])
    ext_ref[HALO:HALO + tm, :] = jnp.dot(h, wx, preferred_element_type=F32)
    ext_ref[HALO + tm:2 * HALO + tm, :] = jnp.where(last, 0.0, halo[HALO:2 * HALO, :])
    pos = lax.broadcasted_iota(jnp.int32, (tm, 1), 0) % CTX_LEN
    acc = jnp.zeros((tm, CONV_DIM), F32) + cb_ref[...]
    for j in range(CONV_W):
        d = j - CONV_W // 2
        tap = ext_ref[HALO + d:HALO + d + tm, :] * cw_ref[j:j + 1, :]
        if d != 0:
            crosses = jnp.logical_and(is_ctx, jnp.logical_or(pos + d < 0, pos + d >= CTX_LEN))
            tap = jnp.where(crosses, 0.0, tap)
        acc = acc + tap
    xbc_ref[...] = _silu(acc)

    for (a, b), o_ref in zip(_IN_SPLITS, (hg_ref, na_ref, z_ref, None, dt_ref)):
        if o_ref is not None:
            o_ref[...] = jnp.dot(h, w_ref[:, a:b], preferred_element_type=F32).astype(o_ref.dtype)


def _inproj_call(x_all, mods_l, nw, w_in_b, conv_w, conv_b, *, n_rows, seq, nb, tm=512):
    widths = [b - a for a, b in _IN_SPLITS]
    dtypes = [F32, BF16, F32, F32, F32]
    r8 = tm // HALO
    last8 = n_rows // HALO - 1
    kern = functools.partial(_inproj_kernel, n_lat_tiles=nb * seq // tm, tpb=seq // tm)
    return pl.pallas_call(
        kern,
        grid=(n_rows // tm,),
        in_specs=[
            pl.BlockSpec((tm, D_MODEL), lambda i: (i, 0)),
            pl.BlockSpec((HALO, D_MODEL), lambda i: (jnp.maximum(i * r8 - 1, 0), 0)),
            pl.BlockSpec((HALO, D_MODEL), lambda i: (jnp.minimum((i + 1) * r8, last8), 0)),
            _mod_spec(tm, seq, nb),
            pl.BlockSpec((1, D_MODEL), lambda i: (0, 0)),
            pl.BlockSpec((D_MODEL, D_IN_PAD), lambda i: (0, 0), pipeline_mode=pl.Buffered(1)),
            pl.BlockSpec((CONV_W, CONV_DIM), lambda i: (0, 0)),
            pl.BlockSpec((1, CONV_DIM), lambda i: (0, 0)),
        ],
        out_specs=[pl.BlockSpec((tm, w), lambda i: (i, 0)) for w in widths],
        out_shape=[jax.ShapeDtypeStruct((n_rows, w), dt) for w, dt in zip(widths, dtypes)],
        scratch_shapes=[pltpu.VMEM((tm + 2 * HALO, CONV_DIM), F32)],
        compiler_params=_cparams(("parallel",)),
        name="inproj",
    )(x_all, x_all, x_all, mods_l, nw, w_in_b, conv_w, conv_b)


def _scan_block_map(*, seq, nb, q, rev):
    nc_ctx = CTX_LEN // q
    nc_lat = seq // q
    ctx_base = nb * seq // q

    def row_block(b, j):
        cj = j
        lj = j - nc_ctx
        if rev:
            cj = nc_ctx - 1 - cj
            lj = nc_lat - 1 - lj
        return jnp.where(j < nc_ctx, ctx_base + b * nc_ctx + cj, b * nc_lat + lj)

    return row_block, nc_ctx + nc_lat


def _round_robin(stage_generators):
    live = list(stage_generators)
    while live:
        for gen in list(live):
            try:
                next(gen)
            except StopIteration:
                live.remove(gen)


def _ssd_direction(xbc_ref, dt_ref, arow_ref, brow_ref, e_ref, tri_ref, y_ref, st_ref, *, rev, lane0):
    q = SSD_Q
    nch = SSD_STEP // q
    gw = D_SSM // SSM_G
    t_idx = lax.broadcasted_iota(jnp.int32, (q, q), 0)
    s_idx = lax.broadcasted_iota(jnp.int32, (q, q), 1)
    causal = (s_idx >= t_idx) if rev else (s_idx <= t_idx)
    lane = lax.broadcasted_iota(jnp.int32, (q, LANES), 1)
    low_half = lane < SSM_P
    e = e_ref[...]
    tri = tri_ref[...]
    edge = 0 if rev else q - 1
    chunks = [dict(rows=slice(ci * q, (ci + 1) * q))
              for ci in (range(nch - 1, -1, -1) if rev else range(nch))]

    def stage_cumsum(ch):
        rows = ch["rows"]
        dt = jax.nn.softplus(dt_ref[rows, :] + brow_ref[...])
        ch["dt"] = dt
        ch["cum"] = _ldot3(tri, dt * arow_ref[...])

    def stage_expand(ch):
        rows = ch["rows"]
        cum = ch["cum"]
        ch["cum_t"] = cum.T
        cum_e = _rdot3(cum, e)
        ch["cum_e"] = cum_e
        ch["tot_e"] = cum_e[edge:edge + 1, :]
        ch["xdt"] = xbc_ref[rows, 0:D_SSM] * _rdot3(ch["dt"], e)
        ch["bg"] = []
        ch["cg"] = []
        ch["cb"] = []
        for g in range(SSM_G):
            b0 = D_SSM + g * SSM_N
            c0 = D_SSM + SSM_G * SSM_N + g * SSM_N
            bg = xbc_ref[rows, b0:b0 + SSM_N].astype(BF16)
            cg = xbc_ref[rows, c0:c0 + SSM_N].astype(BF16)
            ch["bg"].append(bg)
            ch["cg"].append(cg)
            ch["cb"].append(_dot_nt(cg, bg))

    def stage_local(ch):
        cum, cum_t, cum_e, tot_e, xdt = ch["cum"], ch["cum_t"], ch["cum_e"], ch["tot_e"], ch["xdt"]
        ch["y_diag"] = []
        ch["upd"] = []
        for g in range(SSM_G):
            gl = g * gw
            for hp in range(2):
                pl0 = gl + hp * LANES
                xpair = xdt[:, pl0:pl0 + LANES]
                acc = jnp.zeros((q, LANES), F32)
                for hh in range(2):
                    idx = lane0 + g * 4 + hp * 2 + hh
                    diff = cum[:, idx:idx + 1] - cum_t[idx:idx + 1, :]
                    m = ch["cb"][g] * jnp.exp(jnp.where(causal, diff, MASK_VALUE))
                    xm = jnp.where(low_half if hh == 0 else jnp.logical_not(low_half), xpair, 0.0)
                    acc = acc + jnp.dot(m.astype(BF16), xm.astype(BF16), preferred_element_type=F32)
                ch["y_diag"].append(acc)
            xw = (xdt[:, gl:gl + gw] * jnp.exp(tot_e[:, gl:gl + gw] - cum_e[:, gl:gl + gw])).astype(BF16)
            ch["upd"].append(_dot_tn(ch["bg"][g], xw))
        ch["off_scale"] = jnp.exp(cum_e)
        ch["dec"] = jnp.exp(tot_e)

    def stage_state():
        st = [st_ref[:, g * gw:(g + 1) * gw] for g in range(SSM_G)]
        for ch in chunks:
            y_parts = []
            for g in range(SSM_G):
                gl = g * gw
                y_off = jnp.dot(ch["cg"][g], st[g].astype(BF16), preferred_element_type=F32)
                y_parts.append(jnp.concatenate(ch["y_diag"][2 * g:2 * g + 2], axis=1)
                               + y_off * ch["off_scale"][:, gl:gl + gw])
                st[g] = st[g] * ch["dec"][:, gl:gl + gw] + ch["upd"][g]
            y_ref[ch["rows"], :] = jnp.concatenate(y_parts, axis=1)
        for g in range(SSM_G):
            st_ref[:, g * gw:(g + 1) * gw] = st[g]

    for ch in chunks:
        yield stage_cumsum(ch)
    for ch in chunks:
        yield stage_expand(ch)
    for ch in chunks:
        yield stage_local(ch)
    yield stage_state()


def _ssd_kernel(xf_ref, xb_ref, dtf_ref, dtb_ref, arow_ref, brow_ref, ef_ref, eb_ref, trif_ref, trib_ref,
                yf_ref, yb_ref, stf_ref, stb_ref):
    @pl.when(pl.program_id(1) == 0)
    def _():
        stf_ref[...] = jnp.zeros_like(stf_ref)
        stb_ref[...] = jnp.zeros_like(stb_ref)

    _round_robin([
        _ssd_direction(xf_ref, dtf_ref, arow_ref, brow_ref, ef_ref, trif_ref, yf_ref, stf_ref,
                       rev=False, lane0=0),
        _ssd_direction(xb_ref, dtb_ref, arow_ref, brow_ref, eb_ref, trib_ref, yb_ref, stb_ref,
                       rev=True, lane0=SSM_HEADS)])


def _ssd_call(xbc_c, dt_raw, a_row, b_row, e_f, e_b, tri_f, tri_b, *, n_rows, seq, nb):
    q = SSD_STEP
    blk_f, steps = _scan_block_map(seq=seq, nb=nb, q=q, rev=False)
    blk_b, _ = _scan_block_map(seq=seq, nb=nb, q=q, rev=True)
    const = lambda b, j: (0, 0)
    return pl.pallas_call(
        _ssd_kernel,
        grid=(nb, steps),
        in_specs=[
            pl.BlockSpec((q, CONV_DIM), lambda b, j: (blk_f(b, j), 0)),
            pl.BlockSpec((q, CONV_DIM), lambda b, j: (blk_b(b, j), 0)),
            pl.BlockSpec((q, LANES), lambda b, j: (blk_f(b, j), 0)),
            pl.BlockSpec((q, LANES), lambda b, j: (blk_b(b, j), 0)),
            pl.BlockSpec((1, LANES), const),
            pl.BlockSpec((1, LANES), const),
            pl.BlockSpec((3 * LANES, D_SSM), const),
            pl.BlockSpec((3 * LANES, D_SSM), const),
            pl.BlockSpec((SSD_Q, SSD_Q), const),
            pl.BlockSpec((SSD_Q, SSD_Q), const),
        ],
        out_specs=[pl.BlockSpec((q, D_SSM), lambda b, j: (blk_f(b, j), 0)),
                   pl.BlockSpec((q, D_SSM), lambda b, j: (blk_b(b, j), 0))],
        out_shape=[jax.ShapeDtypeStruct((n_rows, D_SSM), F32)] * 2,
        scratch_shapes=[pltpu.VMEM((SSM_N, D_SSM), F32)] * 2,
        compiler_params=_cparams(("arbitrary", "arbitrary")),
        name="ssd",
    )(xbc_c, xbc_c, dt_raw, dt_raw, a_row, b_row, e_f, e_b, tri_f, tri_b)


def _gla_direction(q_ref, f_ref, v_ref, lb_ref, tri_ref, same_blk, ind_ref, o_ref, p_s, upd_s, oi_s, qkb_s,
                   st_ref, *, rev):
    t = GLA_TILE
    c = GLA_SUB
    nsub = t // c
    lb = lb_ref[...]
    fr = f_ref[...]
    f = lb + (1.0 - lb) * jax.nn.sigmoid(fr)
    logf = jnp.log(jnp.maximum(f, TINY))
    k = (1.0 - lb) * jax.nn.sigmoid(-fr)
    qv = _silu(q_ref[...])
    vv = v_ref[...]
    yield
    brel = _ldot3(tri_ref[...], logf)
    tot = _ldot3(same_blk, logf)
    qkb_s[0] = qv
    qkb_s[1] = k
    qkb_s[2] = brel
    yield

    qt = (qv * jnp.exp(brel)).astype(BF16)
    kt = k * jnp.exp(tot - brel)
    dec_blk = jnp.exp(tot)
    lane = lax.broadcasted_iota(jnp.int32, (t, D_HG), 1)
    head_masks = [(lane >= h * HG_DK) & (lane < (h + 1) * HG_DK) for h in range(HG_HEADS)]
    kxs = [jnp.where(m, kt, 0.0).astype(BF16) for m in head_masks]
    vxs = [jnp.where(m, vv, 0.0).astype(BF16) for m in head_masks]
    for blk in range(nsub):
        r0 = blk * c
        kx = jnp.concatenate([a[r0:r0 + c, :] for a in kxs], axis=0)
        vx = jnp.concatenate([a[r0:r0 + c, :] for a in vxs], axis=0)
        upd_s[blk] = _dot_tn(vx, kx)
        if blk % 4 == 3:
            yield

    mid = c // 2
    bref = jnp.concatenate([jnp.broadcast_to(brel[b * c + mid:b * c + mid + 1, :], (c, D_HG))
                            for b in range(nsub)], axis=0)
    dev = brel - bref
    worst = jnp.max(jnp.max(jnp.abs(dev), axis=1, keepdims=True), axis=0, keepdims=True)
    safe_v = worst <= GLA_SAFE_EXPONENT
    safe = worst[0, 0] <= GLA_SAFE_EXPONENT
    qh = qv * jnp.exp(dev)
    kh = (k * jnp.exp(-dev)).astype(BF16)
    qx = jnp.concatenate([jnp.where(m, qh, 0.0).astype(BF16) for m in head_masks], axis=0)
    yield
    sc = _dot_nt(qx, kh)
    visible = tri_ref[...].astype(F32) > 0.5
    pm = jnp.concatenate([jnp.where(visible, sc[h * t:(h + 1) * t, :], 0.0).astype(BF16)
                          for h in range(HG_HEADS)], axis=0)
    yield
    oh = jnp.dot(pm, vv.astype(BF16), preferred_element_type=F32)
    o_fast = jnp.where(head_masks[0], oh[0:t, :], 0.0)
    for h in range(1, HG_HEADS):
        o_fast = o_fast + jnp.where(head_masks[h], oh[h * t:(h + 1) * t, :], 0.0)
    oi_s[...] = jnp.where(safe_v, o_fast, 0.0)
    yield

    @pl.when(jnp.logical_not(safe))
    def _():
        _gla_intra_pairwise(qkb_s, v_ref, ind_ref, p_s, oi_s, rev=rev)

    yield

    st = st_ref[...]
    o_inter = [None] * nsub
    for blk in (range(nsub - 1, -1, -1) if rev else range(nsub)):
        r0 = blk * c
        o_inter[blk] = _dot_nt(qt[r0:r0 + c, :], st.astype(BF16))
        st = st * dec_blk[r0:r0 + 1, :] + upd_s[blk]
        yield
    st_ref[...] = st
    o_ref[...] = oi_s[...] + jnp.concatenate(o_inter, axis=0)


def _gla_intra_pairwise(qkb_s, v_ref, ind_ref, p_s, oi_s, *, rev):
    t = GLA_TILE
    c = GLA_SUB
    nsub = t // c
    sub = SUBLANES
    qv, k, brel = qkb_s[0], qkb_s[1], qkb_s[2]
    vv = v_ref[...]
    t_idx = lax.broadcasted_iota(jnp.int32, (sub, D_HG), 0)
    pieces = [(s, g) for s in range(c) for g in range(c // sub)
              if (g <= s // sub if rev else g >= s // sub)]
    rows_per_blk = len(pieces) * sub

    brel2 = brel * LOG2E
    for blk in range(nsub):
        r0 = blk * c
        qb = qv[r0:r0 + c, :]
        kb = k[r0:r0 + c, :]
        bb = brel2[r0:r0 + c, :]
        for u in range(0, len(pieces), 2):
            rows = []
            for s, g in pieces[u:u + 2]:
                tt = t_idx + g * sub
                keep = (tt <= s) if rev else (tt >= s)
                dec = jnp.exp2(jnp.where(keep, bb[g * sub:(g + 1) * sub, :] - bb[s:s + 1, :], MASK_VALUE))
                rows.append(qb[g * sub:(g + 1) * sub, :] * kb[s:s + 1, :] * dec)
            po = blk * rows_per_blk + u * sub
            p_s[po:po + 2 * sub, :] = jnp.concatenate(rows, axis=0).astype(BF16)
    r = jnp.dot(p_s[...], ind_ref[...], preferred_element_type=F32)
    for blk in range(nsub):
        r0 = blk * c
        vb = vv[r0:r0 + c, :]
        accs = [jnp.zeros((sub, D_HG), F32) for _ in range(c // sub)]
        for n, (s, g) in enumerate(pieces):
            po = blk * rows_per_blk + n * sub
            accs[g] = accs[g] + r[po:po + sub, :] * vb[s:s + 1, :]
        oi_s[r0:r0 + c, :] = jnp.concatenate(accs, axis=0)


def _gla_kernel(qf_ref, ff_ref, vf_ref, qb_ref, fb_ref, vb_ref, lbf_ref, lbb_ref, trif_ref, trib_ref,
                ind_ref, of_ref, ob_ref, pf_s, pb_s, updf_s, updb_s, oif_s, oib_s, qkbf_s, qkbb_s,
                stf_ref, stb_ref):
    @pl.when(pl.program_id(1) == 0)
    def _():
        stf_ref[...] = jnp.zeros_like(stf_ref)
        stb_ref[...] = jnp.zeros_like(stb_ref)

    same_blk = jnp.maximum(trif_ref[...], trib_ref[...])
    _round_robin([
        _gla_direction(qf_ref, ff_ref, vf_ref, lbf_ref, trif_ref, same_blk, ind_ref, of_ref,
                       pf_s, updf_s, oif_s, qkbf_s, stf_ref, rev=False),
        _gla_direction(qb_ref, fb_ref, vb_ref, lbb_ref, trib_ref, same_blk, ind_ref, ob_ref,
                       pb_s, updb_s, oib_s, qkbb_s, stb_ref, rev=True)])


def _gla_call(hg, lb_f, lb_b, tri_f, tri_b, ind, *, n_rows, seq, nb):
    t = GLA_TILE
    blk_f, steps = _scan_block_map(seq=seq, nb=nb, q=t, rev=False)
    blk_b, _ = _scan_block_map(seq=seq, nb=nb, q=t, rev=True)
    const = lambda b, j: (0, 0)
    col = lambda blk, cidx: pl.BlockSpec((t, D_HG), lambda b, j: (blk(b, j), cidx))
    return pl.pallas_call(
        _gla_kernel,
        grid=(nb, steps),
        in_specs=[
            col(blk_f, 0), col(blk_f, 1), col(blk_f, 3),
            col(blk_b, 0), col(blk_b, 2), col(blk_b, 3),
            pl.BlockSpec((1, D_HG), const),
            pl.BlockSpec((1, D_HG), const),
            pl.BlockSpec((t, t), const),
            pl.BlockSpec((t, t), const),
            pl.BlockSpec((D_HG, D_HG), const),
        ],
        out_specs=[col(blk_f, 0), col(blk_b, 0)],
        out_shape=[jax.ShapeDtypeStruct((n_rows, D_HG), F32)] * 2,
        scratch_shapes=(
            [pltpu.VMEM((GLA_P_ROWS, D_HG), BF16)] * 2
            + [pltpu.VMEM((GLA_TILE // GLA_SUB, D_HG, D_HG), F32)] * 2
            + [pltpu.VMEM((GLA_TILE, D_HG), F32)] * 2
            + [pltpu.VMEM((3, GLA_TILE, D_HG), F32)] * 2
            + [pltpu.VMEM((D_HG, D_HG), F32)] * 2),
        compiler_params=_cparams(("arbitrary", "arbitrary")),
        name="gla",
    )(hg, hg, hg, hg, hg, hg, lb_f, lb_b, tri_f, tri_b, ind)


def _expand_heads(x, masks):
    return jnp.concatenate([jnp.where(m, x, jnp.zeros_like(x)) for m in masks], axis=0)


def _collapse_heads(o, masks, t):
    acc = jnp.where(masks[0], o[0:t, :], 0.0)
    for h in range(1, NA_HEADS):
        acc = acc + jnp.where(masks[h], o[h * t:(h + 1) * t, :], 0.0)
    return acc


def _na_finish(o, nw):
    ms = jnp.mean(o * o, axis=-1, keepdims=True)
    return (o * lax.rsqrt(ms + EPS)) * nw


def _na_kernel(q_ref, k_ref, v_ref, kc_ref, vc_ref, bias_ref, nw_ref, o_ref, *, n_grid_rows):
    w = GRID_W
    i = pl.program_id(1)
    lane = lax.broadcasted_iota(jnp.int32, (w, D_NA), 1)
    masks = [(lane >= h * NA_DH) & (lane < (h + 1) * NA_DH) for h in range(NA_HEADS)]
    kc = kc_ref[...]
    vc = vc_ref[...]
    nw = nw_ref[...]
    scale = NA_DH ** -0.5

    def query_row(jr):
        r = i * NA_ROWS + jr
        r0 = jnp.clip(r - WIN_R // 2, 0, n_grid_rows - WIN_R)
        var = r - r0
        k0 = pl.multiple_of(r0 * w, w)
        q0 = jr * w
        qx = _expand_heads(q_ref[q0:q0 + w, :] * scale, masks)
        kw = k_ref[pl.ds(k0, WIN_R * w), :]
        vw = v_ref[pl.ds(k0, WIN_R * w), :]
        s_loc = _dot_nt(qx, kw) + bias_ref[var]
        s_ctx = _dot_nt(qx, kc)
        yield
        m = jnp.maximum(jnp.max(s_loc, axis=-1, keepdims=True), jnp.max(s_ctx, axis=-1, keepdims=True))
        p_loc = jnp.exp(s_loc - m)
        p_ctx = jnp.exp(s_ctx - m)
        den = jnp.sum(p_loc, axis=-1, keepdims=True) + jnp.sum(p_ctx, axis=-1, keepdims=True)
        yield
        o = (jnp.dot(p_loc.astype(BF16), vw, preferred_element_type=F32)
             + jnp.dot(p_ctx.astype(BF16), vc, preferred_element_type=F32)) / den
        yield
        o_ref[q0:q0 + w, :] = _na_finish(_collapse_heads(o, masks, w), nw)

    for g0 in range(0, NA_ROWS, NA_INTERLEAVE):
        _round_robin([query_row(jr) for jr in range(g0, g0 + NA_INTERLEAVE)])


def _na_call(na, bias, nw, *, seq, nb):
    rows = seq // GRID_W
    tq = NA_ROWS * GRID_W
    qpb = seq // tq
    ctx_blk = nb * seq // CTX_LEN
    kern = functools.partial(_na_kernel, n_grid_rows=rows)
    return pl.pallas_call(
        kern,
        grid=(nb, qpb),
        in_specs=[
            pl.BlockSpec((tq, D_NA), lambda b, i: (b * qpb + i, 0)),
            pl.BlockSpec((seq, D_NA), lambda b, i: (b, 1)),
            pl.BlockSpec((seq, D_NA), lambda b, i: (b, 2)),
            pl.BlockSpec((CTX_LEN, D_NA), lambda b, i: (ctx_blk + b, 1)),
            pl.BlockSpec((CTX_LEN, D_NA), lambda b, i: (ctx_blk + b, 2)),
            pl.BlockSpec((WIN_R, NA_HEADS * GRID_W, WIN_R * GRID_W), lambda b, i: (0, 0, 0)),
            pl.BlockSpec((1, D_NA), lambda b, i: (0, 0)),
        ],
        out_specs=pl.BlockSpec((tq, D_NA), lambda b, i: (b * qpb + i, 0)),
        out_shape=jax.ShapeDtypeStruct((nb * seq, D_NA), F32),
        compiler_params=_cparams(("parallel", "arbitrary")),
        name="natten",
    )(na, na, na, na, na, bias, nw)


def _ctxattn_kernel(q_ref, k_ref, v_ref, nw_ref, o_ref):
    t = CTX_LEN
    lane = lax.broadcasted_iota(jnp.int32, (t, D_NA), 1)
    masks = [(lane >= h * NA_DH) & (lane < (h + 1) * NA_DH) for h in range(NA_HEADS)]
    qx = _expand_heads(q_ref[...] * (NA_DH ** -0.5), masks)
    s = _dot_nt(qx, k_ref[...])
    m = jnp.max(s, axis=-1, keepdims=True)
    p = jnp.exp(s - m)
    den = jnp.sum(p, axis=-1, keepdims=True)
    o = jnp.dot(p.astype(BF16), v_ref[...], preferred_element_type=F32) / den
    o_ref[...] = _na_finish(_collapse_heads(o, masks, t), nw_ref[...])


def _ctxattn_call(na, nw, *, seq, nb):
    ctx_blk = nb * seq // CTX_LEN
    return pl.pallas_call(
        _ctxattn_kernel,
        grid=(nb,),
        in_specs=[
            pl.BlockSpec((CTX_LEN, D_NA), lambda b: (ctx_blk + b, 0)),
            pl.BlockSpec((CTX_LEN, D_NA), lambda b: (ctx_blk + b, 1)),
            pl.BlockSpec((CTX_LEN, D_NA), lambda b: (ctx_blk + b, 2)),
            pl.BlockSpec((1, D_NA), lambda b: (0, 0)),
        ],
        out_specs=pl.BlockSpec((CTX_LEN, D_NA), lambda b: (b, 0)),
        out_shape=jax.ShapeDtypeStruct((nb * CTX_LEN, D_NA), F32),
        compiler_params=_cparams(("parallel",)),
        name="ctxattn",
    )(na, na, na, nw)


def _mixed_residual(x, mod_ref, of_ref, ob_ref, g_ref, na_ref, yf_ref, yb_ref, xs_ref, z_ref,
                    hgw_ref, hm_ref, dsk_ref, sw_ref, wo_ref, *, rows):
    hm = hm_ref[...]
    o = of_ref[rows, :] + ob_ref[rows, :]
    sq = o * o
    hi = sq.astype(BF16)
    lo = (sq - hi.astype(F32)).astype(BF16)
    ms = (jnp.dot(hi, hm, preferred_element_type=F32)
          + jnp.dot(lo, hm, preferred_element_type=F32)) * (1.0 / HG_DK)
    hg = (o * lax.rsqrt(ms + EPS)) * hgw_ref[...] * _silu(g_ref[rows, :])
    ys = (yf_ref[rows, :] + yb_ref[rows, :] + dsk_ref[...] * xs_ref[rows, :]) * _silu(z_ref[rows, :])
    ms2 = jnp.mean(ys * ys, axis=-1, keepdims=True)
    ssm = (ys * lax.rsqrt(ms2 + EPS)) * sw_ref[...]
    mix = (jnp.dot(hg.astype(BF16), wo_ref[0:D_HG, :], preferred_element_type=F32)
           + jnp.dot(na_ref[rows, :].astype(BF16), wo_ref[D_HG:D_HG + D_NA, :], preferred_element_type=F32)
           + jnp.dot(ssm.astype(BF16), wo_ref[D_HG + D_NA:, :], preferred_element_type=F32))
    return x + mod_ref[5:6, :] * mix


def _mix_specs(tm):
    row = lambda i: (i, 0)
    const = lambda i: (0, 0)
    return [
        pl.BlockSpec((tm, D_HG), row),
        pl.BlockSpec((tm, D_HG), row),
        pl.BlockSpec((tm, D_HG), lambda i: (i, 4)),
        pl.BlockSpec((tm, D_NA), row),
        pl.BlockSpec((tm, D_SSM), row),
        pl.BlockSpec((tm, D_SSM), row),
        pl.BlockSpec((tm, D_SSM), row),
        pl.BlockSpec((tm, D_SSM), row),
        pl.BlockSpec((1, D_HG), const),
        pl.BlockSpec((D_HG, D_HG), const),
        pl.BlockSpec((1, D_SSM), const),
        pl.BlockSpec((1, D_SSM), const),
        pl.BlockSpec((D_MODEL, D_MODEL), const, pipeline_mode=pl.Buffered(1)),
    ]


def _na_bias_table(rpb):
    w = GRID_W
    ndr = 2 * WIN_R - 1
    ndc = 2 * WIN_C - 1
    col = np.arange(w)
    c0 = np.clip(col - WIN_C // 2, 0, w - WIN_C)
    col_in = (col[None, :] >= c0[:, None]) & (col[None, :] < c0[:, None] + WIN_C)
    dc = np.clip(col[None, :] - col[:, None], -(WIN_C - 1), WIN_C - 1) + (WIN_C - 1)
    onehot = (dc.reshape(1, -1) == np.arange(ndc)[:, None]).astype(np.float32)
    t = jnp.dot(rpb.reshape(NA_HEADS * ndr, ndc).astype(F32), onehot,
                precision=lax.Precision.HIGHEST).reshape(NA_HEADS, ndr, w, w)
    t = jnp.where(col_in[None, None], t, MASK_VALUE)
    b = jnp.stack([t[:, WIN_R - 1 - var:2 * WIN_R - 1 - var] for var in range(WIN_R)], axis=0)
    b = jnp.transpose(b, (0, 1, 3, 2, 4))
    return b.reshape(WIN_R, NA_HEADS * w, WIN_R * w)


def _block_tri(n, c, rev):
    r = jnp.arange(n)
    same = (r[:, None] // c) == (r[None, :] // c)
    tri = (r[None, :] >= r[:, None]) if rev else (r[None, :] <= r[:, None])
    return (same & tri).astype(BF16)


def _head_block_ones(n, hd):
    r = jnp.arange(n)
    return ((r[:, None] // hd) == (r[None, :] // hd)).astype(BF16)


def _ssd_expand(lane0):
    r = jnp.arange(LANES)[:, None]
    cidx = jnp.arange(D_SSM)[None, :]
    e = (r == lane0 + cidx // SSM_P).astype(BF16)
    return jnp.concatenate([e, e, e], axis=0)


def _lane_row(vals):
    return jnp.zeros((1, LANES), F32).at[0, :vals.shape[0]].set(vals.astype(F32))


def kernel(x, c, ctx, c_ctx, w_mod, b_mod, norm_ffn1, ffn1_w13, ffn1_w2, norm_mix, w_in,
           hg_lower_bounds, hg_norm, na_rpb, na_norm, ssm_conv_w, ssm_conv_b, ssm_a_log,
           ssm_dt_bias, ssm_d, ssm_norm, w_out, norm_ffn2, ffn2_w13, ffn2_w2, final_norm):
    nb, seq, d = x.shape
    depth = w_mod.shape[0]
    assert d == D_MODEL and ctx.shape[1] == CTX_LEN and nb + 1 <= MOD_ROWS
    assert seq % 512 == 0 and seq // GRID_W >= WIN_R
    n_lat = nb * seq
    n_all = n_lat + nb * CTX_LEN

    lb_soft = jax.nn.softmax(hg_lower_bounds.astype(F32), axis=1)
    lower_bounds = jnp.cumsum(lb_soft, axis=1) - lb_soft[:, :1]

    c_rows = jnp.zeros((MOD_ROWS, d), F32).at[:nb].set(c).at[nb].set(c_ctx)
    mods = _mods_call(c_rows, w_mod, b_mod).reshape(depth, MOD_ROWS, N_MOD, d)

    x_all = x.reshape(n_lat, d)
    x_ctx = ctx.reshape(nb * CTX_LEN, d)

    tri_f = _block_tri(GLA_TILE, GLA_SUB, False)
    tri_b = _block_tri(GLA_TILE, GLA_SUB, True)
    ind = _head_block_ones(D_HG, HG_DK)
    ssd_tri_f = _block_tri(SSD_Q, SSD_Q, False)
    ssd_tri_b = _block_tri(SSD_Q, SSD_Q, True)
    e_f = _ssd_expand(0)
    e_b = _ssd_expand(SSM_HEADS)
    one_row = lambda v: v.reshape(1, -1).astype(F32)
    common = dict(seq=seq, nb=nb)

    for layer in range(depth):
        last = layer == depth - 1
        mods_l = mods[layer]
        w13_1 = _cast_call(ffn1_w13, layer)
        w2_1 = _cast_call(ffn1_w2, layer)
        w13_2 = _cast_call(ffn2_w13, layer)
        w2_2 = _cast_call(ffn2_w2, layer)
        w_in_b = _cast_call(w_in, layer, out_cols=D_IN_PAD)
        wo_b = _cast_call(w_out, layer)
        fw = one_row(final_norm)

        x_all = _ffn_call(x_all, mods_l, one_row(norm_ffn1[layer]), w13_1, w2_1, fw,
                          n_rows=n_all, ks=(0, 1, 2), final=False, x_ctx=x_ctx if layer == 0 else None,
                          **common)
        hg, na, z, xbc_c, dt_raw = _inproj_call(x_all, mods_l, one_row(norm_mix[layer]), w_in_b,
                                                ssm_conv_w[layer].astype(F32), one_row(ssm_conv_b[layer]),
                                                n_rows=n_all, **common)

        o_f, o_b = _gla_call(hg, one_row(lower_bounds[0, layer]), one_row(lower_bounds[1, layer]),
                             tri_f, tri_b, ind, n_rows=n_all, **common)

        bias = _na_bias_table(na_rpb[layer])
        nw = one_row(na_norm[layer])
        y_na = _na_call(na, bias, nw, **common)
        if not last:
            y_na = jnp.concatenate([y_na, _ctxattn_call(na, nw, **common)], axis=0)

        a_neg = -jnp.exp(ssm_a_log[layer].astype(F32))
        y_f, y_b = _ssd_call(xbc_c, dt_raw, _lane_row(a_neg.reshape(-1)),
                             _lane_row(ssm_dt_bias[layer].reshape(-1)),
                             e_f, e_b, ssd_tri_f, ssd_tri_b, n_rows=n_all, **common)

        n_out = n_lat if last else n_all
        dsk = jnp.repeat(ssm_d[layer].astype(F32), SSM_P).reshape(1, D_SSM)
        mix = (o_f, o_b, hg, y_na, y_f, y_b, xbc_c, z,
               one_row(hg_norm[layer]), ind, dsk, one_row(ssm_norm[layer]), wo_b)
        assert len(mix) == N_MIX_OPERANDS
        x_all = _ffn_call(x_all, mods_l, one_row(norm_ffn2[layer]), w13_2, w2_2, fw,
                          n_rows=n_out, ks=(6, 7, 8), final=last, mix=mix, **common)

    return x_all.reshape(nb, seq, d)
```
